```python
import jax, jax.numpy as jnp
from jax import lax
import numpy as np

D_MODEL = 1024
BATCH = 8
SEQ = 8192
DEPTH = 2

PLE_DIM = 256
HEAD_DIM = 64
BRANCH_WIDTH = D_MODEL // 2
N_BRANCHES = 3
NSA_HEADS = BRANCH_WIDTH // HEAD_DIM
NSA_KV_GROUPS = 2
NSA_HPG = NSA_HEADS // NSA_KV_GROUPS
NSA_KV_WIDTH = NSA_KV_GROUPS * HEAD_DIM
NSA_GATE_WIDTH = 3 * NSA_HEADS
CMP_BLOCK = 32
CMP_STRIDE = 16
CMP_HIDDEN = 2 * HEAD_DIM
SEL_BLOCK = 64
SEL_TOPK = 16
WINDOW = 512
Q_BLOCK = 128
SG_GROUPS = 8
SG_CHUNK = 128
RWKV_HEADS = BRANCH_WIDTH // HEAD_DIM
DECAY_LORA = 64
AAA_LORA = 64
RK_SHIFT_WIDTH = 3 * BRANCH_WIDTH + DECAY_LORA + AAA_LORA
IN_SIZES = (BRANCH_WIDTH, 6 * NSA_KV_WIDTH, NSA_GATE_WIDTH, BRANCH_WIDTH,
            BRANCH_WIDTH, BRANCH_WIDTH, BRANCH_WIDTH,
            RK_SHIFT_WIDTH, BRANCH_WIDTH, N_BRANCHES * D_MODEL)
N_IN = 7 * BRANCH_WIDTH + 6 * NSA_KV_WIDTH + NSA_GATE_WIDTH + RK_SHIFT_WIDTH + N_BRANCHES * D_MODEL

NORM_EPS = 1e-6
LN_EPS = 1e-5
GN_EPS = 64e-5
MASK_NEG = -1e30
FORCE_BONUS = 1e4

kernel_name = "hybrid_nsa_sgmlp_rwkv7_block"


def _rmsnorm(x, g):
    xf = x.astype(jnp.float32)
    y = xf * lax.rsqrt(jnp.mean(xf * xf, axis=-1, keepdims=True) + NORM_EPS)
    return y.astype(x.dtype) * g


def _masked_softmax(s, mask):
    s = jnp.where(mask, s.astype(jnp.float32), MASK_NEG)
    return jnp.where(mask, jax.nn.softmax(s, axis=-1), 0.0)


def _split_cols(proj):
    out, off = [], 0
    for n in IN_SIZES:
        out.append(proj[..., off:off + n])
        off += n
    return out


def _compress(kv, w1, w2, pe):
    S = kv.shape[1]
    n_cmp = (S - CMP_BLOCK) // CMP_STRIDE + 1
    idx = jnp.arange(n_cmp)[:, None] * CMP_STRIDE + jnp.arange(CMP_BLOCK)[None, :]
    blk = kv[:, idx] + pe[:, None, :]
    hid = jax.nn.silu(jnp.einsum('bnlgd,ldh->bngh', blk, w1))
    return jnp.einsum('bngh,hd->bngd', hid, w2)


def _nsa(q, kc, vc, ks, vs, kw, vw, gate, cmp_w1, cmp_w2, cmp_pe):
    B, S, _ = q.shape
    G, HPG, Dh = NSA_KV_GROUPS, NSA_HPG, HEAD_DIM
    q = q.reshape(B, S, G, HPG, Dh) * (Dh ** -0.5)
    gate = jax.nn.sigmoid(gate).reshape(B, S, G, HPG, 3)
    k_cmp = _compress(kc.reshape(B, S, G, Dh), cmp_w1[0], cmp_w2[0], cmp_pe[0])
    v_cmp = _compress(vc.reshape(B, S, G, Dh), cmp_w1[1], cmp_w2[1], cmp_pe[1])
    n_cmp = k_cmp.shape[1]
    cmp_start = jnp.arange(n_cmp) * CMP_STRIDE
    cmp_end = cmp_start + CMP_BLOCK - 1
    n_slc = S // SEL_BLOCK
    top_k = min(SEL_TOPK, n_slc)
    slc_start = jnp.arange(n_slc) * SEL_BLOCK
    cover = ((cmp_start[:, None] <= slc_start[None, :] + SEL_BLOCK - 1)
             & (cmp_end[:, None] >= slc_start[None, :])).astype(jnp.float32)
    k_slc = ks.reshape(B, n_slc, SEL_BLOCK, G, Dh).transpose(0, 3, 1, 2, 4)
    v_slc = vs.reshape(B, n_slc, SEL_BLOCK, G, Dh).transpose(0, 3, 1, 2, 4)
    k_win = jnp.pad(kw.reshape(B, S, G, Dh), ((0, 0), (WINDOW, 0), (0, 0), (0, 0)))
    v_win = jnp.pad(vw.reshape(B, S, G, Dh), ((0, 0), (WINDOW, 0), (0, 0), (0, 0)))
    gather_blocks = jax.vmap(jax.vmap(lambda blocks, ix: blocks[ix]))
    j = jnp.arange(n_slc)

    def block(qb):
        q0 = qb * Q_BLOCK
        qblk = lax.dynamic_slice_in_dim(q, q0, Q_BLOCK, axis=1)
        gblk = lax.dynamic_slice_in_dim(gate, q0, Q_BLOCK, axis=1)
        t = q0 + jnp.arange(Q_BLOCK)
        valid_c = cmp_end[None, :] <= t[:, None]
        p_c = _masked_softmax(jnp.einsum('bqghd,bngd->bghqn', qblk, k_cmp), valid_c)
        o_c = jnp.einsum('bghqn,bngd->bqghd', p_c.astype(q.dtype), v_cmp)
        imp = jnp.einsum('bghqn,nj->bgqj', p_c, cover)
        t_blk = t // SEL_BLOCK
        forced = (j[None, :] == 0) | (j[None, :] == t_blk[:, None]) | (j[None, :] == t_blk[:, None] - 1)
        causal_blk = slc_start[None, :] <= t[:, None]
        score = jnp.where(causal_blk, jnp.where(forced, FORCE_BONUS, imp), -FORCE_BONUS)
        _, sel = lax.top_k(score, top_k)
        k_sel = gather_blocks(k_slc, sel).reshape(B, G, Q_BLOCK, top_k * SEL_BLOCK, Dh)
        v_sel = gather_blocks(v_slc, sel).reshape(B, G, Q_BLOCK, top_k * SEL_BLOCK, Dh)
        pos_sel = (sel[..., None] * SEL_BLOCK + jnp.arange(SEL_BLOCK)).reshape(B, G, Q_BLOCK, top_k * SEL_BLOCK)
        valid_s = (pos_sel <= t[None, None, :, None])[:, :, None]
        p_s = _masked_softmax(jnp.einsum('bqghd,bgqkd->bghqk', qblk, k_sel), valid_s)
        o_s = jnp.einsum('bghqk,bgqkd->bqghd', p_s.astype(q.dtype), v_sel)
        kwb = lax.dynamic_slice_in_dim(k_win, q0, Q_BLOCK + WINDOW, axis=1)
        vwb = lax.dynamic_slice_in_dim(v_win, q0, Q_BLOCK + WINDOW, axis=1)
        pos_w = q0 - WINDOW + jnp.arange(Q_BLOCK + WINDOW)
        valid_w = ((pos_w[None, :] <= t[:, None]) & (pos_w[None, :] > t[:, None] - WINDOW)
                   & (pos_w[None, :] >= 0))
        p_w = _masked_softmax(jnp.einsum('bqghd,bkgd->bghqk', qblk, kwb), valid_w)
        o_w = jnp.einsum('bghqk,bkgd->bqghd', p_w.astype(q.dtype), vwb)
        return gblk[..., 0:1] * o_c + gblk[..., 1:2] * o_s + gblk[..., 2:3] * o_w

    out = lax.map(block, jnp.arange(S // Q_BLOCK))
    return jnp.moveaxis(out, 0, 1).reshape(B, S, G * HPG * Dh)


def _spatial_gating(u, v, ln_g, ln_b, w_s, b_s):
    B, S, C = v.shape
    vf = v.astype(jnp.float32)
    mu = jnp.mean(vf, axis=-1, keepdims=True)
    var = jnp.var(vf, axis=-1, keepdims=True)
    vn = ((vf - mu) * lax.rsqrt(var + LN_EPS)).astype(v.dtype) * ln_g + ln_b
    vn = vn.reshape(B, S // SG_CHUNK, SG_CHUNK, SG_GROUPS, C // SG_GROUPS)
    causal = jnp.tril(jnp.ones((SG_CHUNK, SG_CHUNK), dtype=bool))
    w = jnp.where(causal, w_s, 0.0)
    mixed = jnp.einsum('gts,bcsgd->bctgd', w, vn) + b_s.T[:, :, None]
    return u * mixed.reshape(B, S, C)


def _rwkv7(xs, mu, w0, w2, a0, a2, k_k, k_a, r_k, lnx_g, lnx_b):
    B, S, _ = xs.shape
    C, H, N = BRANCH_WIDTH, RWKV_HEADS, HEAD_DIM
    prev = jnp.pad(xs, ((0, 0), (1, 0), (0, 0)))[:, :-1]
    xs = xs + (prev - xs) * mu
    r, k, v, wl, al = jnp.split(xs, [C, 2 * C, 3 * C, 3 * C + DECAY_LORA], axis=-1)
    w = -jax.nn.softplus(-(w0 + jnp.tanh(wl) @ w2)) - 0.5
    a = jax.nn.sigmoid(a0 + al @ a2)
    r, k, v, w, a = (z.reshape(B, S, H, N) for z in (r, k, v, w, a))
    kkf = (k * k_k).astype(jnp.float32)
    kk = kkf / jnp.maximum(jnp.sqrt(jnp.sum(kkf * kkf, axis=-1, keepdims=True)), 1e-12)
    k = k * (1 + (a - 1) * k_a)
    decay = jnp.exp(-jnp.exp(w.astype(jnp.float32)))

    def step(state, inp):
        r_t, d_t, k_t, v_t, a_t, b_t = inp
        sa = jnp.einsum('bhij,bhj->bhi', state, a_t)
        state = (state * d_t[:, :, None, :] + sa[..., None] * b_t[:, :, None, :]
                 + v_t[..., None] * k_t[:, :, None, :])
        return state, jnp.einsum('bhij,bhj->bhi', state, r_t)

    seq = lambda z: jnp.moveaxis(z.astype(jnp.float32), 1, 0)
    state0 = jnp.zeros((B, H, N, N), jnp.float32)
    _, y = lax.scan(step, state0, (seq(r), seq(decay), seq(k), seq(v), seq(-kk), seq(kk * a)))
    y = jnp.moveaxis(y, 0, 1)
    m = jnp.mean(y, axis=-1, keepdims=True)
    var = jnp.var(y, axis=-1, keepdims=True)
    y = ((y - m) * lax.rsqrt(var + GN_EPS)).reshape(B, S, C) * lnx_g + lnx_b
    bonus = (jnp.sum(r * k * r_k, axis=-1, keepdims=True) * v).reshape(B, S, C)
    return (y + bonus).astype(xs.dtype)


def setup_inputs(seed: int = 0) -> dict:
    key = jax.random.key(seed)
    ks = iter(jax.random.split(key, 32))
    nrm = lambda shape, scale: jax.random.normal(next(ks), shape, jnp.float32) * scale
    L, D, W = DEPTH, D_MODEL, BRANCH_WIDTH
    return {
        "x": nrm((BATCH, SEQ, D), 1.0),
        "p": nrm((DEPTH, BATCH, SEQ, PLE_DIM), 1.0),
        "norm_g": 1.0 + nrm((L, D), 0.02),
        "w_in": nrm((L, D, N_IN), D ** -0.5),
        "cmp_w1": nrm((L, 2, CMP_BLOCK, HEAD_DIM, CMP_HIDDEN), (CMP_BLOCK * HEAD_DIM) ** -0.5),
        "cmp_w2": nrm((L, 2, CMP_HIDDEN, HEAD_DIM), CMP_HIDDEN ** -0.5),
        "cmp_pe": nrm((L, 2, CMP_BLOCK, HEAD_DIM), 0.5),
        "sg_ln_g": 1.0 + nrm((L, W), 0.02),
        "sg_ln_b": nrm((L, W), 0.02),
        "sg_w": nrm((L, SG_GROUPS, SG_CHUNK, SG_CHUNK), SG_CHUNK ** -0.5),
        "sg_b": 1.0 + nrm((L, SG_GROUPS, SG_CHUNK), 0.02),
        "rk_mu": jax.random.uniform(next(ks), (L, RK_SHIFT_WIDTH), jnp.float32),
        "rk_w0": -1.0 + nrm((L, W), 0.5),
        "rk_w2": nrm((L, DECAY_LORA, W), 0.5 * DECAY_LORA ** -0.5),
        "rk_a0": nrm((L, W), 0.1),
        "rk_a2": nrm((L, AAA_LORA, W), 0.5 * AAA_LORA ** -0.5),
        "rk_kk": 0.85 + nrm((L, RWKV_HEADS, HEAD_DIM), 0.02),
        "rk_ka": 1.0 + nrm((L, RWKV_HEADS, HEAD_DIM), 0.02),
        "rk_rk": nrm((L, RWKV_HEADS, HEAD_DIM), 0.1),
        "rk_lnx_g": 1.0 + nrm((L, W), 0.02),
        "rk_lnx_b": nrm((L, W), 0.02),
        "w_branch": nrm((L, N_BRANCHES, W, D), W ** -0.5),
        "w_o": nrm((L, D, D), D ** -0.5),
        "ple_norm_g": 1.0 + nrm((L, D), 0.02),
        "w_ple_gate": nrm((L, D, D), D ** -0.5),
        "w_ple_proj": nrm((L, PLE_DIM, D), PLE_DIM ** -0.5),
        "final_norm_g": 1.0 + nrm((D,), 0.02),
    }


def reference(x, p, norm_g, w_in, cmp_w1, cmp_w2, cmp_pe, sg_ln_g, sg_ln_b, sg_w, sg_b,
              rk_mu, rk_w0, rk_w2, rk_a0, rk_a2, rk_kk, rk_ka, rk_rk, rk_lnx_g, rk_lnx_b,
              w_branch, w_o, ple_norm_g, w_ple_gate, w_ple_proj, final_norm_g):
    B, S, D = x.shape
    for i in range(DEPTH):
        h = _rmsnorm(x, norm_g[i])
        proj = h @ w_in[i]
        nq, nkv, ngate, nz, su, sv, sz, rs, rz, mg = _split_cols(proj)
        nkc, nvc, nks, nvs, nkw, nvw = jnp.split(nkv, 6, axis=-1)
        y_nsa = _nsa(nq, nkc, nvc, nks, nvs, nkw, nvw, ngate, cmp_w1[i], cmp_w2[i], cmp_pe[i])
        y_sg = _spatial_gating(su, sv, sg_ln_g[i], sg_ln_b[i], sg_w[i], sg_b[i])
        y_rk = _rwkv7(rs, rk_mu[i], rk_w0[i], rk_w2[i], rk_a0[i], rk_a2[i], rk_kk[i], rk_ka[i],
                      rk_rk[i], rk_lnx_g[i], rk_lnx_b[i])
        ys = jnp.stack([y_nsa * jax.nn.silu(nz), y_sg * jax.nn.silu(sz), y_rk * jax.nn.silu(rz)], axis=2)
        zs = jnp.einsum('bsnc,ncd->bsnd', ys, w_branch[i])
        merge = jax.nn.sigmoid(mg).reshape(B, S, N_BRANCHES, D)
        x = x + jnp.sum(merge * zs, axis=2) @ w_o[i]
        hp = _rmsnorm(x, ple_norm_g[i])
        x = x + jax.nn.sigmoid(hp @ w_ple_gate[i]) * (p[i] @ w_ple_proj[i])
    return _rmsnorm(x, final_norm_g)
```

```python
import functools

import jax
import jax.numpy as jnp
from jax import lax
from jax.experimental import pallas as pl
from jax.experimental.pallas import tpu as pltpu

F32 = jnp.float32
BF16 = jnp.bfloat16

HEAD_DIM = 64
LANES = 128
NSA_HEADS = 8
NSA_GROUPS = 2
NSA_HPG = NSA_HEADS // NSA_GROUPS
CMP_BLOCK = 32
CMP_STRIDE = 16
CMP_HIDDEN = 128
SEL_BLOCK = 64
SEL_TOPK = 16
WINDOW = 512
Q_TILE = 128
SEL_KEY_TILE = 512
SG_GROUPS = 8
SG_CHUNK = 128
RWKV_HEADS = 8
RWKV_CHUNK = 64
LORA = 64
NORM_EPS = 1e-6
LN_EPS = 1e-5
GN_EPS = 64e-5
MASK_NEG = -1e30
FORCE_BONUS = 1e4
VMEM_LIMIT = 56 * 1024 * 1024

HIGHEST = lax.Precision.HIGHEST


def _cparams(*sem):
    return pltpu.CompilerParams(dimension_semantics=sem, vmem_limit_bytes=VMEM_LIMIT)


def _dot(a, b, precision=None):
    return jnp.dot(a, b, preferred_element_type=F32, precision=precision)


def _dot_nt(a, b, precision=None):
    return lax.dot_general(a, b, (((1,), (1,)), ((), ())), preferred_element_type=F32,
                           precision=precision)


def _dot_tn(a, b, precision=None):
    return lax.dot_general(a, b, (((0,), (0,)), ((), ())), preferred_element_type=F32,
                           precision=precision)


def _split3(x):
    hi = x.astype(BF16)
    r1 = x - hi.astype(F32)
    mid = r1.astype(BF16)
    lo = (r1 - mid.astype(F32)).astype(BF16)
    return hi, mid, lo


def _dot3_lhs(x, w):
    hi, mid, lo = _split3(x)
    return _dot(hi, w) + _dot(mid, w) + _dot(lo, w)


def _dot3_rhs(w, x):
    hi, mid, lo = _split3(x)
    return _dot(w, hi) + _dot(w, mid) + _dot(w, lo)


def _sigmoid(x):
    return 1.0 / (1.0 + jnp.exp(-x))


def _silu(x):
    return x * _sigmoid(x)


def _rmsnorm(x, g):
    return x * lax.rsqrt(jnp.mean(x * x, axis=-1, keepdims=True) + NORM_EPS) * g


def _proj32_kernel(x_ref, g_ref, w_ref, o_ref, h_scr):
    @pl.when(pl.program_id(1) == 0)
    def _():
        h_scr[...] = _rmsnorm(x_ref[...], g_ref[...]).astype(BF16)

    o_ref[...] = _dot(h_scr[...], w_ref[...])


def _proj32(x2, g, w, tm=1024, tn=1024):
    t, d = x2.shape
    n = w.shape[1]
    return pl.pallas_call(
        _proj32_kernel,
        grid=(t // tm, n // tn),
        in_specs=[pl.BlockSpec((tm, d), lambda i, j: (i, 0)),
                  pl.BlockSpec((1, d), lambda i, j: (0, 0)),
                  pl.BlockSpec((d, tn), lambda i, j: (0, j))],
        out_specs=pl.BlockSpec((tm, tn), lambda i, j: (i, j)),
        out_shape=jax.ShapeDtypeStruct((t, n), F32),
        scratch_shapes=[pltpu.VMEM((tm, d), BF16)],
        compiler_params=_cparams("parallel", "arbitrary"),
        name="proj32",
    )(x2, g, w)


P16_Q = 0
P16_KX = NSA_HEADS * LANES
P16_VX = P16_KX + NSA_GROUPS * 2 * LANES
P16_KW = P16_VX + NSA_GROUPS * LANES
P16_VW = P16_KW + NSA_GROUPS * LANES
P16_N = P16_VW + NSA_GROUPS * LANES


def _proj16_kernel(x_ref, g_ref, w_ref, o_ref, *, seq, tm):
    h = _rmsnorm(x_ref[...], g_ref[...]).astype(BF16)
    pos = (pl.program_id(0) * tm) % seq + lax.broadcasted_iota(jnp.int32, (tm, LANES), 0)
    lane = lax.broadcasted_iota(jnp.int32, (tm, LANES), 1)
    onehot_blk = jnp.where(pos // SEL_BLOCK == lane, 1.0, 0.0)
    ones_col = jnp.where(lane == HEAD_DIM, 1.0, 0.0)
    for c0 in range(0, P16_N, LANES):
        y = _dot(h, w_ref[:, c0:c0 + LANES])
        if P16_KX <= c0 < P16_VX and ((c0 - P16_KX) // LANES) % 2 == 1:
            y = y + onehot_blk
        elif c0 >= P16_VX and not (P16_KW <= c0 < P16_VW):
            y = y + ones_col
        o_ref[:, c0:c0 + LANES] = y.astype(BF16)


def _proj16(x2, g, w, seq, tm=512):
    t, d = x2.shape
    return pl.pallas_call(
        functools.partial(_proj16_kernel, seq=seq, tm=tm),
        grid=(t // tm,),
        in_specs=[pl.BlockSpec((tm, d), lambda i: (i, 0)),
                  pl.BlockSpec((1, d), lambda i: (0, 0)),
                  pl.BlockSpec((d, P16_N), lambda i: (0, 0))],
        out_specs=pl.BlockSpec((tm, P16_N), lambda i: (i, 0)),
        out_shape=jax.ShapeDtypeStruct((t, P16_N), BF16),
        compiler_params=_cparams("parallel"),
        name="proj16",
    )(x2, g, w)


def _compress_kernel(x_ref, pe_ref, wa_ref, wb_ref, w2_ref, k_ref, v_ref):
    x = x_ref[0]
    n_grp = x.shape[0]
    a = _dot((x + pe_ref[0:1, :]).astype(BF16), wa_ref[...])
    b = _dot((x + pe_ref[1:2, :]).astype(BF16), wb_ref[...])
    hid = _silu(a + pltpu.roll(b, n_grp - 1, 0)).astype(BF16)
    row = lax.broadcasted_iota(jnp.int32, (n_grp, LANES), 0)
    lane = lax.broadcasted_iota(jnp.int32, (n_grp, LANES), 1)
    live = row < n_grp - 1
    for kv, out_ref in enumerate((k_ref, v_ref)):
        for g in range(NSA_GROUPS):
            c0 = (kv * NSA_GROUPS + g) * CMP_HIDDEN
            y = _dot(hid[:, c0:c0 + CMP_HIDDEN], w2_ref[kv])
            if kv == 1:
                y = y + jnp.where(lane == HEAD_DIM, 1.0, 0.0)
            out_ref[0, :, g * LANES:(g + 1) * LANES] = jnp.where(live, y, 0.0).astype(BF16)


def _compress(xg, pe2, wa, wb, w2p):
    b, n_grp, width = xg.shape
    out = jax.ShapeDtypeStruct((b, n_grp, NSA_GROUPS * LANES), BF16)
    return pl.pallas_call(
        _compress_kernel,
        grid=(b,),
        in_specs=[pl.BlockSpec((1, n_grp, width), lambda i: (i, 0, 0)),
                  pl.BlockSpec(pe2.shape, lambda i: (0, 0)),
                  pl.BlockSpec(wa.shape, lambda i: (0, 0)),
                  pl.BlockSpec(wb.shape, lambda i: (0, 0)),
                  pl.BlockSpec(w2p.shape, lambda i: (0, 0, 0))],
        out_specs=[pl.BlockSpec((1, n_grp, NSA_GROUPS * LANES), lambda i: (i, 0, 0))] * 2,
        out_shape=[out, out],
        compiler_params=_cparams("parallel"),
        name="nsa_compress",
    )(xg, pe2, wa, wb, w2p)


def _stack_heads(q_ref, g):
    parts = [q_ref[:, (g * NSA_HPG + h) * LANES:(g * NSA_HPG + h + 1) * LANES]
             for h in range(NSA_HPG)]
    return jnp.concatenate(parts, axis=0) * jnp.asarray(HEAD_DIM ** -0.5, BF16)


def _store_heads(o_ref, g, o):
    lane = lax.broadcasted_iota(jnp.int32, (Q_TILE, LANES), 1)
    for h in range(NSA_HPG):
        c0 = (g * NSA_HPG + h) * LANES
        o_ref[:, c0:c0 + LANES] = jnp.where(lane < HEAD_DIM, o[h * Q_TILE:(h + 1) * Q_TILE], 0.0)


def _cmp_select_kernel(q_ref, kc_ref, vc_ref, cov_ref, o_ref, bias_ref, *, top_k):
    n_cmp = kc_ref.shape[1]
    n_slc = cov_ref.shape[0]
    q0 = pl.program_id(1) * Q_TILE
    rows = NSA_HPG * Q_TILE
    t_row = q0 + lax.broadcasted_iota(jnp.int32, (rows, n_cmp), 0) % Q_TILE
    cmp_end = lax.broadcasted_iota(jnp.int32, (rows, n_cmp), 1) * CMP_STRIDE + CMP_BLOCK - 1
    valid = cmp_end <= t_row
    j_idx = lax.broadcasted_iota(jnp.int32, (n_slc, Q_TILE), 0)
    t_lane = q0 + lax.broadcasted_iota(jnp.int32, (n_slc, Q_TILE), 1)
    t_blk = t_lane // SEL_BLOCK
    forced = (j_idx == 0) | (j_idx == t_blk) | (j_idx == t_blk - 1)
    causal_blk = j_idx <= t_blk
    for g in range(NSA_GROUPS):
        qg = _stack_heads(q_ref, g)
        s = _dot_nt(qg, kc_ref[0, :, g * LANES:(g + 1) * LANES])
        s = jnp.where(valid, s, MASK_NEG)
        m = jnp.max(s, axis=-1, keepdims=True)
        p = jnp.where(valid, jnp.exp(s - m), 0.0)
        l = jnp.sum(p, axis=-1, keepdims=True)
        p = p * jnp.where(l > 0.0, 1.0 / l, 0.0)
        o = _dot(p.astype(BF16), vc_ref[0, :, g * LANES:(g + 1) * LANES])
        _store_heads(o_ref, g, o)
        p_sum = p[0:Q_TILE]
        for h in range(1, NSA_HPG):
            p_sum = p_sum + p[h * Q_TILE:(h + 1) * Q_TILE]
        hi, mid, lo = _split3(p_sum)
        cov = cov_ref[...]
        imp_t = _dot_nt(cov, hi) + _dot_nt(cov, mid) + _dot_nt(cov, lo)
        score = jnp.where(causal_blk, jnp.where(forced, FORCE_BONUS, imp_t), -FORCE_BONUS)

        def pick(_, carry):
            sc, sel = carry
            best = jnp.max(sc, axis=0, keepdims=True)
            first = jnp.min(jnp.where(sc == best, j_idx, n_slc), axis=0, keepdims=True)
            hit = j_idx == first
            return jnp.where(hit, -jnp.inf, sc), jnp.where(hit, 1.0, sel)

        _, sel = lax.fori_loop(0, top_k, pick,
                               (score, jnp.zeros((n_slc, Q_TILE), F32)))
        bias_t = jnp.where((sel > 0.0) & causal_blk, 0.0, MASK_NEG)
        bias_ref[:, g * n_slc:(g + 1) * n_slc] = bias_t.T.astype(BF16)


def _cmp_select(p16, kc, vc, cov_t, batch, seq):
    n_cmp = kc.shape[1]
    n_slc = cov_t.shape[0]
    nq = seq // Q_TILE
    return pl.pallas_call(
        functools.partial(_cmp_select_kernel, top_k=min(SEL_TOPK, seq // SEL_BLOCK)),
        grid=(batch, nq),
        in_specs=[pl.BlockSpec((Q_TILE, NSA_HEADS * LANES), lambda b, i: (b * nq + i, 0)),
                  pl.BlockSpec((1, n_cmp, NSA_GROUPS * LANES), lambda b, i: (b, 0, 0)),
                  pl.BlockSpec((1, n_cmp, NSA_GROUPS * LANES), lambda b, i: (b, 0, 0)),
                  pl.BlockSpec(cov_t.shape, lambda b, i: (0, 0))],
        out_specs=[pl.BlockSpec((Q_TILE, NSA_HEADS * LANES), lambda b, i: (b * nq + i, 0)),
                   pl.BlockSpec((Q_TILE, NSA_GROUPS * n_slc), lambda b, i: (b * nq + i, 0))],
        out_shape=[jax.ShapeDtypeStruct((batch * seq, NSA_HEADS * LANES), F32),
                   jax.ShapeDtypeStruct((batch * seq, NSA_GROUPS * n_slc), BF16)],
        compiler_params=_cparams("parallel", "arbitrary"),
        name="nsa_cmp_select",
    )(p16, kc, vc, cov_t)


def _sel_attn_kernel(q_ref, bias_ref, kx_ref, vx_ref, o_ref, *, tk):
    n_slc = bias_ref.shape[1] // NSA_GROUPS
    q0 = pl.program_id(1) * Q_TILE
    rows = NSA_HPG * Q_TILE
    n_tiles = (q0 + Q_TILE + tk - 1) // tk
    t_row = q0 + lax.broadcasted_iota(jnp.int32, (rows, tk), 0) % Q_TILE
    key_lane = lax.broadcasted_iota(jnp.int32, (rows, tk), 1)
    for g in range(NSA_GROUPS):
        qg = _stack_heads(q_ref, g)
        bias = bias_ref[:, g * n_slc:(g + 1) * n_slc]
        qx = jnp.concatenate([qg, jnp.concatenate([bias] * NSA_HPG, axis=0)], axis=1)

        def tile(kt, carry):
            m, acc = carry
            k0 = pl.multiple_of(kt * tk, tk)
            kx = kx_ref[pl.ds(k0, tk), g * 2 * LANES:(g + 1) * 2 * LANES]
            s = _dot_nt(qx, kx)
            s = jnp.where(k0 + key_lane <= t_row, s, MASK_NEG)
            m_new = jnp.maximum(m, jnp.max(s, axis=-1, keepdims=True))
            p = jnp.exp(s - m_new).astype(BF16)
            acc = jnp.exp(m - m_new) * acc + _dot(p, vx_ref[pl.ds(k0, tk), g * LANES:(g + 1) * LANES])
            return m_new, acc

        m0 = jnp.full((rows, 1), MASK_NEG, F32)
        _, acc = lax.fori_loop(0, n_tiles, tile, (m0, jnp.zeros((rows, LANES), F32)))
        _store_heads(o_ref, g, acc / acc[:, HEAD_DIM:HEAD_DIM + 1])


def _sel_attn(p16, bias, batch, seq):
    nq = seq // Q_TILE
    tk = min(SEL_KEY_TILE, seq)
    kx_w = NSA_GROUPS * 2 * LANES
    vx_w = NSA_GROUPS * LANES
    return pl.pallas_call(
        functools.partial(_sel_attn_kernel, tk=tk),
        grid=(batch, nq),
        in_specs=[pl.BlockSpec((Q_TILE, NSA_HEADS * LANES), lambda b, i: (b * nq + i, 0)),
                  pl.BlockSpec((Q_TILE, bias.shape[1]), lambda b, i: (b * nq + i, 0)),
                  pl.BlockSpec((seq, kx_w), lambda b, i: (b, P16_KX // kx_w)),
                  pl.BlockSpec((seq, vx_w), lambda b, i: (b, P16_VX // vx_w))],
        out_specs=pl.BlockSpec((Q_TILE, NSA_HEADS * LANES), lambda b, i: (b * nq + i, 0)),
        out_shape=jax.ShapeDtypeStruct((batch * seq, NSA_HEADS * LANES), F32),
        compiler_params=_cparams("parallel", "arbitrary"),
        name="nsa_sel_attn",
    )(p16, bias, p16, p16)


def _win_attn_kernel(q_ref, kw_ref, vw_ref, o_ref, *, span):
    seq = kw_ref.shape[0]
    q0 = pl.program_id(1) * Q_TILE
    rows = NSA_HPG * Q_TILE
    start = pl.multiple_of(jnp.clip(q0 + Q_TILE - span, 0, seq - span), Q_TILE)
    t_row = q0 + lax.broadcasted_iota(jnp.int32, (rows, span), 0) % Q_TILE
    pos = start + lax.broadcasted_iota(jnp.int32, (rows, span), 1)
    valid = (pos <= t_row) & (pos > t_row - WINDOW)
    for g in range(NSA_GROUPS):
        qg = _stack_heads(q_ref, g)
        s = _dot_nt(qg, kw_ref[pl.ds(start, span), g * LANES:(g + 1) * LANES])
        s = jnp.where(valid, s, MASK_NEG)
        m = jnp.max(s, axis=-1, keepdims=True)
        p = jnp.where(valid, jnp.exp(s - m), 0.0).astype(BF16)
        acc = _dot(p, vw_ref[pl.ds(start, span), g * LANES:(g + 1) * LANES])
        _store_heads(o_ref, g, acc / acc[:, HEAD_DIM:HEAD_DIM + 1])


def _win_attn(p16, batch, seq):
    nq = seq // Q_TILE
    span = min(WINDOW + Q_TILE, seq)
    w = NSA_GROUPS * LANES
    return pl.pallas_call(
        functools.partial(_win_attn_kernel, span=span),
        grid=(batch, nq),
        in_specs=[pl.BlockSpec((Q_TILE, NSA_HEADS * LANES), lambda b, i: (b * nq + i, 0)),
                  pl.BlockSpec((seq, w), lambda b, i: (b, P16_KW // w)),
                  pl.BlockSpec((seq, w), lambda b, i: (b, P16_VW // w))],
        out_specs=pl.BlockSpec((Q_TILE, NSA_HEADS * LANES), lambda b, i: (b * nq + i, 0)),
        out_shape=jax.ShapeDtypeStruct((batch * seq, NSA_HEADS * LANES), F32),
        compiler_params=_cparams("parallel", "arbitrary"),
        name="nsa_win_attn",
    )(p16, p16, p16)


def _spatial_gating_kernel(u_ref, v_ref, g_ref, b_ref, w_ref, bexp_ref, o_ref, *, chunks):
    c = SG_CHUNK
    tri = (lax.broadcasted_iota(jnp.int32, (c, c), 1) <= lax.broadcasted_iota(jnp.int32, (c, c), 0))
    lane = lax.broadcasted_iota(jnp.int32, (c, LANES), 1)
    wm = [jnp.where(tri, w_ref[g], 0.0).astype(BF16) for g in range(SG_GROUPS)]
    for ci in range(chunks):
        v = v_ref[ci * c:(ci + 1) * c, :]
        mu = jnp.mean(v, axis=-1, keepdims=True)
        d = v - mu
        var = jnp.mean(d * d, axis=-1, keepdims=True)
        vn = (d * lax.rsqrt(var + LN_EPS) * g_ref[...] + b_ref[...]).astype(BF16)
        for pair in range(SG_GROUPS // 2):
            vp = vn[:, pair * LANES:(pair + 1) * LANES]
            mixed = jnp.where(lane < HEAD_DIM, _dot(wm[2 * pair], vp), _dot(wm[2 * pair + 1], vp))
            sl = (slice(ci * c, (ci + 1) * c), slice(pair * LANES, (pair + 1) * LANES))
            o_ref[sl] = u_ref[sl] * (mixed + bexp_ref[:, pair * LANES:(pair + 1) * LANES])


def _spatial_gating(p32, ln_g, ln_b, w_s, bexp, u_blk, v_blk, chunks=4):
    t = p32.shape[0]
    width = ln_g.shape[1]
    tm = chunks * SG_CHUNK
    return pl.pallas_call(
        functools.partial(_spatial_gating_kernel, chunks=chunks),
        grid=(t // tm,),
        in_specs=[pl.BlockSpec((tm, width), lambda i: (i, u_blk)),
                  pl.BlockSpec((tm, width), lambda i: (i, v_blk)),
                  pl.BlockSpec((1, width), lambda i: (0, 0)),
                  pl.BlockSpec((1, width), lambda i: (0, 0)),
                  pl.BlockSpec(w_s.shape, lambda i: (0, 0, 0)),
                  pl.BlockSpec(bexp.shape, lambda i: (0, 0))],
        out_specs=pl.BlockSpec((tm, width), lambda i: (i, 0)),
        out_shape=jax.ShapeDtypeStruct((t, width), F32),
        compiler_params=_cparams("parallel"),
        name="spatial_gating",
    )(p32, p32, ln_g, ln_b, w_s, bexp)


def _rwkv_kernel(r_ref, k_ref, v_ref, lo_ref, mu_r, mu_k, mu_v, mu_lo, w0_ref, w2_ref, a0_ref,
                 a2_ref, kk_ref, ka_ref, rk_ref, lng_ref, lnb_ref, seg_ref, o_ref,
                 state, prev_r, prev_k, prev_v, prev_lo):
    c = RWKV_CHUNK
    width = r_ref.shape[1]
    n_pairs = width // LANES
    prec = HIGHEST

    @pl.when(pl.program_id(1) == 0)
    def _():
        state[...] = jnp.zeros_like(state)
        for ref in (prev_r, prev_k, prev_v, prev_lo):
            ref[...] = jnp.zeros_like(ref)

    def token_shift(x_ref, prev_ref, mu_ref):
        x = x_ref[...]
        first = lax.broadcasted_iota(jnp.int32, x.shape, 0) == 0
        prev = jnp.where(first, prev_ref[0:1, :], pltpu.roll(x, 1, 0))
        prev_ref[0:1, :] = x[c - 1:c, :]
        return x + (prev - x) * mu_ref[...]

    r = token_shift(r_ref, prev_r, mu_r)
    k = token_shift(k_ref, prev_k, mu_k)
    v = token_shift(v_ref, prev_v, mu_v)
    lo = token_shift(lo_ref, prev_lo, mu_lo)

    lane_lo = lax.broadcasted_iota(jnp.int32, lo.shape, 1)
    w_in = jnp.where(lane_lo < LORA, jnp.tanh(lo), 0.0).astype(BF16)
    a_in = jnp.where(lane_lo < LORA, 0.0, lo).astype(BF16)
    z = -(w0_ref[...] + _dot(w_in, w2_ref[...]))
    w = -(jnp.maximum(z, 0.0) + jnp.log(1.0 + jnp.exp(-jnp.abs(z)))) - 0.5
    a = _sigmoid(a0_ref[...] + _dot(a_in, a2_ref[...]))

    seg = seg_ref[...]
    kkf = k * kk_ref[...]
    kk = kkf / jnp.maximum(jnp.sqrt(_dot3_lhs(kkf * kkf, seg)), 1e-12)
    k = k * (1.0 + (a - 1.0) * ka_ref[...])
    logd = -jnp.exp(w)
    tri_incl = (lax.broadcasted_iota(jnp.int32, (c, c), 1)
                <= lax.broadcasted_iota(jnp.int32, (c, c), 0))
    cl = _dot3_rhs(jnp.where(tri_incl, 1.0, 0.0).astype(BF16), logd)
    cl_end = cl[c - 1:c, :]
    e_neg = jnp.exp(-cl)
    e_end = jnp.exp(cl_end - cl)
    b = kk * a
    a_t = -kk * jnp.exp(cl - logd)
    b_t = b * e_neg
    k_t = k * e_neg
    r_t = r * jnp.exp(cl)
    b_e = b * e_end
    k_e = k * e_end
    p_end = jnp.exp(cl_end)

    lane = lax.broadcasted_iota(jnp.int32, (c, LANES), 1)
    row2 = lax.broadcasted_iota(jnp.int32, (2 * c, 2 * c), 0)
    col2 = lax.broadcasted_iota(jnp.int32, (2 * c, 2 * c), 1)
    same_head = (row2 // c) == (col2 // c)
    strict = same_head & (col2 < row2)
    incl = same_head & (col2 <= row2)
    eye = row2 == col2

    def stack(x, pair):
        xp = x[:, pair * LANES:(pair + 1) * LANES]
        return jnp.concatenate([jnp.where(lane < HEAD_DIM, xp, 0.0),
                                jnp.where(lane < HEAD_DIM, 0.0, xp)], axis=0)

    ys = []
    for pair in range(n_pairs):
        a_st, r_st = stack(a_t, pair), stack(r_t, pair)
        b_st, k_st, v_st = stack(b_t, pair), stack(k_t, pair), stack(v, pair)
        be_st, ke_st = stack(b_e, pair), stack(k_e, pair)
        l_ab = jnp.where(strict, _dot_nt(a_st, b_st, prec), 0.0)
        l_ak = jnp.where(strict, _dot_nt(a_st, k_st, prec), 0.0)
        m_rb = jnp.where(incl, _dot_nt(r_st, b_st, prec), 0.0)
        m_rk = jnp.where(incl, _dot_nt(r_st, k_st, prec), 0.0)
        inv = jnp.where(eye, 1.0, 0.0) + l_ab
        pw = l_ab
        steps = 1
        while steps * 2 < c:
            pw = _dot(pw, pw, prec)
            inv = inv + _dot(inv, pw, prec)
            steps *= 2
        u0 = _dot(inv, _dot(l_ak, v_st, prec), prec)
        w_st = _dot(inv, a_st, prec)
        st0 = state[pair]
        u_st = _dot(w_st, st0, prec) + u0
        y_st = _dot(r_st, st0, prec) + _dot(m_rb, u_st, prec) + _dot(m_rk, v_st, prec)
        decay_diag = jnp.where(eye, jnp.broadcast_to(p_end[:, pair * LANES:(pair + 1) * LANES],
                                                     (2 * c, 2 * c)), 0.0)
        state[pair] = (_dot(decay_diag, st0, prec) + _dot_tn(be_st, u_st, prec)
                       + _dot_tn(ke_st, v_st, prec))
        ys.append(y_st[0:c] + y_st[c:2 * c])
    y = jnp.concatenate(ys, axis=1)

    seg_mean = (seg * (1.0 / HEAD_DIM)).astype(BF16)
    mean = _dot3_lhs(y, seg_mean)
    d = y - mean
    var = _dot3_lhs(d * d, seg_mean)
    yn = d * lax.rsqrt(var + GN_EPS) * lng_ref[...] + lnb_ref[...]
    bonus = _dot3_lhs(r * k * rk_ref[...], seg) * v
    o_ref[...] = yn + bonus


def _rwkv(p32, params, batch, seq, r_blk, k_blk, v_blk, lo_blk):
    width = params["w0"].shape[1]
    c = RWKV_CHUNK
    nc = seq // c
    row = lambda w: pl.BlockSpec((1, w), lambda b, i: (0, 0))
    full = lambda a: pl.BlockSpec(a.shape, lambda b, i: (0,) * a.ndim)
    names = ("mu_r", "mu_k", "mu_v", "mu_lo", "w0", "w2", "a0", "a2", "kk", "ka", "rk",
             "lng", "lnb", "seg")
    return pl.pallas_call(
        _rwkv_kernel,
        grid=(batch, nc),
        in_specs=[pl.BlockSpec((c, width), lambda b, i: (b * nc + i, r_blk)),
                  pl.BlockSpec((c, width), lambda b, i: (b * nc + i, k_blk)),
                  pl.BlockSpec((c, width), lambda b, i: (b * nc + i, v_blk)),
                  pl.BlockSpec((c, LANES), lambda b, i: (b * nc + i, lo_blk))]
                 + [full(params[n]) for n in names],
        out_specs=pl.BlockSpec((c, width), lambda b, i: (b * nc + i, 0)),
        out_shape=jax.ShapeDtypeStruct((batch * seq, width), F32),
        scratch_shapes=[pltpu.VMEM((width // LANES, LANES, LANES), F32),
                        pltpu.VMEM((8, width), F32), pltpu.VMEM((8, width), F32),
                        pltpu.VMEM((8, width), F32), pltpu.VMEM((8, LANES), F32)],
        compiler_params=_cparams("parallel", "arbitrary"),
        name="rwkv7",
    )(p32, p32, p32, p32, *[params[n] for n in names])


def _merge_kernel(oc_ref, os_ref, ow_ref, gate_ref, nz_ref, sg_ref, sz_ref, rk_ref, rz_ref,
                  mg_ref, x_ref, p_ref, ec_ref, wb0_ref, wb1_ref, wb2_ref, wo_ref, pleg_ref,
                  wpg_ref, wpp_ref, fin_ref, o_ref, *, last):
    d = x_ref.shape[1]
    gate = _sigmoid(gate_ref[...])
    hi, mid, lo = _split3(gate)
    y_nsa = None
    for i, ref in enumerate((oc_ref, os_ref, ow_ref)):
        e = ec_ref[i]
        term = (_dot(hi, e) + _dot(mid, e) + _dot(lo, e)) * ref[...]
        y_nsa = term if y_nsa is None else y_nsa + term
    z0 = _dot((y_nsa * _silu(nz_ref[...])).astype(BF16), wb0_ref[...])
    z1 = _dot((sg_ref[...] * _silu(sz_ref[...])).astype(BF16), wb1_ref[...])
    z2 = _dot((rk_ref[...] * _silu(rz_ref[...])).astype(BF16), wb2_ref[...])
    merged = (_sigmoid(mg_ref[:, 0:d]) * z0 + _sigmoid(mg_ref[:, d:2 * d]) * z1
              + _sigmoid(mg_ref[:, 2 * d:3 * d]) * z2)
    x1 = x_ref[...] + _dot(merged.astype(BF16), wo_ref[...])
    hp = _rmsnorm(x1, pleg_ref[...]).astype(BF16)
    x2 = x1 + _sigmoid(_dot(hp, wpg_ref[...])) * _dot(p_ref[...].astype(BF16), wpp_ref[...])
    o_ref[...] = _rmsnorm(x2, fin_ref[...]) if last else x2


def _merge(o_c, o_s, o_w, p32, y_sg, y_rk, x2, p2, consts, blocks, last, tm=256):
    t, d = x2.shape
    half = y_sg.shape[1]
    tok = lambda w, j=0: pl.BlockSpec((tm, w), lambda i, j=j: (i, j))
    full = lambda a: pl.BlockSpec(a.shape, lambda i: (0,) * a.ndim)
    names = ("ec", "wb0", "wb1", "wb2", "wo", "pleg", "wpg", "wpp", "fin")
    return pl.pallas_call(
        functools.partial(_merge_kernel, last=last),
        grid=(t // tm,),
        in_specs=[tok(o_c.shape[1]), tok(o_s.shape[1]), tok(o_w.shape[1]),
                  tok(LANES, blocks["gate"]), tok(d, blocks["nz"]),
                  tok(half), tok(half, blocks["sz"]), tok(half), tok(half, blocks["rz"]),
                  tok(3 * d, blocks["mg"]), tok(d), tok(p2.shape[1])]
                 + [full(consts[n]) for n in names],
        out_specs=tok(d),
        out_shape=jax.ShapeDtypeStruct((t, d), F32),
        compiler_params=_cparams("parallel"),
        name="merge_out_ple",
    )(o_c, o_s, o_w, p32, p32, y_sg, p32, y_rk, p32, p32, x2, p2, *[consts[n] for n in names])


def _pad_cols(w, n):
    return jnp.pad(w, ((0, 0), (0, n - w.shape[1])))


def _layer_layout(d_model):
    half = d_model // 2
    kvw = NSA_GROUPS * HEAD_DIM
    sizes = dict(nq=half, nkv=6 * kvw, ngate=3 * NSA_HEADS, nz=half, su=half, sv=half, sz=half,
                 rs=3 * half + 2 * LORA, rz=half, mg=3 * d_model)
    off, out = 0, {}
    for name, n in sizes.items():
        out[name] = (off, n)
        off += n
    return out


def _prep_proj_weights(w_in, d_model):
    lay = _layer_layout(d_model)
    half = d_model // 2
    kvw = NSA_GROUPS * HEAD_DIM
    col = lambda name, a=0, n=None: w_in[:, lay[name][0] + a: lay[name][0] + a + (n or lay[name][1] - a)]
    heads = lambda w: jnp.concatenate(
        [_pad_cols(w[:, h * HEAD_DIM:(h + 1) * HEAD_DIM], LANES) for h in range(w.shape[1] // HEAD_DIM)], axis=1)
    w32 = jnp.concatenate([
        col("mg"), heads(col("nz")), col("su"), col("sv"), col("sz"), col("rz"),
        col("rs", 0, half), col("rs", half, half), col("rs", 2 * half, half),
        col("nkv", 0, 2 * kvw), col("rs", 3 * half, 2 * LORA), _pad_cols(col("ngate"), LANES)], axis=1)
    o = 3 * d_model + NSA_HEADS * LANES
    blocks = dict(mg=0, nz=(3 * d_model) // (NSA_HEADS * LANES))
    for name in ("su", "sv", "sz", "rz", "rr", "rk", "rv"):
        blocks[name] = o // half
        o += half
    blocks["kcvc"] = o
    o += 2 * kvw
    blocks["lora"] = o // LANES
    o += LANES
    blocks["gate"] = o // LANES
    o += LANES
    zero = jnp.zeros((w_in.shape[0], LANES), w_in.dtype)
    grp = lambda w, g: _pad_cols(w[:, g * HEAD_DIM:(g + 1) * HEAD_DIM], LANES)
    ks, vs = col("nkv", 2 * kvw, kvw), col("nkv", 3 * kvw, kvw)
    kw, vw = col("nkv", 4 * kvw, kvw), col("nkv", 5 * kvw, kvw)
    w16 = jnp.concatenate(
        [heads(col("nq"))]
        + [x for g in range(NSA_GROUPS) for x in (grp(ks, g), zero)]
        + [grp(vs, g) for g in range(NSA_GROUPS)]
        + [grp(kw, g) for g in range(NSA_GROUPS)]
        + [grp(vw, g) for g in range(NSA_GROUPS)], axis=1)
    return w32.astype(BF16), w16.astype(BF16), blocks


def _prep_compress_weights(cmp_w1, cmp_w2, cmp_pe):
    half_blk = CMP_BLOCK // 2
    w1r = cmp_w1.reshape(2, 2, half_blk, HEAD_DIM, CMP_HIDDEN)
    eye = jnp.eye(NSA_GROUPS, dtype=cmp_w1.dtype)
    eye_kv = jnp.eye(2, dtype=cmp_w1.dtype)
    big = jnp.einsum('kaldh,kK,gG->alkgdKGh', w1r, eye_kv, eye)
    big = big.reshape(2, half_blk * 2 * NSA_GROUPS * HEAD_DIM, 2 * NSA_GROUPS * CMP_HIDDEN)
    pe = cmp_pe.reshape(2, 2, half_blk, HEAD_DIM)
    pe2 = jnp.broadcast_to(pe.transpose(1, 2, 0, 3)[:, :, :, None, :],
                           (2, half_blk, 2, NSA_GROUPS, HEAD_DIM)).reshape(2, -1)
    w2p = jnp.pad(cmp_w2, ((0, 0), (0, 0), (0, LANES - HEAD_DIM)))
    return pe2, big[0].astype(BF16), big[1].astype(BF16), w2p.astype(BF16)


def _cover_t(seq):
    n_grp = seq // CMP_STRIDE
    n_slc = seq // SEL_BLOCK
    cmp_start = jnp.arange(n_grp) * CMP_STRIDE
    slc_start = jnp.arange(n_slc) * SEL_BLOCK
    cover = ((cmp_start[None, :] <= slc_start[:, None] + SEL_BLOCK - 1)
             & (cmp_start[None, :] + CMP_BLOCK - 1 >= slc_start[:, None]))
    return jnp.pad(cover, ((0, LANES - n_slc), (0, 0))).astype(BF16)


def _gate_expand():
    c = jnp.arange(LANES)[:, None]
    col = jnp.arange(NSA_HEADS * LANES)[None, :]
    return jnp.stack([(c == (col // LANES) * 3 + i) for i in range(3)]).astype(BF16)


def kernel(x, p, norm_g, w_in, cmp_w1, cmp_w2, cmp_pe, sg_ln_g, sg_ln_b, sg_w, sg_b, rk_mu, rk_w0,
           rk_w2, rk_a0, rk_a2, rk_kk, rk_ka, rk_rk, rk_lnx_g, rk_lnx_b, w_branch, w_o, ple_norm_g,
           w_ple_gate, w_ple_proj, final_norm_g):
    batch, seq, d = x.shape
    depth = w_in.shape[0]
    half = d // 2
    t = batch * seq
    assert seq % SEL_KEY_TILE == 0 or seq < SEL_KEY_TILE
    assert seq >= WINDOW + Q_TILE and seq // SEL_BLOCK <= LANES
    x2 = x.reshape(t, d)
    cov_t = _cover_t(seq)
    ec = _gate_expand()
    seg = (jnp.arange(half)[:, None] // HEAD_DIM == jnp.arange(half)[None, :] // HEAD_DIM).astype(BF16)
    row = lambda v: v.reshape(1, -1)
    for i in range(depth):
        w32, w16, blk = _prep_proj_weights(w_in[i], d)
        g = row(norm_g[i])
        p32 = _proj32(x2, g, w32)
        p16 = _proj16(x2, g, w16, seq)

        pe2, wa, wb, w2p = _prep_compress_weights(cmp_w1[i], cmp_w2[i], cmp_pe[i])
        kcvc = p32[:, blk["kcvc"]:blk["kcvc"] + 2 * NSA_GROUPS * HEAD_DIM]
        xg = kcvc.reshape(batch, seq // CMP_STRIDE, CMP_STRIDE * 2 * NSA_GROUPS * HEAD_DIM)
        kc, vc = _compress(xg, pe2, wa, wb, w2p)
        o_c, bias = _cmp_select(p16, kc, vc, cov_t, batch, seq)
        o_s = _sel_attn(p16, bias, batch, seq)
        o_w = _win_attn(p16, batch, seq)

        bexp = jnp.repeat(sg_b[i].T, half // SG_GROUPS, axis=1)
        y_sg = _spatial_gating(p32, row(sg_ln_g[i]), row(sg_ln_b[i]), sg_w[i], bexp, blk["su"], blk["sv"])

        mu = rk_mu[i]
        rk_params = dict(
            mu_r=row(mu[0:half]), mu_k=row(mu[half:2 * half]), mu_v=row(mu[2 * half:3 * half]),
            mu_lo=row(mu[3 * half:]), w0=row(rk_w0[i]),
            w2=jnp.pad(rk_w2[i], ((0, LANES - LORA), (0, 0))).astype(BF16), a0=row(rk_a0[i]),
            a2=jnp.pad(rk_a2[i], ((LANES - LORA, 0), (0, 0))).astype(BF16),
            kk=row(rk_kk[i]), ka=row(rk_ka[i]), rk=row(rk_rk[i]), lng=row(rk_lnx_g[i]),
            lnb=row(rk_lnx_b[i]), seg=seg)
        y_rk = _rwkv(p32, rk_params, batch, seq, blk["rr"], blk["rk"], blk["rv"], blk["lora"])

        wb0 = jnp.pad(w_branch[i, 0].reshape(NSA_HEADS, HEAD_DIM, d),
                      ((0, 0), (0, LANES - HEAD_DIM), (0, 0))).reshape(NSA_HEADS * LANES, d)
        consts = dict(ec=ec, wb0=wb0.astype(BF16), wb1=w_branch[i, 1].astype(BF16),
                      wb2=w_branch[i, 2].astype(BF16), wo=w_o[i].astype(BF16),
                      pleg=row(ple_norm_g[i]), wpg=w_ple_gate[i].astype(BF16),
                      wpp=w_ple_proj[i].astype(BF16), fin=row(final_norm_g))
        x2 = _merge(o_c, o_s, o_w, p32, y_sg, y_rk, x2, p[i].reshape(t, -1), consts, blk,
                    last=(i == depth - 1))
    return x2.reshape(batch, seq, d)
```

```python
import functools

import jax
import jax.numpy as jnp
from jax import lax
from jax.experimental import pallas as pl
from jax.experimental.pallas import tpu as pltpu

F32 = jnp.float32
BF16 = jnp.bfloat16

HEAD_DIM = 64
LANES = 128
NSA_HEADS = 8
NSA_GROUPS = 2
NSA_HPG = NSA_HEADS // NSA_GROUPS
CMP_BLOCK = 32
CMP_STRIDE = 16
CMP_HIDDEN = 128
SEL_BLOCK = 64
SEL_TOPK = 16
WINDOW = 512
Q_TILE = 128
SEL_KEY_TILE = 512
SG_GROUPS = 8
SG_CHUNK = 128
RWKV_HEADS = 8
RWKV_CHUNK = 64
LORA = 64
NORM_EPS = 1e-6
LN_EPS = 1e-5
GN_EPS = 64e-5
MASK_NEG = -1e30
FORCE_BONUS = 1e4
VMEM_LIMIT = 56 * 1024 * 1024


def _cparams(*sem):
    return pltpu.CompilerParams(dimension_semantics=sem, vmem_limit_bytes=VMEM_LIMIT)


def _dot(a, b, precision=None):
    return jnp.dot(a, b, preferred_element_type=F32, precision=precision)


def _dot_nt(a, b, precision=None):
    return lax.dot_general(a, b, (((1,), (1,)), ((), ())), preferred_element_type=F32,
                           precision=precision)


def _dot_tn(a, b, precision=None):
    return lax.dot_general(a, b, (((0,), (0,)), ((), ())), preferred_element_type=F32,
                           precision=precision)


def _split3(x):
    hi = x.astype(BF16)
    r1 = x - hi.astype(F32)
    mid = r1.astype(BF16)
    lo = (r1 - mid.astype(F32)).astype(BF16)
    return hi, mid, lo


def _dot3_lhs(x, w):
    hi, mid, lo = _split3(x)
    return _dot(hi, w) + _dot(mid, w) + _dot(lo, w)


def _dot3_rhs(w, x):
    hi, mid, lo = _split3(x)
    return _dot(w, hi) + _dot(w, mid) + _dot(w, lo)


def _sigmoid(x):
    return 1.0 / (1.0 + jnp.exp(-x))


def _silu(x):
    return x * _sigmoid(x)


def _rmsnorm(x, g):
    return x * lax.rsqrt(jnp.mean(x * x, axis=-1, keepdims=True) + NORM_EPS) * g


def _proj32_kernel(x_ref, g_ref, w_ref, o_ref, h_scr):
    @pl.when(pl.program_id(1) == 0)
    def _():
        h_scr[...] = _rmsnorm(x_ref[...], g_ref[...]).astype(BF16)

    o_ref[...] = _dot(h_scr[...], w_ref[...])


def _proj32(x2, g, w, tm=1024, tn=1024):
    t, d = x2.shape
    n = w.shape[1]
    return pl.pallas_call(
        _proj32_kernel,
        grid=(t // tm, n // tn),
        in_specs=[pl.BlockSpec((tm, d), lambda i, j: (i, 0)),
                  pl.BlockSpec((1, d), lambda i, j: (0, 0)),
                  pl.BlockSpec((d, tn), lambda i, j: (0, j))],
        out_specs=pl.BlockSpec((tm, tn), lambda i, j: (i, j)),
        out_shape=jax.ShapeDtypeStruct((t, n), F32),
        scratch_shapes=[pltpu.VMEM((tm, d), BF16)],
        compiler_params=_cparams("parallel", "arbitrary"),
        name="proj32",
    )(x2, g, w)


P16_Q = 0
P16_KX = NSA_HEADS * LANES
P16_VX = P16_KX + NSA_GROUPS * 2 * LANES
P16_KW = P16_VX + NSA_GROUPS * LANES
P16_VW = P16_KW + NSA_GROUPS * LANES
P16_N = P16_VW + NSA_GROUPS * LANES


def _proj16_kernel(x_ref, g_ref, w_ref, o_ref, *, seq, tm):
    h = _rmsnorm(x_ref[...], g_ref[...]).astype(BF16)
    pos = (pl.program_id(0) * tm) % seq + lax.broadcasted_iota(jnp.int32, (tm, LANES), 0)
    lane = lax.broadcasted_iota(jnp.int32, (tm, LANES), 1)
    onehot_blk = jnp.where(pos // SEL_BLOCK == lane, 1.0, 0.0)
    ones_col = jnp.where(lane == HEAD_DIM, 1.0, 0.0)
    for c0 in range(0, P16_N, LANES):
        y = _dot(h, w_ref[:, c0:c0 + LANES])
        if P16_KX <= c0 < P16_VX and ((c0 - P16_KX) // LANES) % 2 == 1:
            y = y + onehot_blk
        elif c0 >= P16_VX and not (P16_KW <= c0 < P16_VW):
            y = y + ones_col
        o_ref[:, c0:c0 + LANES] = y.astype(BF16)


def _proj16(x2, g, w, seq, tm=512):
    t, d = x2.shape
    return pl.pallas_call(
        functools.partial(_proj16_kernel, seq=seq, tm=tm),
        grid=(t // tm,),
        in_specs=[pl.BlockSpec((tm, d), lambda i: (i, 0)),
                  pl.BlockSpec((1, d), lambda i: (0, 0)),
                  pl.BlockSpec((d, P16_N), lambda i: (0, 0))],
        out_specs=pl.BlockSpec((tm, P16_N), lambda i: (i, 0)),
        out_shape=jax.ShapeDtypeStruct((t, P16_N), BF16),
        compiler_params=_cparams("parallel"),
        name="proj16",
    )(x2, g, w)


def _compress_kernel(x_ref, pe_ref, wa_ref, wb_ref, w2_ref, k_ref, v_ref):
    x = x_ref[0]
    n_grp = x.shape[0]
    a = _dot((x + pe_ref[0:1, :]).astype(BF16), wa_ref[...])
    b = _dot((x + pe_ref[1:2, :]).astype(BF16), wb_ref[...])
    hid = _silu(a + pltpu.roll(b, n_grp - 1, 0)).astype(BF16)
    row = lax.broadcasted_iota(jnp.int32, (n_grp, LANES), 0)
    lane = lax.broadcasted_iota(jnp.int32, (n_grp, LANES), 1)
    live = row < n_grp - 1
    for kv, out_ref in enumerate((k_ref, v_ref)):
        for g in range(NSA_GROUPS):
            c0 = (kv * NSA_GROUPS + g) * CMP_HIDDEN
            y = _dot(hid[:, c0:c0 + CMP_HIDDEN], w2_ref[kv])
            if kv == 1:
                y = y + jnp.where(lane == HEAD_DIM, 1.0, 0.0)
            out_ref[0, :, g * LANES:(g + 1) * LANES] = jnp.where(live, y, 0.0).astype(BF16)


def _compress(xg, pe2, wa, wb, w2p):
    b, n_grp, width = xg.shape
    out = jax.ShapeDtypeStruct((b, n_grp, NSA_GROUPS * LANES), BF16)
    return pl.pallas_call(
        _compress_kernel,
        grid=(b,),
        in_specs=[pl.BlockSpec((1, n_grp, width), lambda i: (i, 0, 0)),
                  pl.BlockSpec(pe2.shape, lambda i: (0, 0)),
                  pl.BlockSpec(wa.shape, lambda i: (0, 0)),
                  pl.BlockSpec(wb.shape, lambda i: (0, 0)),
                  pl.BlockSpec(w2p.shape, lambda i: (0, 0, 0))],
        out_specs=[pl.BlockSpec((1, n_grp, NSA_GROUPS * LANES), lambda i: (i, 0, 0))] * 2,
        out_shape=[out, out],
        compiler_params=_cparams("parallel"),
        name="nsa_compress",
    )(xg, pe2, wa, wb, w2p)


def _stack_heads(q_ref, g):
    parts = [q_ref[:, (g * NSA_HPG + h) * LANES:(g * NSA_HPG + h + 1) * LANES]
             for h in range(NSA_HPG)]
    return jnp.concatenate(parts, axis=0) * jnp.asarray(HEAD_DIM ** -0.5, BF16)


def _store_heads(o_ref, g, o):
    lane = lax.broadcasted_iota(jnp.int32, (Q_TILE, LANES), 1)
    for h in range(NSA_HPG):
        c0 = (g * NSA_HPG + h) * LANES
        o_ref[:, c0:c0 + LANES] = jnp.where(lane < HEAD_DIM, o[h * Q_TILE:(h + 1) * Q_TILE], 0.0)


def _cmp_select_kernel(q_ref, kc_ref, vc_ref, cov_ref, o_ref, bias_ref, *, top_k):
    n_cmp = kc_ref.shape[1]
    n_slc = cov_ref.shape[0]
    q0 = pl.program_id(1) * Q_TILE
    rows = NSA_HPG * Q_TILE
    t_row = q0 + lax.broadcasted_iota(jnp.int32, (rows, n_cmp), 0) % Q_TILE
    cmp_end = lax.broadcasted_iota(jnp.int32, (rows, n_cmp), 1) * CMP_STRIDE + CMP_BLOCK - 1
    valid = cmp_end <= t_row
    sel_shape = (n_slc, NSA_GROUPS * Q_TILE)
    j_idx = lax.broadcasted_iota(jnp.int32, sel_shape, 0)
    t_lane = q0 + lax.broadcasted_iota(jnp.int32, sel_shape, 1) % Q_TILE
    t_blk = t_lane // SEL_BLOCK
    forced = (j_idx == 0) | (j_idx == t_blk) | (j_idx == t_blk - 1)
    causal_blk = j_idx <= t_blk
    imp_t = []
    for g in range(NSA_GROUPS):
        qg = _stack_heads(q_ref, g)
        s = _dot_nt(qg, kc_ref[0, :, g * LANES:(g + 1) * LANES])
        s = jnp.where(valid, s, MASK_NEG)
        m = jnp.max(s, axis=-1, keepdims=True)
        p = jnp.where(valid, jnp.exp(s - m), 0.0)
        l = jnp.sum(p, axis=-1, keepdims=True)
        p = p * jnp.where(l > 0.0, 1.0 / l, 0.0)
        o = _dot(p.astype(BF16), vc_ref[0, :, g * LANES:(g + 1) * LANES])
        _store_heads(o_ref, g, o)
        p_sum = p[0:Q_TILE]
        for h in range(1, NSA_HPG):
            p_sum = p_sum + p[h * Q_TILE:(h + 1) * Q_TILE]
        hi, mid, lo = _split3(p_sum)
        cov = cov_ref[...]
        imp_t.append(_dot_nt(cov, hi) + _dot_nt(cov, mid) + _dot_nt(cov, lo))
    score = jnp.where(causal_blk, jnp.where(forced, FORCE_BONUS, jnp.concatenate(imp_t, axis=1)),
                      -FORCE_BONUS)

    def pick(_, carry):
        sc, sel = carry
        best = jnp.max(sc, axis=0, keepdims=True)
        first = jnp.min(jnp.where(sc == best, j_idx, n_slc), axis=0, keepdims=True)
        hit = j_idx == first
        return jnp.where(hit, -jnp.inf, sc), jnp.where(hit, 1.0, sel)

    _, sel = lax.fori_loop(0, top_k, pick, (score, jnp.zeros(sel_shape, F32)))
    bias_t = jnp.where((sel > 0.0) & causal_blk, 0.0, MASK_NEG)
    for g in range(NSA_GROUPS):
        bias_ref[:, g * n_slc:(g + 1) * n_slc] = bias_t[:, g * Q_TILE:(g + 1) * Q_TILE].T.astype(BF16)


def _cmp_select(p16, kc, vc, cov_t, batch, seq):
    n_cmp = kc.shape[1]
    n_slc = cov_t.shape[0]
    nq = seq // Q_TILE
    return pl.pallas_call(
        functools.partial(_cmp_select_kernel, top_k=min(SEL_TOPK, seq // SEL_BLOCK)),
        grid=(batch, nq),
        in_specs=[pl.BlockSpec((Q_TILE, NSA_HEADS * LANES), lambda b, i: (b * nq + i, 0)),
                  pl.BlockSpec((1, n_cmp, NSA_GROUPS * LANES), lambda b, i: (b, 0, 0)),
                  pl.BlockSpec((1, n_cmp, NSA_GROUPS * LANES), lambda b, i: (b, 0, 0)),
                  pl.BlockSpec(cov_t.shape, lambda b, i: (0, 0))],
        out_specs=[pl.BlockSpec((Q_TILE, NSA_HEADS * LANES), lambda b, i: (b * nq + i, 0)),
                   pl.BlockSpec((Q_TILE, NSA_GROUPS * n_slc), lambda b, i: (b * nq + i, 0))],
        out_shape=[jax.ShapeDtypeStruct((batch * seq, NSA_HEADS * LANES), F32),
                   jax.ShapeDtypeStruct((batch * seq, NSA_GROUPS * n_slc), BF16)],
        compiler_params=_cparams("parallel", "arbitrary"),
        name="nsa_cmp_select",
    )(p16, kc, vc, cov_t)


def _sel_attn_kernel(q_ref, bias_ref, kx_ref, vx_ref, o_ref, m_scr, acc_scr, *, tk):
    n_slc = bias_ref.shape[1] // NSA_GROUPS
    q0 = pl.program_id(1) * Q_TILE
    rows = NSA_HPG * Q_TILE
    n_past = q0 // tk
    qx = []
    for g in range(NSA_GROUPS):
        bias = bias_ref[:, g * n_slc:(g + 1) * n_slc]
        qx.append(jnp.concatenate([_stack_heads(q_ref, g),
                                   jnp.concatenate([bias] * NSA_HPG, axis=0)], axis=1))

    for g in range(NSA_GROUPS):
        m_scr[g] = jnp.full((rows, LANES), MASK_NEG, F32)
        acc_scr[g] = jnp.zeros((rows, LANES), F32)

    def step(k0, width, diagonal):
        for g in range(NSA_GROUPS):
            s = _dot_nt(qx[g], kx_ref[pl.ds(k0, width), g * 2 * LANES:(g + 1) * 2 * LANES])
            if diagonal:
                t_row = q0 + lax.broadcasted_iota(jnp.int32, (rows, width), 0) % Q_TILE
                key = k0 + lax.broadcasted_iota(jnp.int32, (rows, width), 1)
                s = jnp.where(key <= t_row, s, MASK_NEG)
            m_old = m_scr[g]
            m_new = jnp.maximum(m_old, jnp.max(s, axis=-1, keepdims=True))
            p = jnp.exp(s - jnp.tile(m_new, (1, width // LANES))).astype(BF16)
            pv = _dot(p, vx_ref[pl.ds(k0, width), g * LANES:(g + 1) * LANES])
            acc_scr[g] = jnp.exp(m_old - m_new) * acc_scr[g] + pv
            m_scr[g] = m_new

    def wide(j, _):
        step(pl.multiple_of(j * 2 * tk, 2 * tk), 2 * tk, False)
        return 0

    lax.fori_loop(0, n_past // 2, wide, 0)

    @pl.when(n_past % 2 == 1)
    def _():
        step(pl.multiple_of((n_past - 1) * tk, tk), tk, False)

    step(pl.multiple_of(n_past * tk, tk), tk, True)
    for g in range(NSA_GROUPS):
        acc = acc_scr[g]
        _store_heads(o_ref, g, acc / acc[:, HEAD_DIM:HEAD_DIM + 1])


def _sel_attn(p16, bias, batch, seq):
    nq = seq // Q_TILE
    tk = min(SEL_KEY_TILE, seq)
    kx_w = NSA_GROUPS * 2 * LANES
    vx_w = NSA_GROUPS * LANES
    return pl.pallas_call(
        functools.partial(_sel_attn_kernel, tk=tk),
        grid=(batch, nq),
        in_specs=[pl.BlockSpec((Q_TILE, NSA_HEADS * LANES), lambda b, i: (b * nq + i, 0)),
                  pl.BlockSpec((Q_TILE, bias.shape[1]), lambda b, i: (b * nq + i, 0)),
                  pl.BlockSpec((seq, kx_w), lambda b, i: (b, P16_KX // kx_w)),
                  pl.BlockSpec((seq, vx_w), lambda b, i: (b, P16_VX // vx_w))],
        out_specs=pl.BlockSpec((Q_TILE, NSA_HEADS * LANES), lambda b, i: (b * nq + i, 0)),
        out_shape=jax.ShapeDtypeStruct((batch * seq, NSA_HEADS * LANES), F32),
        scratch_shapes=[pltpu.VMEM((NSA_GROUPS, NSA_HPG * Q_TILE, LANES), F32),
                        pltpu.VMEM((NSA_GROUPS, NSA_HPG * Q_TILE, LANES), F32)],
        compiler_params=_cparams("parallel", "arbitrary"),
        name="nsa_sel_attn",
    )(p16, bias, p16, p16)


def _win_attn_kernel(q_ref, kw_ref, vw_ref, o_ref, *, span):
    seq = kw_ref.shape[0]
    q0 = pl.program_id(1) * Q_TILE
    rows = NSA_HPG * Q_TILE
    start = pl.multiple_of(jnp.clip(q0 + Q_TILE - span, 0, seq - span), Q_TILE)
    t_row = q0 + lax.broadcasted_iota(jnp.int32, (rows, span), 0) % Q_TILE
    pos = start + lax.broadcasted_iota(jnp.int32, (rows, span), 1)
    valid = (pos <= t_row) & (pos > t_row - WINDOW)
    for g in range(NSA_GROUPS):
        qg = _stack_heads(q_ref, g)
        s = _dot_nt(qg, kw_ref[pl.ds(start, span), g * LANES:(g + 1) * LANES])
        s = jnp.where(valid, s, MASK_NEG)
        m = jnp.max(s, axis=-1, keepdims=True)
        p = jnp.where(valid, jnp.exp(s - m), 0.0).astype(BF16)
        acc = _dot(p, vw_ref[pl.ds(start, span), g * LANES:(g + 1) * LANES])
        _store_heads(o_ref, g, acc / acc[:, HEAD_DIM:HEAD_DIM + 1])


def _win_attn(p16, batch, seq):
    nq = seq // Q_TILE
    span = min(WINDOW + Q_TILE, seq)
    w = NSA_GROUPS * LANES
    return pl.pallas_call(
        functools.partial(_win_attn_kernel, span=span),
        grid=(batch, nq),
        in_specs=[pl.BlockSpec((Q_TILE, NSA_HEADS * LANES), lambda b, i: (b * nq + i, 0)),
                  pl.BlockSpec((seq, w), lambda b, i: (b, P16_KW // w)),
                  pl.BlockSpec((seq, w), lambda b, i: (b, P16_VW // w))],
        out_specs=pl.BlockSpec((Q_TILE, NSA_HEADS * LANES), lambda b, i: (b * nq + i, 0)),
        out_shape=jax.ShapeDtypeStruct((batch * seq, NSA_HEADS * LANES), F32),
        compiler_params=_cparams("parallel", "arbitrary"),
        name="nsa_win_attn",
    )(p16, p16, p16)


def _spatial_gating_kernel(u_ref, v_ref, g_ref, b_ref, w_ref, bexp_ref, o_ref, *, chunks):
    c = SG_CHUNK
    tri = (lax.broadcasted_iota(jnp.int32, (c, c), 1) <= lax.broadcasted_iota(jnp.int32, (c, c), 0))
    lane = lax.broadcasted_iota(jnp.int32, (c, LANES), 1)
    wm = [jnp.where(tri, w_ref[g], 0.0).astype(BF16) for g in range(SG_GROUPS)]
    for ci in range(chunks):
        v = v_ref[ci * c:(ci + 1) * c, :]
        mu = jnp.mean(v, axis=-1, keepdims=True)
        d = v - mu
        var = jnp.mean(d * d, axis=-1, keepdims=True)
        vn = (d * lax.rsqrt(var + LN_EPS) * g_ref[...] + b_ref[...]).astype(BF16)
        for pair in range(SG_GROUPS // 2):
            vp = vn[:, pair * LANES:(pair + 1) * LANES]
            mixed = jnp.where(lane < HEAD_DIM, _dot(wm[2 * pair], vp), _dot(wm[2 * pair + 1], vp))
            sl = (slice(ci * c, (ci + 1) * c), slice(pair * LANES, (pair + 1) * LANES))
            o_ref[sl] = u_ref[sl] * (mixed + bexp_ref[:, pair * LANES:(pair + 1) * LANES])


def _spatial_gating(p32, ln_g, ln_b, w_s, bexp, u_blk, v_blk, chunks=4):
    t = p32.shape[0]
    width = ln_g.shape[1]
    tm = chunks * SG_CHUNK
    return pl.pallas_call(
        functools.partial(_spatial_gating_kernel, chunks=chunks),
        grid=(t // tm,),
        in_specs=[pl.BlockSpec((tm, width), lambda i: (i, u_blk)),
                  pl.BlockSpec((tm, width), lambda i: (i, v_blk)),
                  pl.BlockSpec((1, width), lambda i: (0, 0)),
                  pl.BlockSpec((1, width), lambda i: (0, 0)),
                  pl.BlockSpec(w_s.shape, lambda i: (0, 0, 0)),
                  pl.BlockSpec(bexp.shape, lambda i: (0, 0))],
        out_specs=pl.BlockSpec((tm, width), lambda i: (i, 0)),
        out_shape=jax.ShapeDtypeStruct((t, width), F32),
        compiler_params=_cparams("parallel"),
        name="spatial_gating",
    )(p32, p32, ln_g, ln_b, w_s, bexp)


def _rwkv_kernel(r_ref, k_ref, v_ref, lo_ref, mu_r, mu_k, mu_v, mu_lo, w0_ref, w2_ref, a0_ref,
                 a2_ref, kk_ref, ka_ref, rk_ref, lng_ref, lnb_ref, seg_ref, o_ref,
                 state, prev_r, prev_k, prev_v, prev_lo, *, nch):
    c = RWKV_CHUNK
    rows = nch * c
    width = r_ref.shape[1]
    n_pairs = width // LANES

    @pl.when(pl.program_id(1) == 0)
    def _():
        state[...] = jnp.zeros_like(state)
        for ref in (prev_r, prev_k, prev_v, prev_lo):
            ref[...] = jnp.zeros_like(ref)

    def token_shift(x_ref, prev_ref, mu_ref):
        x = x_ref[...]
        first = lax.broadcasted_iota(jnp.int32, x.shape, 0) == 0
        prev = jnp.where(first, prev_ref[0:1, :], pltpu.roll(x, 1, 0))
        prev_ref[0:1, :] = x[rows - 1:rows, :]
        return x + (prev - x) * mu_ref[...]

    r = token_shift(r_ref, prev_r, mu_r)
    k = token_shift(k_ref, prev_k, mu_k)
    v = token_shift(v_ref, prev_v, mu_v)
    lo = token_shift(lo_ref, prev_lo, mu_lo)

    lane_lo = lax.broadcasted_iota(jnp.int32, lo.shape, 1)
    w_in = jnp.where(lane_lo < LORA, jnp.tanh(lo), 0.0).astype(BF16)
    a_in = jnp.where(lane_lo < LORA, 0.0, lo).astype(BF16)
    z = -(w0_ref[...] + _dot(w_in, w2_ref[...]))
    w = -(jnp.maximum(z, 0.0) + jnp.log(1.0 + jnp.exp(-jnp.abs(z)))) - 0.5
    a = _sigmoid(a0_ref[...] + _dot(a_in, a2_ref[...]))

    seg = seg_ref[...]
    kkf = k * kk_ref[...]
    kk = kkf / jnp.maximum(jnp.sqrt(_dot3_lhs(kkf * kkf, seg)), 1e-12)
    k = k * (1.0 + (a - 1.0) * ka_ref[...])
    logd = -jnp.exp(w)
    t_row = lax.broadcasted_iota(jnp.int32, (rows, rows), 0)
    t_col = lax.broadcasted_iota(jnp.int32, (rows, rows), 1)
    cum = (t_col <= t_row) & (t_col // c == t_row // c)
    cl = _dot3_rhs(jnp.where(cum, 1.0, 0.0).astype(BF16), logd)
    cl_end = cl.reshape(nch, c, width)[:, c - 1:c, :]
    e_neg = jnp.exp(-cl)
    e_end = jnp.exp(jnp.broadcast_to(cl_end, (nch, c, width)).reshape(rows, width) - cl)
    b = kk * a
    a_t = -kk * jnp.exp(cl - logd)
    b_t = b * e_neg
    k_t = k * e_neg
    r_t = r * jnp.exp(cl)
    b_e = b * e_end
    k_e = k * e_end
    p_end = jnp.exp(cl_end)

    lane = lax.broadcasted_iota(jnp.int32, (c, LANES), 1)
    row2 = lax.broadcasted_iota(jnp.int32, (2 * c, 2 * c), 0)
    col2 = lax.broadcasted_iota(jnp.int32, (2 * c, 2 * c), 1)
    same_head = (row2 // c) == (col2 // c)
    strict = same_head & (col2 < row2)
    incl = same_head & (col2 <= row2)
    eye = jnp.where(row2 == col2, 1.0, 0.0)
    zero16 = jnp.zeros((c, LANES), BF16)

    def stack(x, ci, pair):
        xp = x[ci * c:(ci + 1) * c, pair * LANES:(pair + 1) * LANES].astype(BF16)
        return jnp.concatenate([jnp.where(lane < HEAD_DIM, xp, zero16),
                                jnp.where(lane < HEAD_DIM, zero16, xp)], axis=0)

    def off(m):
        return (((row2 % (2 * m)) >= m) & ((col2 // (2 * m)) == (row2 // (2 * m)))
                & ((col2 % (2 * m)) < m))

    combos = [(ci, pair) for ci in range(nch) for pair in range(n_pairs)]
    a_st = {q: stack(a_t, *q) for q in combos}
    r_st = {q: stack(r_t, *q) for q in combos}
    v_st = {q: stack(v, *q) for q in combos}
    g = {q: _dot_nt(jnp.concatenate([a_st[q], r_st[q]], axis=0),
                    jnp.concatenate([stack(b_t, *q), stack(k_t, *q)], axis=0)) for q in combos}
    l_ab = {q: jnp.where(strict, g[q][0:2 * c, 0:2 * c], 0.0) for q in combos}
    inv = {q: eye + jnp.where(off(1), l_ab[q], 0.0) for q in combos}
    m = 2
    while m < c:
        inv16 = {q: inv[q].astype(BF16) for q in combos}
        x = {q: _dot(jnp.where(off(m), l_ab[q], 0.0).astype(BF16), inv16[q]) for q in combos}
        inv = {q: inv[q] + _dot(inv16[q], x[q].astype(BF16)) for q in combos}
        m *= 2
    lakv = {q: _dot(jnp.where(strict, g[q][0:2 * c, 2 * c:4 * c], 0.0).astype(BF16), v_st[q])
            for q in combos}
    wu = {q: _dot(inv[q].astype(BF16), jnp.concatenate([a_st[q], lakv[q].astype(BF16)], axis=1))
          for q in combos}
    m_rbk = {q: jnp.concatenate([jnp.where(incl, g[q][2 * c:4 * c, 0:2 * c], 0.0),
                                 jnp.where(incl, g[q][2 * c:4 * c, 2 * c:4 * c], 0.0)],
                                axis=1).astype(BF16) for q in combos}
    bke_st = {q: jnp.concatenate([stack(b_e, *q), stack(k_e, *q)], axis=0) for q in combos}

    st = [state[pair] for pair in range(n_pairs)]
    ys = [[None] * n_pairs for _ in range(nch)]
    for ci in range(nch):
        wr = {pair: _dot(jnp.concatenate([wu[ci, pair][:, 0:LANES].astype(BF16), r_st[ci, pair]],
                                         axis=0), st[pair].astype(BF16)) for pair in range(n_pairs)}
        for pair in range(n_pairs):
            q = (ci, pair)
            u_st = wr[pair][0:2 * c] + wu[q][:, LANES:2 * LANES]
            uv_st = jnp.concatenate([u_st.astype(BF16), v_st[q]], axis=0)
            y_st = wr[pair][2 * c:4 * c] + _dot(m_rbk[q], uv_st)
            decay_col = jnp.broadcast_to(p_end[ci, :, pair * LANES:(pair + 1) * LANES],
                                         (LANES, LANES)).T
            st[pair] = st[pair] * decay_col + _dot_tn(bke_st[q], uv_st)
            ys[ci][pair] = y_st[0:c] + y_st[c:2 * c]
    for pair in range(n_pairs):
        state[pair] = st[pair]
    y = jnp.concatenate([jnp.concatenate(yr, axis=1) for yr in ys], axis=0)

    seg_mean = (seg * (1.0 / HEAD_DIM)).astype(BF16)
    mean = _dot3_lhs(y, seg_mean)
    d = y - mean
    var = _dot3_lhs(d * d, seg_mean)
    yn = d * lax.rsqrt(var + GN_EPS) * lng_ref[...] + lnb_ref[...]
    bonus = _dot3_lhs(r * k * rk_ref[...], seg) * v
    o_ref[...] = yn + bonus


def _rwkv(p32, params, batch, seq, r_blk, k_blk, v_blk, lo_blk, nch=4):
    width = params["w0"].shape[1]
    c = nch * RWKV_CHUNK
    nc = seq // c
    full = lambda a: pl.BlockSpec(a.shape, lambda b, i: (0,) * a.ndim)
    names = ("mu_r", "mu_k", "mu_v", "mu_lo", "w0", "w2", "a0", "a2", "kk", "ka", "rk",
             "lng", "lnb", "seg")
    return pl.pallas_call(
        functools.partial(_rwkv_kernel, nch=nch),
        grid=(batch, nc),
        in_specs=[pl.BlockSpec((c, width), lambda b, i: (b * nc + i, r_blk)),
                  pl.BlockSpec((c, width), lambda b, i: (b * nc + i, k_blk)),
                  pl.BlockSpec((c, width), lambda b, i: (b * nc + i, v_blk)),
                  pl.BlockSpec((c, LANES), lambda b, i: (b * nc + i, lo_blk))]
                 + [full(params[n]) for n in names],
        out_specs=pl.BlockSpec((c, width), lambda b, i: (b * nc + i, 0)),
        out_shape=jax.ShapeDtypeStruct((batch * seq, width), F32),
        scratch_shapes=[pltpu.VMEM((width // LANES, LANES, LANES), F32),
                        pltpu.VMEM((8, width), F32), pltpu.VMEM((8, width), F32),
                        pltpu.VMEM((8, width), F32), pltpu.VMEM((8, LANES), F32)],
        compiler_params=_cparams("parallel", "arbitrary"),
        name="rwkv7",
    )(p32, p32, p32, p32, *[params[n] for n in names])


def _merge_kernel(oc_ref, os_ref, ow_ref, gate_ref, nz_ref, sg_ref, sz_ref, rk_ref, rz_ref,
                  mg_ref, x_ref, p_ref, ec_ref, wb0_ref, wb1_ref, wb2_ref, wo_ref, pleg_ref,
                  wpg_ref, wpp_ref, fin_ref, o_ref, *, last):
    d = x_ref.shape[1]
    gate = _sigmoid(gate_ref[...])
    hi, mid, lo = _split3(gate)
    y_nsa = None
    for i, ref in enumerate((oc_ref, os_ref, ow_ref)):
        e = ec_ref[i]
        term = (_dot(hi, e) + _dot(mid, e) + _dot(lo, e)) * ref[...]
        y_nsa = term if y_nsa is None else y_nsa + term
    z0 = _dot((y_nsa * _silu(nz_ref[...])).astype(BF16), wb0_ref[...])
    z1 = _dot((sg_ref[...] * _silu(sz_ref[...])).astype(BF16), wb1_ref[...])
    z2 = _dot((rk_ref[...] * _silu(rz_ref[...])).astype(BF16), wb2_ref[...])
    merged = (_sigmoid(mg_ref[:, 0:d]) * z0 + _sigmoid(mg_ref[:, d:2 * d]) * z1
              + _sigmoid(mg_ref[:, 2 * d:3 * d]) * z2)
    x1 = x_ref[...] + _dot(merged.astype(BF16), wo_ref[...])
    hp = _rmsnorm(x1, pleg_ref[...]).astype(BF16)
    x2 = x1 + _sigmoid(_dot(hp, wpg_ref[...])) * _dot(p_ref[...].astype(BF16), wpp_ref[...])
    o_ref[...] = _rmsnorm(x2, fin_ref[...]) if last else x2


def _merge(o_c, o_s, o_w, p32, y_sg, y_rk, x2, p2, consts, blocks, last, tm=256):
    t, d = x2.shape
    half = y_sg.shape[1]
    tok = lambda w, j=0: pl.BlockSpec((tm, w), lambda i, j=j: (i, j))
    full = lambda a: pl.BlockSpec(a.shape, lambda i: (0,) * a.ndim)
    names = ("ec", "wb0", "wb1", "wb2", "wo", "pleg", "wpg", "wpp", "fin")
    return pl.pallas_call(
        functools.partial(_merge_kernel, last=last),
        grid=(t // tm,),
        in_specs=[tok(o_c.shape[1]), tok(o_s.shape[1]), tok(o_w.shape[1]),
                  tok(LANES, blocks["gate"]), tok(d, blocks["nz"]),
                  tok(half), tok(half, blocks["sz"]), tok(half), tok(half, blocks["rz"]),
                  tok(3 * d, blocks["mg"]), tok(d), tok(p2.shape[1])]
                 + [full(consts[n]) for n in names],
        out_specs=tok(d),
        out_shape=jax.ShapeDtypeStruct((t, d), F32),
        compiler_params=_cparams("parallel"),
        name="merge_out_ple",
    )(o_c, o_s, o_w, p32, p32, y_sg, p32, y_rk, p32, p32, x2, p2, *[consts[n] for n in names])


def _pad_cols(w, n):
    return jnp.pad(w, ((0, 0), (0, n - w.shape[1])))


def _layer_layout(d_model):
    half = d_model // 2
    kvw = NSA_GROUPS * HEAD_DIM
    sizes = dict(nq=half, nkv=6 * kvw, ngate=3 * NSA_HEADS, nz=half, su=half, sv=half, sz=half,
                 rs=3 * half + 2 * LORA, rz=half, mg=3 * d_model)
    off, out = 0, {}
    for name, n in sizes.items():
        out[name] = (off, n)
        off += n
    return out


def _prep_proj_weights(w_in, d_model):
    lay = _layer_layout(d_model)
    half = d_model // 2
    kvw = NSA_GROUPS * HEAD_DIM
    col = lambda name, a=0, n=None: w_in[:, lay[name][0] + a: lay[name][0] + a + (n or lay[name][1] - a)]
    heads = lambda w: jnp.concatenate(
        [_pad_cols(w[:, h * HEAD_DIM:(h + 1) * HEAD_DIM], LANES) for h in range(w.shape[1] // HEAD_DIM)], axis=1)
    w32 = jnp.concatenate([
        col("mg"), heads(col("nz")), col("su"), col("sv"), col("sz"), col("rz"),
        col("rs", 0, half), col("rs", half, half), col("rs", 2 * half, half),
        col("nkv", 0, 2 * kvw), col("rs", 3 * half, 2 * LORA), _pad_cols(col("ngate"), LANES)], axis=1)
    o = 3 * d_model + NSA_HEADS * LANES
    blocks = dict(mg=0, nz=(3 * d_model) // (NSA_HEADS * LANES))
    for name in ("su", "sv", "sz", "rz", "rr", "rk", "rv"):
        blocks[name] = o // half
        o += half
    blocks["kcvc"] = o
    o += 2 * kvw
    blocks["lora"] = o // LANES
    o += LANES
    blocks["gate"] = o // LANES
    o += LANES
    zero = jnp.zeros((w_in.shape[0], LANES), w_in.dtype)
    grp = lambda w, g: _pad_cols(w[:, g * HEAD_DIM:(g + 1) * HEAD_DIM], LANES)
    ks, vs = col("nkv", 2 * kvw, kvw), col("nkv", 3 * kvw, kvw)
    kw, vw = col("nkv", 4 * kvw, kvw), col("nkv", 5 * kvw, kvw)
    w16 = jnp.concatenate(
        [heads(col("nq"))]
        + [x for g in range(NSA_GROUPS) for x in (grp(ks, g), zero)]
        + [grp(vs, g) for g in range(NSA_GROUPS)]
        + [grp(kw, g) for g in range(NSA_GROUPS)]
        + [grp(vw, g) for g in range(NSA_GROUPS)], axis=1)
    return w32.astype(BF16), w16.astype(BF16), blocks


def _prep_compress_weights(cmp_w1, cmp_w2, cmp_pe):
    half_blk = CMP_BLOCK // 2
    w1r = cmp_w1.reshape(2, 2, half_blk, HEAD_DIM, CMP_HIDDEN)
    eye = jnp.eye(NSA_GROUPS, dtype=cmp_w1.dtype)
    eye_kv = jnp.eye(2, dtype=cmp_w1.dtype)
    big = jnp.einsum('kaldh,kK,gG->alkgdKGh', w1r, eye_kv, eye)
    big = big.reshape(2, half_blk * 2 * NSA_GROUPS * HEAD_DIM, 2 * NSA_GROUPS * CMP_HIDDEN)
    pe = cmp_pe.reshape(2, 2, half_blk, HEAD_DIM)
    pe2 = jnp.broadcast_to(pe.transpose(1, 2, 0, 3)[:, :, :, None, :],
                           (2, half_blk, 2, NSA_GROUPS, HEAD_DIM)).reshape(2, -1)
    w2p = jnp.pad(cmp_w2, ((0, 0), (0, 0), (0, LANES - HEAD_DIM)))
    return pe2, big[0].astype(BF16), big[1].astype(BF16), w2p.astype(BF16)


def _cover_t(seq):
    n_grp = seq // CMP_STRIDE
    n_slc = seq // SEL_BLOCK
    cmp_start = jnp.arange(n_grp) * CMP_STRIDE
    slc_start = jnp.arange(n_slc) * SEL_BLOCK
    cover = ((cmp_start[None, :] <= slc_start[:, None] + SEL_BLOCK - 1)
             & (cmp_start[None, :] + CMP_BLOCK - 1 >= slc_start[:, None]))
    return jnp.pad(cover, ((0, LANES - n_slc), (0, 0))).astype(BF16)


def _gate_expand():
    c = jnp.arange(LANES)[:, None]
    col = jnp.arange(NSA_HEADS * LANES)[None, :]
    return jnp.stack([(c == (col // LANES) * 3 + i) for i in range(3)]).astype(BF16)


def kernel(x, p, norm_g, w_in, cmp_w1, cmp_w2, cmp_pe, sg_ln_g, sg_ln_b, sg_w, sg_b, rk_mu, rk_w0,
           rk_w2, rk_a0, rk_a2, rk_kk, rk_ka, rk_rk, rk_lnx_g, rk_lnx_b, w_branch, w_o, ple_norm_g,
           w_ple_gate, w_ple_proj, final_norm_g):
    batch, seq, d = x.shape
    depth = w_in.shape[0]
    half = d // 2
    t = batch * seq
    assert seq % SEL_KEY_TILE == 0 or seq < SEL_KEY_TILE
    assert seq >= WINDOW + Q_TILE and seq // SEL_BLOCK <= LANES
    x2 = x.reshape(t, d)
    cov_t = _cover_t(seq)
    ec = _gate_expand()
    seg = (jnp.arange(half)[:, None] // HEAD_DIM == jnp.arange(half)[None, :] // HEAD_DIM).astype(BF16)
    row = lambda v: v.reshape(1, -1)
    for i in range(depth):
        w32, w16, blk = _prep_proj_weights(w_in[i], d)
        g = row(norm_g[i])
        p32 = _proj32(x2, g, w32)
        p16 = _proj16(x2, g, w16, seq)

        pe2, wa, wb, w2p = _prep_compress_weights(cmp_w1[i], cmp_w2[i], cmp_pe[i])
        kcvc = p32[:, blk["kcvc"]:blk["kcvc"] + 2 * NSA_GROUPS * HEAD_DIM]
        xg = kcvc.reshape(batch, seq // CMP_STRIDE, CMP_STRIDE * 2 * NSA_GROUPS * HEAD_DIM)
        kc, vc = _compress(xg, pe2, wa, wb, w2p)
        o_c, bias = _cmp_select(p16, kc, vc, cov_t, batch, seq)
        o_s = _sel_attn(p16, bias, batch, seq)
        o_w = _win_attn(p16, batch, seq)

        bexp = jnp.repeat(sg_b[i].T, half // SG_GROUPS, axis=1)
        y_sg = _spatial_gating(p32, row(sg_ln_g[i]), row(sg_ln_b[i]), sg_w[i], bexp, blk["su"], blk["sv"])

        mu = rk_mu[i]
        rk_params = dict(
            mu_r=row(mu[0:half]), mu_k=row(mu[half:2 * half]), mu_v=row(mu[2 * half:3 * half]),
            mu_lo=row(mu[3 * half:]), w0=row(rk_w0[i]),
            w2=jnp.pad(rk_w2[i], ((0, LANES - LORA), (0, 0))).astype(BF16), a0=row(rk_a0[i]),
            a2=jnp.pad(rk_a2[i], ((LANES - LORA, 0), (0, 0))).astype(BF16),
            kk=row(rk_kk[i]), ka=row(rk_ka[i]), rk=row(rk_rk[i]), lng=row(rk_lnx_g[i]),
            lnb=row(rk_lnx_b[i]), seg=seg)
        y_rk = _rwkv(p32, rk_params, batch, seq, blk["rr"], blk["rk"], blk["rv"], blk["lora"])

        wb0 = jnp.pad(w_branch[i, 0].reshape(NSA_HEADS, HEAD_DIM, d),
                      ((0, 0), (0, LANES - HEAD_DIM), (0, 0))).reshape(NSA_HEADS * LANES, d)
        consts = dict(ec=ec, wb0=wb0.astype(BF16), wb1=w_branch[i, 1].astype(BF16),
                      wb2=w_branch[i, 2].astype(BF16), wo=w_o[i].astype(BF16),
                      pleg=row(ple_norm_g[i]), wpg=w_ple_gate[i].astype(BF16),
                      wpp=w_ple_proj[i].astype(BF16), fin=row(final_norm_g))
        x2 = _merge(o_c, o_s, o_w, p32, y_sg, y_rk, x2, p[i].reshape(t, -1), consts, blk,
                    last=(i == depth - 1))
    return x2.reshape(batch, seq, d)
```

```python
import functools

import jax
import jax.numpy as jnp
from jax import lax
from jax.experimental import pallas as pl
from jax.experimental.pallas import tpu as pltpu

F32 = jnp.float32
BF16 = jnp.bfloat16

HEAD_DIM = 64
LANES = 128
NSA_HEADS = 8
NSA_GROUPS = 2
NSA_HPG = NSA_HEADS // NSA_GROUPS
CMP_BLOCK = 32
CMP_STRIDE = 16
CMP_HIDDEN = 128
SEL_BLOCK = 64
SEL_TOPK = 16
WINDOW = 512
Q_TILE = 128
SEL_KEY_TILE = 512
SG_GROUPS = 8
SG_CHUNK = 128
RWKV_HEADS = 8
RWKV_CHUNK = 64
LORA = 64
NORM_EPS = 1e-6
LN_EPS = 1e-5
GN_EPS = 64e-5
MASK_NEG = -1e30
FORCE_BONUS = 1e4
VMEM_LIMIT = 56 * 1024 * 1024


def _cparams(*sem):
    return pltpu.CompilerParams(dimension_semantics=sem, vmem_limit_bytes=VMEM_LIMIT)


def _dot(a, b, precision=None):
    return jnp.dot(a, b, preferred_element_type=F32, precision=precision)


def _dot_nt(a, b, precision=None):
    return lax.dot_general(a, b, (((1,), (1,)), ((), ())), preferred_element_type=F32,
                           precision=precision)


def _dot_tn(a, b, precision=None):
    return lax.dot_general(a, b, (((0,), (0,)), ((), ())), preferred_element_type=F32,
                           precision=precision)


def _split3(x):
    hi = x.astype(BF16)
    r1 = x - hi.astype(F32)
    mid = r1.astype(BF16)
    lo = (r1 - mid.astype(F32)).astype(BF16)
    return hi, mid, lo


def _dot3_lhs(x, w):
    hi, mid, lo = _split3(x)
    return _dot(hi, w) + _dot(mid, w) + _dot(lo, w)


def _dot3_rhs(w, x):
    hi, mid, lo = _split3(x)
    return _dot(w, hi) + _dot(w, mid) + _dot(w, lo)


def _sigmoid(x):
    return 1.0 / (1.0 + jnp.exp(-x))


def _silu(x):
    return x * _sigmoid(x)


def _rmsnorm(x, g):
    return x * lax.rsqrt(jnp.mean(x * x, axis=-1, keepdims=True) + NORM_EPS) * g


P16_Q = 0
P16_KX = NSA_HEADS * LANES
P16_VX = P16_KX + NSA_GROUPS * 2 * LANES
P16_KW = P16_VX + NSA_GROUPS * LANES
P16_VW = P16_KW + NSA_GROUPS * LANES
P16_N = P16_VW + NSA_GROUPS * LANES


PROJ_CHUNK = 2 * LANES


def _proj_kernel(x_ref, g_ref, w32_ref, w16_ref, o32_ref, o16_ref, *, seq, tm):
    h = _rmsnorm(x_ref[...], g_ref[...]).astype(BF16)
    for c0 in range(0, o32_ref.shape[1], 2 * PROJ_CHUNK):
        o32_ref[:, c0:c0 + 2 * PROJ_CHUNK] = _dot(h, w32_ref[:, c0:c0 + 2 * PROJ_CHUNK])
    pos = (pl.program_id(0) * tm) % seq + lax.broadcasted_iota(jnp.int32, (tm, LANES), 0)
    lane = lax.broadcasted_iota(jnp.int32, (tm, LANES), 1)
    onehot_blk = jnp.where(pos // SEL_BLOCK == lane, 1.0, 0.0)
    ones_col = jnp.where(lane == HEAD_DIM, 1.0, 0.0)
    zeros = jnp.zeros((tm, LANES), F32)

    def constant(c0):
        if P16_KX <= c0 < P16_VX and ((c0 - P16_KX) // LANES) % 2 == 1:
            return onehot_blk
        if c0 >= P16_VX and not (P16_KW <= c0 < P16_VW):
            return ones_col
        return zeros

    for c0 in range(0, P16_N, PROJ_CHUNK):
        y = _dot(h, w16_ref[:, c0:c0 + PROJ_CHUNK])
        consts = [constant(c0 + i * LANES) for i in range(PROJ_CHUNK // LANES)]
        if any(c is not zeros for c in consts):
            y = y + jnp.concatenate(consts, axis=1)
        o16_ref[:, c0:c0 + PROJ_CHUNK] = y.astype(BF16)


def _proj(x2, g, w32, w16, seq, tm=512):
    t, d = x2.shape
    n32 = w32.shape[1]
    const = lambda a: pl.BlockSpec(a.shape, lambda i: (0, 0), pipeline_mode=pl.Buffered(1))
    return pl.pallas_call(
        functools.partial(_proj_kernel, seq=seq, tm=tm),
        grid=(t // tm,),
        in_specs=[pl.BlockSpec((tm, d), lambda i: (i, 0)), const(g), const(w32), const(w16)],
        out_specs=[pl.BlockSpec((tm, n32), lambda i: (i, 0)),
                   pl.BlockSpec((tm, P16_N), lambda i: (i, 0))],
        out_shape=[jax.ShapeDtypeStruct((t, n32), F32), jax.ShapeDtypeStruct((t, P16_N), BF16)],
        compiler_params=_cparams("parallel"),
        name="proj",
    )(x2, g, w32, w16)


def _compress_kernel(x_ref, pe_ref, wa_ref, wb_ref, w2_ref, k_ref, v_ref):
    x = x_ref[0]
    n_grp = x.shape[0]
    a = _dot((x + pe_ref[0:1, :]).astype(BF16), wa_ref[...])
    b = _dot((x + pe_ref[1:2, :]).astype(BF16), wb_ref[...])
    hid = _silu(a + pltpu.roll(b, n_grp - 1, 0)).astype(BF16)
    row = lax.broadcasted_iota(jnp.int32, (n_grp, LANES), 0)
    lane = lax.broadcasted_iota(jnp.int32, (n_grp, LANES), 1)
    live = row < n_grp - 1
    for kv, out_ref in enumerate((k_ref, v_ref)):
        for g in range(NSA_GROUPS):
            c0 = (kv * NSA_GROUPS + g) * CMP_HIDDEN
            y = _dot(hid[:, c0:c0 + CMP_HIDDEN], w2_ref[kv])
            if kv == 1:
                y = y + jnp.where(lane == HEAD_DIM, 1.0, 0.0)
            out_ref[0, :, g * LANES:(g + 1) * LANES] = jnp.where(live, y, 0.0).astype(BF16)


def _compress(xg, pe2, wa, wb, w2p):
    b, n_grp, width = xg.shape
    out = jax.ShapeDtypeStruct((b, n_grp, NSA_GROUPS * LANES), BF16)
    return pl.pallas_call(
        _compress_kernel,
        grid=(b,),
        in_specs=[pl.BlockSpec((1, n_grp, width), lambda i: (i, 0, 0)),
                  pl.BlockSpec(pe2.shape, lambda i: (0, 0)),
                  pl.BlockSpec(wa.shape, lambda i: (0, 0)),
                  pl.BlockSpec(wb.shape, lambda i: (0, 0)),
                  pl.BlockSpec(w2p.shape, lambda i: (0, 0, 0))],
        out_specs=[pl.BlockSpec((1, n_grp, NSA_GROUPS * LANES), lambda i: (i, 0, 0))] * 2,
        out_shape=[out, out],
        compiler_params=_cparams("parallel"),
        name="nsa_compress",
    )(xg, pe2, wa, wb, w2p)


def _stack_heads(q_ref, g):
    parts = [q_ref[:, (g * NSA_HPG + h) * LANES:(g * NSA_HPG + h + 1) * LANES]
             for h in range(NSA_HPG)]
    return jnp.concatenate(parts, axis=0) * jnp.asarray(HEAD_DIM ** -0.5, BF16)


def _store_heads(o_ref, gate_ref, branch, g, o):
    lane = lax.broadcasted_iota(jnp.int32, (Q_TILE, LANES), 1)
    gate = _sigmoid(gate_ref[...])

    def scaled(h):
        col = (g * NSA_HPG + h) * 3 + branch
        return jnp.broadcast_to(gate[:, col:col + 1], (Q_TILE, LANES)) * o[h * Q_TILE:(h + 1) * Q_TILE]

    for j in range(NSA_HPG // 2):
        tile = jnp.where(lane < HEAD_DIM, scaled(2 * j), pltpu.roll(scaled(2 * j + 1), HEAD_DIM, 1))
        c0 = (g * NSA_HPG // 2 + j) * LANES
        o_ref[:, c0:c0 + LANES] = tile


def _cmp_select_kernel(q_ref, gate_ref, kc_ref, vc_ref, cov_ref, o_ref, bias_ref, *, top_k):
    n_cmp = kc_ref.shape[1]
    n_slc = cov_ref.shape[0]
    q0 = pl.program_id(1) * Q_TILE
    rows = NSA_HPG * Q_TILE
    t_row = q0 + lax.broadcasted_iota(jnp.int32, (rows, n_cmp), 0) % Q_TILE
    cmp_end = lax.broadcasted_iota(jnp.int32, (rows, n_cmp), 1) * CMP_STRIDE + CMP_BLOCK - 1
    valid = cmp_end <= t_row
    sel_shape = (n_slc, NSA_GROUPS * Q_TILE)
    j_idx = lax.broadcasted_iota(jnp.int32, sel_shape, 0)
    t_lane = q0 + lax.broadcasted_iota(jnp.int32, sel_shape, 1) % Q_TILE
    t_blk = t_lane // SEL_BLOCK
    forced = (j_idx == 0) | (j_idx == t_blk) | (j_idx == t_blk - 1)
    causal_blk = j_idx <= t_blk
    imp_t = []
    s_all = [_dot_nt(_stack_heads(q_ref, g), kc_ref[0, :, g * LANES:(g + 1) * LANES])
             for g in range(NSA_GROUPS)]
    for g in range(NSA_GROUPS):
        s = jnp.where(valid, s_all[g], MASK_NEG)
        m = jnp.max(s, axis=-1, keepdims=True)
        p = jnp.where(valid, jnp.exp(s - m), 0.0)
        l = jnp.sum(p, axis=-1, keepdims=True)
        p = p * jnp.where(l > 0.0, 1.0 / l, 0.0)
        o = _dot(p.astype(BF16), vc_ref[0, :, g * LANES:(g + 1) * LANES])
        _store_heads(o_ref, gate_ref, 0, g, o)
        p_sum = p[0:Q_TILE]
        for h in range(1, NSA_HPG):
            p_sum = p_sum + p[h * Q_TILE:(h + 1) * Q_TILE]
        hi, mid, lo = _split3(p_sum)
        cov = cov_ref[...]
        imp_t.append(_dot_nt(cov, hi) + _dot_nt(cov, mid) + _dot_nt(cov, lo))
    score = jnp.where(causal_blk, jnp.where(forced, FORCE_BONUS, jnp.concatenate(imp_t, axis=1)),
                      -FORCE_BONUS)

    def pick(_, carry):
        sc, sel = carry
        best = jnp.max(sc, axis=0, keepdims=True)
        first = jnp.min(jnp.where(sc == best, j_idx, n_slc), axis=0, keepdims=True)
        hit = j_idx == first
        return jnp.where(hit, -jnp.inf, sc), jnp.where(hit, 1.0, sel)

    _, sel = lax.fori_loop(0, top_k, pick, (score, jnp.zeros(sel_shape, F32)))
    bias_t = jnp.where((sel > 0.0) & causal_blk, 0.0, MASK_NEG)
    for g in range(NSA_GROUPS):
        bias_ref[:, g * n_slc:(g + 1) * n_slc] = bias_t[:, g * Q_TILE:(g + 1) * Q_TILE].T.astype(BF16)


def _cmp_select(p16, p32, gate_blk, kc, vc, cov_t, batch, seq):
    n_cmp = kc.shape[1]
    n_slc = cov_t.shape[0]
    nq = seq // Q_TILE
    return pl.pallas_call(
        functools.partial(_cmp_select_kernel, top_k=min(SEL_TOPK, seq // SEL_BLOCK)),
        grid=(batch, nq),
        in_specs=[pl.BlockSpec((Q_TILE, NSA_HEADS * LANES), lambda b, i: (b * nq + i, 0)),
                  pl.BlockSpec((Q_TILE, LANES), lambda b, i: (b * nq + i, gate_blk)),
                  pl.BlockSpec((1, n_cmp, NSA_GROUPS * LANES), lambda b, i: (b, 0, 0)),
                  pl.BlockSpec((1, n_cmp, NSA_GROUPS * LANES), lambda b, i: (b, 0, 0)),
                  pl.BlockSpec(cov_t.shape, lambda b, i: (0, 0))],
        out_specs=[pl.BlockSpec((Q_TILE, NSA_HEADS * HEAD_DIM), lambda b, i: (b * nq + i, 0)),
                   pl.BlockSpec((Q_TILE, NSA_GROUPS * n_slc), lambda b, i: (b * nq + i, 0))],
        out_shape=[jax.ShapeDtypeStruct((batch * seq, NSA_HEADS * HEAD_DIM), F32),
                   jax.ShapeDtypeStruct((batch * seq, NSA_GROUPS * n_slc), BF16)],
        compiler_params=_cparams("parallel", "arbitrary"),
        name="nsa_cmp_select",
    )(p16, p32, kc, vc, cov_t)


def _sel_attn_kernel(q_ref, gate_ref, bias_ref, kx_ref, vx_ref, o_ref, m_scr, acc_scr, *, tk):
    n_slc = bias_ref.shape[1] // NSA_GROUPS
    q0 = pl.program_id(1) * Q_TILE
    rows = NSA_HPG * Q_TILE
    n_past = q0 // tk
    qx = []
    for g in range(NSA_GROUPS):
        bias = bias_ref[:, g * n_slc:(g + 1) * n_slc]
        qx.append(jnp.concatenate([_stack_heads(q_ref, g),
                                   jnp.concatenate([bias] * NSA_HPG, axis=0)], axis=1))

    for g in range(NSA_GROUPS):
        m_scr[g] = jnp.full((rows, LANES), MASK_NEG, F32)
        acc_scr[g] = jnp.zeros((rows, LANES), F32)

    def step(k0, width, diagonal):
        for g in range(NSA_GROUPS):
            s = _dot_nt(qx[g], kx_ref[pl.ds(k0, width), g * 2 * LANES:(g + 1) * 2 * LANES])
            if diagonal:
                t_row = q0 + lax.broadcasted_iota(jnp.int32, (rows, width), 0) % Q_TILE
                key = k0 + lax.broadcasted_iota(jnp.int32, (rows, width), 1)
                s = jnp.where(key <= t_row, s, MASK_NEG)
            m_old = m_scr[g]
            m_new = jnp.maximum(m_old, jnp.max(s, axis=-1, keepdims=True))
            p = jnp.exp(s - jnp.tile(m_new, (1, width // LANES))).astype(BF16)
            pv = _dot(p, vx_ref[pl.ds(k0, width), g * LANES:(g + 1) * LANES])
            acc_scr[g] = jnp.exp(m_old - m_new) * acc_scr[g] + pv
            m_scr[g] = m_new

    def wide(j, _):
        step(pl.multiple_of(j * 2 * tk, 2 * tk), 2 * tk, False)
        return 0

    lax.fori_loop(0, n_past // 2, wide, 0)

    @pl.when(n_past % 2 == 1)
    def _():
        step(pl.multiple_of((n_past - 1) * tk, tk), tk, False)

    step(pl.multiple_of(n_past * tk, tk), tk, True)
    for g in range(NSA_GROUPS):
        acc = acc_scr[g]
        _store_heads(o_ref, gate_ref, 1, g, acc / acc[:, HEAD_DIM:HEAD_DIM + 1])


def _sel_attn(p16, p32, gate_blk, bias, batch, seq):
    nq = seq // Q_TILE
    tk = min(SEL_KEY_TILE, seq)
    kx_w = NSA_GROUPS * 2 * LANES
    vx_w = NSA_GROUPS * LANES
    return pl.pallas_call(
        functools.partial(_sel_attn_kernel, tk=tk),
        grid=(batch, nq),
        in_specs=[pl.BlockSpec((Q_TILE, NSA_HEADS * LANES), lambda b, i: (b * nq + i, 0)),
                  pl.BlockSpec((Q_TILE, LANES), lambda b, i: (b * nq + i, gate_blk)),
                  pl.BlockSpec((Q_TILE, bias.shape[1]), lambda b, i: (b * nq + i, 0)),
                  pl.BlockSpec((seq, kx_w), lambda b, i: (b, P16_KX // kx_w)),
                  pl.BlockSpec((seq, vx_w), lambda b, i: (b, P16_VX // vx_w))],
        out_specs=pl.BlockSpec((Q_TILE, NSA_HEADS * HEAD_DIM), lambda b, i: (b * nq + i, 0)),
        out_shape=jax.ShapeDtypeStruct((batch * seq, NSA_HEADS * HEAD_DIM), F32),
        scratch_shapes=[pltpu.VMEM((NSA_GROUPS, NSA_HPG * Q_TILE, LANES), F32),
                        pltpu.VMEM((NSA_GROUPS, NSA_HPG * Q_TILE, LANES), F32)],
        compiler_params=_cparams("parallel", "arbitrary"),
        name="nsa_sel_attn",
    )(p16, p32, bias, p16, p16)


def _win_attn_kernel(q_ref, gate_ref, kw_ref, vw_ref, o_ref, *, span):
    seq = kw_ref.shape[0]
    q0 = pl.program_id(1) * Q_TILE
    rows = NSA_HPG * Q_TILE
    start = pl.multiple_of(jnp.clip(q0 + Q_TILE - span, 0, seq - span), Q_TILE)
    t_row = q0 + lax.broadcasted_iota(jnp.int32, (rows, span), 0) % Q_TILE
    pos = start + lax.broadcasted_iota(jnp.int32, (rows, span), 1)
    valid = (pos <= t_row) & (pos > t_row - WINDOW)
    s_all = [_dot_nt(_stack_heads(q_ref, g), kw_ref[pl.ds(start, span), g * LANES:(g + 1) * LANES])
             for g in range(NSA_GROUPS)]
    p_all = []
    for g in range(NSA_GROUPS):
        s = jnp.where(valid, s_all[g], MASK_NEG)
        m = jnp.max(s, axis=-1, keepdims=True)
        p_all.append(jnp.where(valid, jnp.exp(s - m), 0.0).astype(BF16))
    for g in range(NSA_GROUPS):
        acc = _dot(p_all[g], vw_ref[pl.ds(start, span), g * LANES:(g + 1) * LANES])
        _store_heads(o_ref, gate_ref, 2, g, acc / acc[:, HEAD_DIM:HEAD_DIM + 1])


def _win_attn(p16, p32, gate_blk, batch, seq):
    nq = seq // Q_TILE
    span = min(WINDOW + Q_TILE, seq)
    w = NSA_GROUPS * LANES
    return pl.pallas_call(
        functools.partial(_win_attn_kernel, span=span),
        grid=(batch, nq),
        in_specs=[pl.BlockSpec((Q_TILE, NSA_HEADS * LANES), lambda b, i: (b * nq + i, 0)),
                  pl.BlockSpec((Q_TILE, LANES), lambda b, i: (b * nq + i, gate_blk)),
                  pl.BlockSpec((seq, w), lambda b, i: (b, P16_KW // w)),
                  pl.BlockSpec((seq, w), lambda b, i: (b, P16_VW // w))],
        out_specs=pl.BlockSpec((Q_TILE, NSA_HEADS * HEAD_DIM), lambda b, i: (b * nq + i, 0)),
        out_shape=jax.ShapeDtypeStruct((batch * seq, NSA_HEADS * HEAD_DIM), F32),
        compiler_params=_cparams("parallel", "arbitrary"),
        name="nsa_win_attn",
    )(p16, p32, p16, p16)


def _spatial_gating_kernel(u_ref, v_ref, g_ref, b_ref, w_ref, bexp_ref, o_ref, *, chunks):
    c = SG_CHUNK
    tri = (lax.broadcasted_iota(jnp.int32, (c, c), 1) <= lax.broadcasted_iota(jnp.int32, (c, c), 0))
    lane = lax.broadcasted_iota(jnp.int32, (c, LANES), 1)
    wm = [jnp.where(tri, w_ref[g], 0.0).astype(BF16) for g in range(SG_GROUPS)]
    for ci in range(chunks):
        v = v_ref[ci * c:(ci + 1) * c, :]
        mu = jnp.mean(v, axis=-1, keepdims=True)
        d = v - mu
        var = jnp.mean(d * d, axis=-1, keepdims=True)
        vn = (d * lax.rsqrt(var + LN_EPS) * g_ref[...] + b_ref[...]).astype(BF16)
        for pair in range(SG_GROUPS // 2):
            vp = vn[:, pair * LANES:(pair + 1) * LANES]
            mixed = jnp.where(lane < HEAD_DIM, _dot(wm[2 * pair], vp), _dot(wm[2 * pair + 1], vp))
            sl = (slice(ci * c, (ci + 1) * c), slice(pair * LANES, (pair + 1) * LANES))
            o_ref[sl] = u_ref[sl] * (mixed + bexp_ref[:, pair * LANES:(pair + 1) * LANES])


def _spatial_gating(p32, ln_g, ln_b, w_s, bexp, u_blk, v_blk, chunks=4):
    t = p32.shape[0]
    width = ln_g.shape[1]
    tm = chunks * SG_CHUNK
    return pl.pallas_call(
        functools.partial(_spatial_gating_kernel, chunks=chunks),
        grid=(t // tm,),
        in_specs=[pl.BlockSpec((tm, width), lambda i: (i, u_blk)),
                  pl.BlockSpec((tm, width), lambda i: (i, v_blk)),
                  pl.BlockSpec((1, width), lambda i: (0, 0)),
                  pl.BlockSpec((1, width), lambda i: (0, 0)),
                  pl.BlockSpec(w_s.shape, lambda i: (0, 0, 0)),
                  pl.BlockSpec(bexp.shape, lambda i: (0, 0))],
        out_specs=pl.BlockSpec((tm, width), lambda i: (i, 0)),
        out_shape=jax.ShapeDtypeStruct((t, width), F32),
        compiler_params=_cparams("parallel"),
        name="spatial_gating",
    )(p32, p32, ln_g, ln_b, w_s, bexp)


def _rwkv_kernel(r_ref, k_ref, v_ref, lo_ref, mu_r, mu_k, mu_v, mu_lo, w0_ref, w2_ref, a0_ref,
                 a2_ref, kk_ref, ka_ref, rk_ref, lng_ref, lnb_ref, seg_ref, o_ref,
                 state, prev_r, prev_k, prev_v, prev_lo, *, nch):
    c = RWKV_CHUNK
    rows = nch * c
    width = r_ref.shape[1]
    n_pairs = width // LANES

    @pl.when(pl.program_id(1) == 0)
    def _():
        state[...] = jnp.zeros_like(state)
        for ref in (prev_r, prev_k, prev_v, prev_lo):
            ref[...] = jnp.zeros_like(ref)

    def token_shift(x_ref, prev_ref, mu_ref):
        x = x_ref[...]
        first = lax.broadcasted_iota(jnp.int32, x.shape, 0) == 0
        prev = jnp.where(first, prev_ref[0:1, :], pltpu.roll(x, 1, 0))
        prev_ref[0:1, :] = x[rows - 1:rows, :]
        return x + (prev - x) * mu_ref[...]

    r = token_shift(r_ref, prev_r, mu_r)
    k = token_shift(k_ref, prev_k, mu_k)
    v = token_shift(v_ref, prev_v, mu_v)
    lo = token_shift(lo_ref, prev_lo, mu_lo)

    lane_lo = lax.broadcasted_iota(jnp.int32, lo.shape, 1)
    w_in = jnp.where(lane_lo < LORA, jnp.tanh(lo), 0.0).astype(BF16)
    a_in = jnp.where(lane_lo < LORA, 0.0, lo).astype(BF16)
    z = -(w0_ref[...] + _dot(w_in, w2_ref[...]))
    w = -(jnp.maximum(z, 0.0) + jnp.log(1.0 + jnp.exp(-jnp.abs(z)))) - 0.5
    a = _sigmoid(a0_ref[...] + _dot(a_in, a2_ref[...]))

    seg = seg_ref[...]
    kkf = k * kk_ref[...]
    kk = kkf / jnp.maximum(jnp.sqrt(_dot3_lhs(kkf * kkf, seg)), 1e-12)
    k = k * (1.0 + (a - 1.0) * ka_ref[...])
    logd = -jnp.exp(w)
    t_row = lax.broadcasted_iota(jnp.int32, (rows, rows), 0)
    t_col = lax.broadcasted_iota(jnp.int32, (rows, rows), 1)
    cum = (t_col <= t_row) & (t_col // c == t_row // c)
    cl = _dot3_rhs(jnp.where(cum, 1.0, 0.0).astype(BF16), logd)
    cl_end = cl.reshape(nch, c, width)[:, c - 1:c, :]
    e_neg = jnp.exp(-cl)
    e_end = jnp.exp(jnp.broadcast_to(cl_end, (nch, c, width)).reshape(rows, width) - cl)
    b = kk * a
    a_t = -kk * jnp.exp(cl - logd)
    b_t = b * e_neg
    k_t = k * e_neg
    r_t = r * jnp.exp(cl)
    b_e = b * e_end
    k_e = k * e_end
    p_end = jnp.exp(cl_end)

    lane = lax.broadcasted_iota(jnp.int32, (c, LANES), 1)
    row2 = lax.broadcasted_iota(jnp.int32, (2 * c, 2 * c), 0)
    col2 = lax.broadcasted_iota(jnp.int32, (2 * c, 2 * c), 1)
    same_head = (row2 // c) == (col2 // c)
    strict = same_head & (col2 < row2)
    incl = same_head & (col2 <= row2)
    eye = jnp.where(row2 == col2, 1.0, 0.0)
    zero16 = jnp.zeros((c, LANES), BF16)

    def stack(x, ci, pair):
        xp = x[ci * c:(ci + 1) * c, pair * LANES:(pair + 1) * LANES].astype(BF16)
        return jnp.concatenate([jnp.where(lane < HEAD_DIM, xp, zero16),
                                jnp.where(lane < HEAD_DIM, zero16, xp)], axis=0)

    def off(m):
        return (((row2 % (2 * m)) >= m) & ((col2 // (2 * m)) == (row2 // (2 * m)))
                & ((col2 % (2 * m)) < m))

    combos = [(ci, pair) for ci in range(nch) for pair in range(n_pairs)]
    a_st = {q: stack(a_t, *q) for q in combos}
    r_st = {q: stack(r_t, *q) for q in combos}
    v_st = {q: stack(v, *q) for q in combos}
    g = {q: _dot_nt(jnp.concatenate([a_st[q], r_st[q]], axis=0),
                    jnp.concatenate([stack(b_t, *q), stack(k_t, *q)], axis=0)) for q in combos}
    l_ab = {q: jnp.where(strict, g[q][0:2 * c, 0:2 * c], 0.0) for q in combos}
    inv = {q: eye + jnp.where(off(1), l_ab[q], 0.0) for q in combos}
    m = 2
    while m < c:
        inv16 = {q: inv[q].astype(BF16) for q in combos}
        x = {q: _dot(jnp.where(off(m), l_ab[q], 0.0).astype(BF16), inv16[q]) for q in combos}
        inv = {q: inv[q] + _dot(inv16[q], x[q].astype(BF16)) for q in combos}
        m *= 2
    lakv = {q: _dot(jnp.where(strict, g[q][0:2 * c, 2 * c:4 * c], 0.0).astype(BF16), v_st[q])
            for q in combos}
    wu = {q: _dot(inv[q].astype(BF16), jnp.concatenate([a_st[q], lakv[q].astype(BF16)], axis=1))
          for q in combos}
    m_rbk = {q: jnp.concatenate([jnp.where(incl, g[q][2 * c:4 * c, 0:2 * c], 0.0),
                                 jnp.where(incl, g[q][2 * c:4 * c, 2 * c:4 * c], 0.0)],
                                axis=1).astype(BF16) for q in combos}
    bke_st = {q: jnp.concatenate([stack(b_e, *q), stack(k_e, *q)], axis=0) for q in combos}

    st = [state[pair] for pair in range(n_pairs)]
    ys = [[None] * n_pairs for _ in range(nch)]
    for ci in range(nch):
        wr = {pair: _dot(jnp.concatenate([wu[ci, pair][:, 0:LANES].astype(BF16), r_st[ci, pair]],
                                         axis=0), st[pair].astype(BF16)) for pair in range(n_pairs)}
        for pair in range(n_pairs):
            q = (ci, pair)
            u_st = wr[pair][0:2 * c] + wu[q][:, LANES:2 * LANES]
            uv_st = jnp.concatenate([u_st.astype(BF16), v_st[q]], axis=0)
            y_st = wr[pair][2 * c:4 * c] + _dot(m_rbk[q], uv_st)
            decay_col = jnp.broadcast_to(p_end[ci, :, pair * LANES:(pair + 1) * LANES],
                                         (LANES, LANES)).T
            st[pair] = st[pair] * decay_col + _dot_tn(bke_st[q], uv_st)
            ys[ci][pair] = y_st[0:c] + y_st[c:2 * c]
    for pair in range(n_pairs):
        state[pair] = st[pair]
    y = jnp.concatenate([jnp.concatenate(yr, axis=1) for yr in ys], axis=0)

    seg_mean = (seg * (1.0 / HEAD_DIM)).astype(BF16)
    mean = _dot3_lhs(y, seg_mean)
    d = y - mean
    var = _dot3_lhs(d * d, seg_mean)
    yn = d * lax.rsqrt(var + GN_EPS) * lng_ref[...] + lnb_ref[...]
    bonus = _dot3_lhs(r * k * rk_ref[...], seg) * v
    o_ref[...] = yn + bonus


def _rwkv(p32, params, batch, seq, r_blk, k_blk, v_blk, lo_blk, nch=4):
    width = params["w0"].shape[1]
    c = nch * RWKV_CHUNK
    nc = seq // c
    full = lambda a: pl.BlockSpec(a.shape, lambda b, i: (0,) * a.ndim)
    names = ("mu_r", "mu_k", "mu_v", "mu_lo", "w0", "w2", "a0", "a2", "kk", "ka", "rk",
             "lng", "lnb", "seg")
    return pl.pallas_call(
        functools.partial(_rwkv_kernel, nch=nch),
        grid=(batch, nc),
        in_specs=[pl.BlockSpec((c, width), lambda b, i: (b * nc + i, r_blk)),
                  pl.BlockSpec((c, width), lambda b, i: (b * nc + i, k_blk)),
                  pl.BlockSpec((c, width), lambda b, i: (b * nc + i, v_blk)),
                  pl.BlockSpec((c, LANES), lambda b, i: (b * nc + i, lo_blk))]
                 + [full(params[n]) for n in names],
        out_specs=pl.BlockSpec((c, width), lambda b, i: (b * nc + i, 0)),
        out_shape=jax.ShapeDtypeStruct((batch * seq, width), F32),
        scratch_shapes=[pltpu.VMEM((width // LANES, LANES, LANES), F32),
                        pltpu.VMEM((8, width), F32), pltpu.VMEM((8, width), F32),
                        pltpu.VMEM((8, width), F32), pltpu.VMEM((8, LANES), F32)],
        compiler_params=_cparams("parallel", "arbitrary"),
        name="rwkv7",
    )(p32, p32, p32, p32, *[params[n] for n in names])


def _merge_kernel(oc_ref, os_ref, ow_ref, sg_ref, rk_ref, x_ref, p_ref, g_ref, wl_ref,
                  wb0_ref, wb1_ref, wb2_ref, wo_ref, pleg_ref, wpg_ref, wpp_ref, fin_ref,
                  o_ref, *, last):
    d = x_ref.shape[1]
    half = sg_ref.shape[1]
    nzw = oc_ref.shape[1]
    h = _rmsnorm(x_ref[...], g_ref[...]).astype(BF16)
    late = lambda c0, n: _dot(h, wl_ref[:, c0:c0 + n])
    y_nsa = oc_ref[...] + os_ref[...] + ow_ref[...]
    z0 = _dot((y_nsa * _silu(late(3 * d, nzw))).astype(BF16), wb0_ref[...])
    merged = _sigmoid(late(0, d)) * z0
    z1 = _dot((sg_ref[...] * _silu(late(3 * d + nzw, half))).astype(BF16), wb1_ref[...])
    merged = merged + _sigmoid(late(d, d)) * z1
    z2 = _dot((rk_ref[...] * _silu(late(3 * d + nzw + half, half))).astype(BF16), wb2_ref[...])
    merged = merged + _sigmoid(late(2 * d, d)) * z2
    x1 = x_ref[...] + _dot(merged.astype(BF16), wo_ref[...])
    hp = _rmsnorm(x1, pleg_ref[...]).astype(BF16)
    x2 = x1 + _sigmoid(_dot(hp, wpg_ref[...])) * _dot(p_ref[...].astype(BF16), wpp_ref[...])
    o_ref[...] = _rmsnorm(x2, fin_ref[...]) if last else x2


def _merge(o_c, o_s, o_w, y_sg, y_rk, x2, p2, consts, last, tm=256):
    t, d = x2.shape
    half = y_sg.shape[1]
    tok = lambda w: pl.BlockSpec((tm, w), lambda i: (i, 0))
    full = lambda a: pl.BlockSpec(a.shape, lambda i: (0,) * a.ndim, pipeline_mode=pl.Buffered(1))
    names = ("g", "wl", "wb0", "wb1", "wb2", "wo", "pleg", "wpg", "wpp", "fin")
    return pl.pallas_call(
        functools.partial(_merge_kernel, last=last),
        grid=(t // tm,),
        in_specs=[tok(o_c.shape[1]), tok(o_s.shape[1]), tok(o_w.shape[1]),
                  tok(half), tok(half), tok(d), tok(p2.shape[1])]
                 + [full(consts[n]) for n in names],
        out_specs=tok(d),
        out_shape=jax.ShapeDtypeStruct((t, d), F32),
        compiler_params=_cparams("parallel"),
        name="merge_out_ple",
    )(o_c, o_s, o_w, y_sg, y_rk, x2, p2, *[consts[n] for n in names])


def _pad_cols(w, n):
    return jnp.pad(w, ((0, 0), (0, n - w.shape[1])))


def _layer_layout(d_model):
    half = d_model // 2
    kvw = NSA_GROUPS * HEAD_DIM
    sizes = dict(nq=half, nkv=6 * kvw, ngate=3 * NSA_HEADS, nz=half, su=half, sv=half, sz=half,
                 rs=3 * half + 2 * LORA, rz=half, mg=3 * d_model)
    off, out = 0, {}
    for name, n in sizes.items():
        out[name] = (off, n)
        off += n
    return out


def _prep_proj_weights(w_in, d_model):
    lay = _layer_layout(d_model)
    half = d_model // 2
    kvw = NSA_GROUPS * HEAD_DIM
    col = lambda name, a=0, n=None: w_in[:, lay[name][0] + a: lay[name][0] + a + (n or lay[name][1] - a)]
    heads = lambda w: jnp.concatenate(
        [_pad_cols(w[:, h * HEAD_DIM:(h + 1) * HEAD_DIM], LANES) for h in range(w.shape[1] // HEAD_DIM)], axis=1)
    w32 = jnp.concatenate([
        col("su"), col("sv"), col("rs", 0, half), col("rs", half, half), col("rs", 2 * half, half),
        col("nkv", 0, 2 * kvw), col("rs", 3 * half, 2 * LORA), _pad_cols(col("ngate"), LANES)], axis=1)
    blocks, o = {}, 0
    for name in ("su", "sv", "rr", "rk", "rv"):
        blocks[name] = o // half
        o += half
    blocks["kcvc"] = o
    o += 2 * kvw
    blocks["lora"] = o // LANES
    o += LANES
    blocks["gate"] = o // LANES
    w_late = jnp.concatenate([col("mg"), col("nz"), col("sz"), col("rz")], axis=1)
    zero = jnp.zeros((w_in.shape[0], LANES), w_in.dtype)
    grp = lambda w, g: _pad_cols(w[:, g * HEAD_DIM:(g + 1) * HEAD_DIM], LANES)
    ks, vs = col("nkv", 2 * kvw, kvw), col("nkv", 3 * kvw, kvw)
    kw, vw = col("nkv", 4 * kvw, kvw), col("nkv", 5 * kvw, kvw)
    w16 = jnp.concatenate(
        [heads(col("nq"))]
        + [x for g in range(NSA_GROUPS) for x in (grp(ks, g), zero)]
        + [grp(vs, g) for g in range(NSA_GROUPS)]
        + [grp(kw, g) for g in range(NSA_GROUPS)]
        + [grp(vw, g) for g in range(NSA_GROUPS)], axis=1)
    return w32.astype(BF16), w16.astype(BF16), w_late.astype(BF16), blocks


def _prep_compress_weights(cmp_w1, cmp_w2, cmp_pe):
    half_blk = CMP_BLOCK // 2
    w1r = cmp_w1.reshape(2, 2, half_blk, HEAD_DIM, CMP_HIDDEN)
    eye = jnp.eye(NSA_GROUPS, dtype=cmp_w1.dtype)
    eye_kv = jnp.eye(2, dtype=cmp_w1.dtype)
    big = jnp.einsum('kaldh,kK,gG->alkgdKGh', w1r, eye_kv, eye)
    big = big.reshape(2, half_blk * 2 * NSA_GROUPS * HEAD_DIM, 2 * NSA_GROUPS * CMP_HIDDEN)
    pe = cmp_pe.reshape(2, 2, half_blk, HEAD_DIM)
    pe2 = jnp.broadcast_to(pe.transpose(1, 2, 0, 3)[:, :, :, None, :],
                           (2, half_blk, 2, NSA_GROUPS, HEAD_DIM)).reshape(2, -1)
    w2p = jnp.pad(cmp_w2, ((0, 0), (0, 0), (0, LANES - HEAD_DIM)))
    return pe2, big[0].astype(BF16), big[1].astype(BF16), w2p.astype(BF16)


def _cover_t(seq):
    n_grp = seq // CMP_STRIDE
    n_slc = seq // SEL_BLOCK
    cmp_start = jnp.arange(n_grp) * CMP_STRIDE
    slc_start = jnp.arange(n_slc) * SEL_BLOCK
    cover = ((cmp_start[None, :] <= slc_start[:, None] + SEL_BLOCK - 1)
             & (cmp_start[None, :] + CMP_BLOCK - 1 >= slc_start[:, None]))
    return jnp.pad(cover, ((0, LANES - n_slc), (0, 0))).astype(BF16)


def kernel(x, p, norm_g, w_in, cmp_w1, cmp_w2, cmp_pe, sg_ln_g, sg_ln_b, sg_w, sg_b, rk_mu, rk_w0,
           rk_w2, rk_a0, rk_a2, rk_kk, rk_ka, rk_rk, rk_lnx_g, rk_lnx_b, w_branch, w_o, ple_norm_g,
           w_ple_gate, w_ple_proj, final_norm_g):
    batch, seq, d = x.shape
    depth = w_in.shape[0]
    half = d // 2
    t = batch * seq
    assert seq % SEL_KEY_TILE == 0 or seq < SEL_KEY_TILE
    assert seq >= WINDOW + Q_TILE and seq // SEL_BLOCK <= LANES
    x2 = x.reshape(t, d)
    cov_t = _cover_t(seq)
    seg = (jnp.arange(half)[:, None] // HEAD_DIM == jnp.arange(half)[None, :] // HEAD_DIM).astype(BF16)
    row = lambda v: v.reshape(1, -1)
    for i in range(depth):
        w32, w16, w_late, blk = _prep_proj_weights(w_in[i], d)
        g = row(norm_g[i])
        p32, p16 = _proj(x2, g, w32, w16, seq)

        pe2, wa, wb, w2p = _prep_compress_weights(cmp_w1[i], cmp_w2[i], cmp_pe[i])
        kcvc = p32[:, blk["kcvc"]:blk["kcvc"] + 2 * NSA_GROUPS * HEAD_DIM]
        xg = kcvc.reshape(batch, seq // CMP_STRIDE, CMP_STRIDE * 2 * NSA_GROUPS * HEAD_DIM)
        kc, vc = _compress(xg, pe2, wa, wb, w2p)
        o_c, bias = _cmp_select(p16, p32, blk["gate"], kc, vc, cov_t, batch, seq)
        o_s = _sel_attn(p16, p32, blk["gate"], bias, batch, seq)
        o_w = _win_attn(p16, p32, blk["gate"], batch, seq)

        bexp = jnp.repeat(sg_b[i].T, half // SG_GROUPS, axis=1)
        y_sg = _spatial_gating(p32, row(sg_ln_g[i]), row(sg_ln_b[i]), sg_w[i], bexp, blk["su"], blk["sv"])

        mu = rk_mu[i]
        rk_params = dict(
            mu_r=row(mu[0:half]), mu_k=row(mu[half:2 * half]), mu_v=row(mu[2 * half:3 * half]),
            mu_lo=row(mu[3 * half:]), w0=row(rk_w0[i]),
            w2=jnp.pad(rk_w2[i], ((0, LANES - LORA), (0, 0))).astype(BF16), a0=row(rk_a0[i]),
            a2=jnp.pad(rk_a2[i], ((LANES - LORA, 0), (0, 0))).astype(BF16),
            kk=row(rk_kk[i]), ka=row(rk_ka[i]), rk=row(rk_rk[i]), lng=row(rk_lnx_g[i]),
            lnb=row(rk_lnx_b[i]), seg=seg)
        y_rk = _rwkv(p32, rk_params, batch, seq, blk["rr"], blk["rk"], blk["rv"], blk["lora"])

        consts = dict(g=g, wl=w_late, wb0=w_branch[i, 0].astype(BF16), wb1=w_branch[i, 1].astype(BF16),
                      wb2=w_branch[i, 2].astype(BF16), wo=w_o[i].astype(BF16),
                      pleg=row(ple_norm_g[i]), wpg=w_ple_gate[i].astype(BF16),
                      wpp=w_ple_proj[i].astype(BF16), fin=row(final_norm_g))
        x2 = _merge(o_c, o_s, o_w, y_sg, y_rk, x2, p[i].reshape(t, -1), consts,
                    last=(i == depth - 1))
    return x2.reshape(batch, seq, d)
```

```python
import functools

import jax
import jax.numpy as jnp
from jax import lax
from jax.experimental import pallas as pl
from jax.experimental.pallas import tpu as pltpu

F32 = jnp.float32
BF16 = jnp.bfloat16

HEAD_DIM = 64
LANES = 128
NSA_HEADS = 8
NSA_GROUPS = 2
NSA_HPG = NSA_HEADS // NSA_GROUPS
CMP_BLOCK = 32
CMP_STRIDE = 16
CMP_HIDDEN = 128
CMP_WIDTH_STEP = 128
SEL_BLOCK = 64
SEL_TOPK = 16
WINDOW = 512
Q_TILE = 128
SEL_KEY_TILE = 512
SEL_TAIL_STEP = 256
SG_GROUPS = 8
SG_CHUNK = 128
RWKV_HEADS = 8
RWKV_CHUNK = 64
LORA = 64
NORM_EPS = 1e-6
LN_EPS = 1e-5
GN_EPS = 64e-5
MASK_NEG = -1e30
FORCE_BONUS = 1e4
VMEM_LIMIT = 56 * 1024 * 1024


def _cparams(*sem):
    return pltpu.CompilerParams(dimension_semantics=sem, vmem_limit_bytes=VMEM_LIMIT)


def _dot(a, b, precision=None):
    return jnp.dot(a, b, preferred_element_type=F32, precision=precision)


def _dot_nt(a, b, precision=None):
    return lax.dot_general(a, b, (((1,), (1,)), ((), ())), preferred_element_type=F32,
                           precision=precision)


def _dot_tn(a, b, precision=None):
    return lax.dot_general(a, b, (((0,), (0,)), ((), ())), preferred_element_type=F32,
                           precision=precision)


def _split3(x):
    hi = x.astype(BF16)
    r1 = x - hi.astype(F32)
    mid = r1.astype(BF16)
    lo = (r1 - mid.astype(F32)).astype(BF16)
    return hi, mid, lo


def _dot3_lhs(x, w):
    hi, mid, lo = _split3(x)
    return _dot(hi, w) + _dot(mid, w) + _dot(lo, w)


def _dot3_rhs(w, x):
    hi, mid, lo = _split3(x)
    return _dot(w, hi) + _dot(w, mid) + _dot(w, lo)


def _sigmoid(x):
    return 1.0 / (1.0 + jnp.exp(-x))


def _silu(x):
    return x * _sigmoid(x)


def _rmsnorm(x, g):
    return x * lax.rsqrt(jnp.mean(x * x, axis=-1, keepdims=True) + NORM_EPS) * g


P16_Q = 0
P16_KX = NSA_HEADS * LANES
P16_VX = P16_KX + NSA_GROUPS * 2 * LANES
P16_KW = P16_VX + NSA_GROUPS * LANES
P16_VW = P16_KW + NSA_GROUPS * LANES
P16_N = P16_VW + NSA_GROUPS * LANES


PROJ_CHUNK = 2 * LANES


def _proj_kernel(x_ref, g_ref, w32_ref, wkc_ref, w16_ref, o32_ref, okc_ref, o16_ref, *, seq, tm):
    h = _rmsnorm(x_ref[...], g_ref[...]).astype(BF16)
    for c0 in range(0, o32_ref.shape[1], PROJ_CHUNK):
        o32_ref[:, c0:c0 + PROJ_CHUNK] = _dot(h, w32_ref[:, c0:c0 + PROJ_CHUNK])
    okc_ref[...] = _dot(h, wkc_ref[...])
    pos = (pl.program_id(0) * tm) % seq + lax.broadcasted_iota(jnp.int32, (tm, LANES), 0)
    lane = lax.broadcasted_iota(jnp.int32, (tm, LANES), 1)
    onehot_blk = jnp.where(pos // SEL_BLOCK == lane, 1.0, 0.0)
    ones_col = jnp.where(lane == HEAD_DIM, 1.0, 0.0)
    zeros = jnp.zeros((tm, LANES), F32)

    def constant(c0):
        if P16_KX <= c0 < P16_VX and ((c0 - P16_KX) // LANES) % 2 == 1:
            return onehot_blk
        if c0 >= P16_VX and not (P16_KW <= c0 < P16_VW):
            return ones_col
        return zeros

    for c0 in range(0, P16_N, PROJ_CHUNK):
        y = _dot(h, w16_ref[:, c0:c0 + PROJ_CHUNK])
        consts = [constant(c0 + i * LANES) for i in range(PROJ_CHUNK // LANES)]
        if any(c is not zeros for c in consts):
            y = y + jnp.concatenate(consts, axis=1)
        o16_ref[:, c0:c0 + PROJ_CHUNK] = y.astype(BF16)


def _proj(x2, g, w32, wkc, w16, seq, tm=512):
    t, d = x2.shape
    n32, nkc = w32.shape[1], wkc.shape[1]
    const = lambda a: pl.BlockSpec(a.shape, lambda i: (0, 0), pipeline_mode=pl.Buffered(1))
    return pl.pallas_call(
        functools.partial(_proj_kernel, seq=seq, tm=tm),
        grid=(t // tm,),
        in_specs=[pl.BlockSpec((tm, d), lambda i: (i, 0)), const(g), const(w32), const(wkc),
                  const(w16)],
        out_specs=[pl.BlockSpec((tm, n32), lambda i: (i, 0)),
                   pl.BlockSpec((tm, nkc), lambda i: (i, 0)),
                   pl.BlockSpec((tm, P16_N), lambda i: (i, 0))],
        out_shape=[jax.ShapeDtypeStruct((t, n32), F32), jax.ShapeDtypeStruct((t, nkc), F32),
                   jax.ShapeDtypeStruct((t, P16_N), BF16)],
        compiler_params=_cparams("parallel"),
        name="proj",
    )(x2, g, w32, wkc, w16)


def _compress_kernel(x_ref, pe_ref, wa_ref, wb_ref, w2_ref, k_ref, v_ref):
    x = x_ref[0]
    n_grp = x.shape[0]
    a = _dot((x + pe_ref[0:1, :]).astype(BF16), wa_ref[...])
    b = _dot((x + pe_ref[1:2, :]).astype(BF16), wb_ref[...])
    hid = _silu(a + pltpu.roll(b, n_grp - 1, 0)).astype(BF16)
    row = lax.broadcasted_iota(jnp.int32, (n_grp, LANES), 0)
    lane = lax.broadcasted_iota(jnp.int32, (n_grp, LANES), 1)
    live = row < n_grp - 1
    for kv, out_ref in enumerate((k_ref, v_ref)):
        for g in range(NSA_GROUPS):
            c0 = (kv * NSA_GROUPS + g) * CMP_HIDDEN
            y = _dot(hid[:, c0:c0 + CMP_HIDDEN], w2_ref[kv])
            if kv == 1:
                y = y + jnp.where(lane == HEAD_DIM, 1.0, 0.0)
            out_ref[0, :, g * LANES:(g + 1) * LANES] = jnp.where(live, y, 0.0).astype(BF16)


def _compress(xg, pe2, wa, wb, w2p):
    b, n_grp, width = xg.shape
    out = jax.ShapeDtypeStruct((b, n_grp, NSA_GROUPS * LANES), BF16)
    return pl.pallas_call(
        _compress_kernel,
        grid=(b,),
        in_specs=[pl.BlockSpec((1, n_grp, width), lambda i: (i, 0, 0)),
                  pl.BlockSpec(pe2.shape, lambda i: (0, 0)),
                  pl.BlockSpec(wa.shape, lambda i: (0, 0)),
                  pl.BlockSpec(wb.shape, lambda i: (0, 0)),
                  pl.BlockSpec(w2p.shape, lambda i: (0, 0, 0))],
        out_specs=[pl.BlockSpec((1, n_grp, NSA_GROUPS * LANES), lambda i: (i, 0, 0))] * 2,
        out_shape=[out, out],
        compiler_params=_cparams("parallel"),
        name="nsa_compress",
    )(xg, pe2, wa, wb, w2p)


def _stack_heads(q_ref, g):
    parts = [q_ref[:, (g * NSA_HPG + h) * LANES:(g * NSA_HPG + h + 1) * LANES]
             for h in range(NSA_HPG)]
    return jnp.concatenate(parts, axis=0) * jnp.asarray(HEAD_DIM ** -0.5, BF16)


def _store_heads(o_ref, gate_ref, branch, g, o):
    lane = lax.broadcasted_iota(jnp.int32, (Q_TILE, LANES), 1)
    gate = _sigmoid(gate_ref[...])

    def scaled(h):
        col = (g * NSA_HPG + h) * 3 + branch
        return jnp.broadcast_to(gate[:, col:col + 1], (Q_TILE, LANES)) * o[h * Q_TILE:(h + 1) * Q_TILE]

    for j in range(NSA_HPG // 2):
        tile = jnp.where(lane < HEAD_DIM, scaled(2 * j), pltpu.roll(scaled(2 * j + 1), HEAD_DIM, 1))
        c0 = (g * NSA_HPG // 2 + j) * LANES
        o_ref[:, c0:c0 + LANES] = tile


def _cmp_select_kernel(q_ref, gate_ref, kc_ref, vc_ref, cov_ref, o_ref, bias_ref, imp_ref, *, top_k):
    n_cmp = kc_ref.shape[1]
    n_slc = cov_ref.shape[0]
    q0 = pl.program_id(1) * Q_TILE
    rows = NSA_HPG * Q_TILE
    sel_shape = (n_slc, NSA_GROUPS * Q_TILE)
    j_idx = lax.broadcasted_iota(jnp.int32, sel_shape, 0)
    t_lane = q0 + lax.broadcasted_iota(jnp.int32, sel_shape, 1) % Q_TILE
    t_blk = t_lane // SEL_BLOCK
    forced = (j_idx == 0) | (j_idx == t_blk) | (j_idx == t_blk - 1)
    causal_blk = j_idx <= t_blk

    def attend(width):
        t_row = q0 + lax.broadcasted_iota(jnp.int32, (Q_TILE, width), 0)
        cmp_end = lax.broadcasted_iota(jnp.int32, (Q_TILE, width), 1) * CMP_STRIDE + CMP_BLOCK - 1
        mask = jnp.tile(jnp.where(cmp_end <= t_row, 0.0, MASK_NEG), (NSA_HPG, 1))
        any_valid = jnp.tile(jnp.where(t_row[:, 0:1] >= CMP_BLOCK - 1, 1.0, 0.0), (NSA_HPG, 1))
        s_all = [_dot_nt(_stack_heads(q_ref, g), kc_ref[0, 0:width, g * LANES:(g + 1) * LANES])
                 for g in range(NSA_GROUPS)]
        for g in range(NSA_GROUPS):
            s = s_all[g] + mask
            p = jnp.exp(s - jnp.max(s, axis=-1, keepdims=True))
            l = jnp.sum(p, axis=-1, keepdims=True)
            p = p * (any_valid / l)
            o = _dot(p.astype(BF16), vc_ref[0, 0:width, g * LANES:(g + 1) * LANES])
            _store_heads(o_ref, gate_ref, 0, g, o)
            p_sum = p[0:Q_TILE]
            for h in range(1, NSA_HPG):
                p_sum = p_sum + p[h * Q_TILE:(h + 1) * Q_TILE]
            hi, mid, lo = _split3(p_sum)
            cov = cov_ref[:, 0:width]
            imp_ref[:, g * Q_TILE:(g + 1) * Q_TILE] = (_dot_nt(cov, hi) + _dot_nt(cov, mid)
                                                       + _dot_nt(cov, lo))

    n_var = n_cmp // CMP_WIDTH_STEP
    for var in range(n_var):
        @pl.when(q0 // (CMP_WIDTH_STEP * CMP_STRIDE) == var)
        def _():
            attend((var + 1) * CMP_WIDTH_STEP)

    score = jnp.where(causal_blk, jnp.where(forced, FORCE_BONUS, imp_ref[...]), -FORCE_BONUS)

    def pick(_, carry):
        sc, sel = carry
        best = jnp.max(sc, axis=0, keepdims=True)
        first = jnp.min(jnp.where(sc == best, j_idx, n_slc), axis=0, keepdims=True)
        hit = j_idx == first
        return jnp.where(hit, -jnp.inf, sc), jnp.where(hit, 1.0, sel)

    _, sel = lax.fori_loop(0, top_k, pick, (score, jnp.zeros(sel_shape, F32)))
    bias_t = jnp.where((sel > 0.0) & causal_blk, 0.0, MASK_NEG)
    for g in range(NSA_GROUPS):
        bias_ref[:, g * n_slc:(g + 1) * n_slc] = bias_t[:, g * Q_TILE:(g + 1) * Q_TILE].T.astype(BF16)


def _cmp_select(p16, p32, gate_blk, kc, vc, cov_t, batch, seq):
    n_cmp = kc.shape[1]
    n_slc = cov_t.shape[0]
    nq = seq // Q_TILE
    return pl.pallas_call(
        functools.partial(_cmp_select_kernel, top_k=min(SEL_TOPK, seq // SEL_BLOCK)),
        grid=(batch, nq),
        in_specs=[pl.BlockSpec((Q_TILE, NSA_HEADS * LANES), lambda b, i: (b * nq + i, 0)),
                  pl.BlockSpec((Q_TILE, LANES), lambda b, i: (b * nq + i, gate_blk)),
                  pl.BlockSpec((1, n_cmp, NSA_GROUPS * LANES), lambda b, i: (b, 0, 0)),
                  pl.BlockSpec((1, n_cmp, NSA_GROUPS * LANES), lambda b, i: (b, 0, 0)),
                  pl.BlockSpec(cov_t.shape, lambda b, i: (0, 0))],
        out_specs=[pl.BlockSpec((Q_TILE, NSA_HEADS * HEAD_DIM), lambda b, i: (b * nq + i, 0)),
                   pl.BlockSpec((Q_TILE, NSA_GROUPS * n_slc), lambda b, i: (b * nq + i, 0))],
        out_shape=[jax.ShapeDtypeStruct((batch * seq, NSA_HEADS * HEAD_DIM), F32),
                   jax.ShapeDtypeStruct((batch * seq, NSA_GROUPS * n_slc), BF16)],
        scratch_shapes=[pltpu.VMEM((n_slc, NSA_GROUPS * Q_TILE), F32)],
        compiler_params=_cparams("parallel", "arbitrary"),
        name="nsa_cmp_select",
    )(p16, p32, kc, vc, cov_t)


def _sel_attn_kernel(q_ref, gate_ref, bias_ref, kx_ref, vx_ref, o_ref, m_scr, acc_scr, *, tk):
    n_slc = bias_ref.shape[1] // NSA_GROUPS
    q0 = pl.program_id(1) * Q_TILE
    rows = NSA_HPG * Q_TILE
    wide = 2 * tk
    n_wide = q0 // wide
    qx = []
    for g in range(NSA_GROUPS):
        bias = bias_ref[:, g * n_slc:(g + 1) * n_slc]
        qx.append(jnp.concatenate([_stack_heads(q_ref, g),
                                   jnp.concatenate([bias] * NSA_HPG, axis=0)], axis=1))

    def load():
        return tuple((m_scr[g], acc_scr[g]) for g in range(NSA_GROUPS))

    def store(state):
        for g in range(NSA_GROUPS):
            m_scr[g], acc_scr[g] = state[g]

    def step(k0, width, diagonal, state):
        out = []
        for g in range(NSA_GROUPS):
            m_old, acc = state[g]
            s = _dot_nt(qx[g], kx_ref[pl.ds(k0, width), g * 2 * LANES:(g + 1) * 2 * LANES])
            if diagonal:
                t_row = q0 + lax.broadcasted_iota(jnp.int32, (Q_TILE, width), 0)
                key = k0 + lax.broadcasted_iota(jnp.int32, (Q_TILE, width), 1)
                s = s + jnp.tile(jnp.where(key <= t_row, 0.0, MASK_NEG), (NSA_HPG, 1))
            m_new = jnp.maximum(m_old, jnp.max(s, axis=-1, keepdims=True))
            p = jnp.exp(s - jnp.tile(m_new, (1, width // LANES))).astype(BF16)
            pv = _dot(p, vx_ref[pl.ds(k0, width), g * LANES:(g + 1) * LANES])
            out.append((m_new, jnp.exp(m_old - m_new) * acc + pv))
        return tuple(out)

    store(tuple((jnp.full((rows, LANES), MASK_NEG, F32), jnp.zeros((rows, LANES), F32))
                for _ in range(NSA_GROUPS)))

    def pair(j, _):
        k0 = pl.multiple_of(j * 2 * wide, 2 * wide)
        store(step(k0 + wide, wide, False, step(k0, wide, False, load())))
        return 0

    lax.fori_loop(0, n_wide // 2, pair, 0)

    @pl.when(n_wide % 2 == 1)
    def _():
        store(step(pl.multiple_of((n_wide - 1) * wide, wide), wide, False, load()))

    tail0 = pl.multiple_of(n_wide * wide, wide)
    tail_steps = (q0 + Q_TILE - tail0 + SEL_TAIL_STEP - 1) // SEL_TAIL_STEP
    for var in range(1, wide // SEL_TAIL_STEP + 1):
        @pl.when(tail_steps == var)
        def _():
            store(step(tail0, var * SEL_TAIL_STEP, True, load()))

    for g in range(NSA_GROUPS):
        acc = acc_scr[g]
        _store_heads(o_ref, gate_ref, 1, g, acc / acc[:, HEAD_DIM:HEAD_DIM + 1])


def _sel_attn(p16, p32, gate_blk, bias, batch, seq):
    nq = seq // Q_TILE
    tk = min(SEL_KEY_TILE, seq)
    kx_w = NSA_GROUPS * 2 * LANES
    vx_w = NSA_GROUPS * LANES
    return pl.pallas_call(
        functools.partial(_sel_attn_kernel, tk=tk),
        grid=(batch, nq),
        in_specs=[pl.BlockSpec((Q_TILE, NSA_HEADS * LANES), lambda b, i: (b * nq + i, 0)),
                  pl.BlockSpec((Q_TILE, LANES), lambda b, i: (b * nq + i, gate_blk)),
                  pl.BlockSpec((Q_TILE, bias.shape[1]), lambda b, i: (b * nq + i, 0)),
                  pl.BlockSpec((seq, kx_w), lambda b, i: (b, P16_KX // kx_w)),
                  pl.BlockSpec((seq, vx_w), lambda b, i: (b, P16_VX // vx_w))],
        out_specs=pl.BlockSpec((Q_TILE, NSA_HEADS * HEAD_DIM), lambda b, i: (b * nq + i, 0)),
        out_shape=jax.ShapeDtypeStruct((batch * seq, NSA_HEADS * HEAD_DIM), F32),
        scratch_shapes=[pltpu.VMEM((NSA_GROUPS, NSA_HPG * Q_TILE, LANES), F32),
                        pltpu.VMEM((NSA_GROUPS, NSA_HPG * Q_TILE, LANES), F32)],
        compiler_params=_cparams("parallel", "arbitrary"),
        name="nsa_sel_attn",
    )(p16, p32, bias, p16, p16)


def _win_attn_kernel(q_ref, gate_ref, kw_ref, vw_ref, o_ref, *, span):
    seq = kw_ref.shape[0]
    q0 = pl.program_id(1) * Q_TILE
    rows = NSA_HPG * Q_TILE
    start = pl.multiple_of(jnp.clip(q0 + Q_TILE - span, 0, seq - span), Q_TILE)
    t_row = q0 + lax.broadcasted_iota(jnp.int32, (Q_TILE, span), 0)
    pos = start + lax.broadcasted_iota(jnp.int32, (Q_TILE, span), 1)
    mask = jnp.tile(jnp.where((pos <= t_row) & (pos > t_row - WINDOW), 0.0, MASK_NEG), (NSA_HPG, 1))
    s_all = [_dot_nt(_stack_heads(q_ref, g), kw_ref[pl.ds(start, span), g * LANES:(g + 1) * LANES])
             for g in range(NSA_GROUPS)]
    p_all = []
    for g in range(NSA_GROUPS):
        s = s_all[g] + mask
        p_all.append(jnp.exp(s - jnp.max(s, axis=-1, keepdims=True)).astype(BF16))
    for g in range(NSA_GROUPS):
        acc = _dot(p_all[g], vw_ref[pl.ds(start, span), g * LANES:(g + 1) * LANES])
        _store_heads(o_ref, gate_ref, 2, g, acc / acc[:, HEAD_DIM:HEAD_DIM + 1])


def _win_attn(p16, p32, gate_blk, batch, seq):
    nq = seq // Q_TILE
    span = min(WINDOW + Q_TILE, seq)
    w = NSA_GROUPS * LANES
    return pl.pallas_call(
        functools.partial(_win_attn_kernel, span=span),
        grid=(batch, nq),
        in_specs=[pl.BlockSpec((Q_TILE, NSA_HEADS * LANES), lambda b, i: (b * nq + i, 0)),
                  pl.BlockSpec((Q_TILE, LANES), lambda b, i: (b * nq + i, gate_blk)),
                  pl.BlockSpec((seq, w), lambda b, i: (b, P16_KW // w)),
                  pl.BlockSpec((seq, w), lambda b, i: (b, P16_VW // w))],
        out_specs=pl.BlockSpec((Q_TILE, NSA_HEADS * HEAD_DIM), lambda b, i: (b * nq + i, 0)),
        out_shape=jax.ShapeDtypeStruct((batch * seq, NSA_HEADS * HEAD_DIM), F32),
        compiler_params=_cparams("parallel", "arbitrary"),
        name="nsa_win_attn",
    )(p16, p32, p16, p16)


def _spatial_gating_kernel(u_ref, v_ref, g_ref, b_ref, w_ref, bexp_ref, o_ref, *, chunks):
    c = SG_CHUNK
    tri = (lax.broadcasted_iota(jnp.int32, (c, c), 1) <= lax.broadcasted_iota(jnp.int32, (c, c), 0))
    lane = lax.broadcasted_iota(jnp.int32, (c, LANES), 1)
    wm = [jnp.where(tri, w_ref[g], 0.0).astype(BF16) for g in range(SG_GROUPS)]
    for ci in range(chunks):
        v = v_ref[ci * c:(ci + 1) * c, :]
        mu = jnp.mean(v, axis=-1, keepdims=True)
        d = v - mu
        var = jnp.mean(d * d, axis=-1, keepdims=True)
        vn = (d * lax.rsqrt(var + LN_EPS) * g_ref[...] + b_ref[...]).astype(BF16)
        for pair in range(SG_GROUPS // 2):
            vp = vn[:, pair * LANES:(pair + 1) * LANES]
            mixed = jnp.where(lane < HEAD_DIM, _dot(wm[2 * pair], vp), _dot(wm[2 * pair + 1], vp))
            sl = (slice(ci * c, (ci + 1) * c), slice(pair * LANES, (pair + 1) * LANES))
            o_ref[sl] = u_ref[sl] * (mixed + bexp_ref[:, pair * LANES:(pair + 1) * LANES])


def _spatial_gating(p32, ln_g, ln_b, w_s, bexp, u_blk, v_blk, chunks=4):
    t = p32.shape[0]
    width = ln_g.shape[1]
    tm = chunks * SG_CHUNK
    return pl.pallas_call(
        functools.partial(_spatial_gating_kernel, chunks=chunks),
        grid=(t // tm,),
        in_specs=[pl.BlockSpec((tm, width), lambda i: (i, u_blk)),
                  pl.BlockSpec((tm, width), lambda i: (i, v_blk)),
                  pl.BlockSpec((1, width), lambda i: (0, 0)),
                  pl.BlockSpec((1, width), lambda i: (0, 0)),
                  pl.BlockSpec(w_s.shape, lambda i: (0, 0, 0)),
                  pl.BlockSpec(bexp.shape, lambda i: (0, 0))],
        out_specs=pl.BlockSpec((tm, width), lambda i: (i, 0)),
        out_shape=jax.ShapeDtypeStruct((t, width), F32),
        compiler_params=_cparams("parallel"),
        name="spatial_gating",
    )(p32, p32, ln_g, ln_b, w_s, bexp)


def _rwkv_kernel(r_ref, k_ref, v_ref, lo_ref, mu_r, mu_k, mu_v, mu_lo, w0_ref, w2_ref, a0_ref,
                 a2_ref, kk_ref, ka_ref, rk_ref, lng_ref, lnb_ref, seg_ref, o_ref,
                 state, prev_r, prev_k, prev_v, prev_lo, *, nch):
    c = RWKV_CHUNK
    rows = nch * c
    width = r_ref.shape[1]
    n_pairs = width // LANES

    @pl.when(pl.program_id(1) == 0)
    def _():
        state[...] = jnp.zeros_like(state)
        for ref in (prev_r, prev_k, prev_v, prev_lo):
            ref[...] = jnp.zeros_like(ref)

    def token_shift(x_ref, prev_ref, mu_ref):
        x = x_ref[...]
        first = lax.broadcasted_iota(jnp.int32, x.shape, 0) == 0
        prev = jnp.where(first, prev_ref[0:1, :], pltpu.roll(x, 1, 0))
        prev_ref[0:1, :] = x[rows - 1:rows, :]
        return x + (prev - x) * mu_ref[...]

    r = token_shift(r_ref, prev_r, mu_r)
    k = token_shift(k_ref, prev_k, mu_k)
    v = token_shift(v_ref, prev_v, mu_v)
    lo = token_shift(lo_ref, prev_lo, mu_lo)

    lane_lo = lax.broadcasted_iota(jnp.int32, lo.shape, 1)
    w_in = jnp.where(lane_lo < LORA, jnp.tanh(lo), 0.0).astype(BF16)
    a_in = jnp.where(lane_lo < LORA, 0.0, lo).astype(BF16)
    z = -(w0_ref[...] + _dot(w_in, w2_ref[...]))
    w = -(jnp.maximum(z, 0.0) + jnp.log(1.0 + jnp.exp(-jnp.abs(z)))) - 0.5
    a = _sigmoid(a0_ref[...] + _dot(a_in, a2_ref[...]))

    seg = seg_ref[...]
    kkf = k * kk_ref[...]
    kk = kkf / jnp.maximum(jnp.sqrt(_dot3_lhs(kkf * kkf, seg)), 1e-12)
    k = k * (1.0 + (a - 1.0) * ka_ref[...])
    logd = -jnp.exp(w)
    t_row = lax.broadcasted_iota(jnp.int32, (rows, rows), 0)
    t_col = lax.broadcasted_iota(jnp.int32, (rows, rows), 1)
    cum = (t_col <= t_row) & (t_col // c == t_row // c)
    cl = _dot3_rhs(jnp.where(cum, 1.0, 0.0).astype(BF16), logd)
    cl_end = cl.reshape(nch, c, width)[:, c - 1:c, :]
    e_neg = jnp.exp(-cl)
    e_end = jnp.exp(jnp.broadcast_to(cl_end, (nch, c, width)).reshape(rows, width) - cl)
    b = kk * a
    a_t = -kk * jnp.exp(cl - logd)
    b_t = b * e_neg
    k_t = k * e_neg
    r_t = r * jnp.exp(cl)
    b_e = b * e_end
    k_e = k * e_end
    p_end = jnp.exp(cl_end)

    lane = lax.broadcasted_iota(jnp.int32, (c, LANES), 1)
    row2 = lax.broadcasted_iota(jnp.int32, (2 * c, 2 * c), 0)
    col2 = lax.broadcasted_iota(jnp.int32, (2 * c, 2 * c), 1)
    same_head = (row2 // c) == (col2 // c)
    strict = same_head & (col2 < row2)
    incl = same_head & (col2 <= row2)
    eye = jnp.where(row2 == col2, 1.0, 0.0)
    zero16 = jnp.zeros((c, LANES), BF16)

    def stack(x, ci, pair):
        xp = x[ci * c:(ci + 1) * c, pair * LANES:(pair + 1) * LANES].astype(BF16)
        return jnp.concatenate([jnp.where(lane < HEAD_DIM, xp, zero16),
                                jnp.where(lane < HEAD_DIM, zero16, xp)], axis=0)

    def off(m):
        return (((row2 % (2 * m)) >= m) & ((col2 // (2 * m)) == (row2 // (2 * m)))
                & ((col2 % (2 * m)) < m))

    combos = [(ci, pair) for ci in range(nch) for pair in range(n_pairs)]
    a_st = {q: stack(a_t, *q) for q in combos}
    r_st = {q: stack(r_t, *q) for q in combos}
    v_st = {q: stack(v, *q) for q in combos}
    g = {q: _dot_nt(jnp.concatenate([a_st[q], r_st[q]], axis=0),
                    jnp.concatenate([stack(b_t, *q), stack(k_t, *q)], axis=0)) for q in combos}
    l_ab = {q: jnp.where(strict, g[q][0:2 * c, 0:2 * c], 0.0) for q in combos}
    inv = {q: eye + jnp.where(off(1), l_ab[q], 0.0) for q in combos}
    m = 2
    while m < c:
        inv16 = {q: inv[q].astype(BF16) for q in combos}
        x = {q: _dot(jnp.where(off(m), l_ab[q], 0.0).astype(BF16), inv16[q]) for q in combos}
        inv = {q: inv[q] + _dot(inv16[q], x[q].astype(BF16)) for q in combos}
        m *= 2
    lakv = {q: _dot(jnp.where(strict, g[q][0:2 * c, 2 * c:4 * c], 0.0).astype(BF16), v_st[q])
            for q in combos}
    wu = {q: _dot(inv[q].astype(BF16), jnp.concatenate([a_st[q], lakv[q].astype(BF16)], axis=1))
          for q in combos}
    m_rbk = {q: jnp.concatenate([jnp.where(incl, g[q][2 * c:4 * c, 0:2 * c], 0.0),
                                 jnp.where(incl, g[q][2 * c:4 * c, 2 * c:4 * c], 0.0)],
                                axis=1).astype(BF16) for q in combos}
    bke_st = {q: jnp.concatenate([stack(b_e, *q), stack(k_e, *q)], axis=0) for q in combos}

    st = [state[pair] for pair in range(n_pairs)]
    ys = [[None] * n_pairs for _ in range(nch)]
    for ci in range(nch):
        wr = {pair: _dot(jnp.concatenate([wu[ci, pair][:, 0:LANES].astype(BF16), r_st[ci, pair]],
                                         axis=0), st[pair].astype(BF16)) for pair in range(n_pairs)}
        for pair in range(n_pairs):
            q = (ci, pair)
            u_st = wr[pair][0:2 * c] + wu[q][:, LANES:2 * LANES]
            uv_st = jnp.concatenate([u_st.astype(BF16), v_st[q]], axis=0)
            y_st = wr[pair][2 * c:4 * c] + _dot(m_rbk[q], uv_st)
            decay_col = jnp.broadcast_to(p_end[ci, :, pair * LANES:(pair + 1) * LANES],
                                         (LANES, LANES)).T
            st[pair] = st[pair] * decay_col + _dot_tn(bke_st[q], uv_st)
            ys[ci][pair] = y_st[0:c] + y_st[c:2 * c]
    for pair in range(n_pairs):
        state[pair] = st[pair]
    y = jnp.concatenate([jnp.concatenate(yr, axis=1) for yr in ys], axis=0)

    seg_mean = (seg * (1.0 / HEAD_DIM)).astype(BF16)
    mean = _dot3_lhs(y, seg_mean)
    d = y - mean
    var = _dot3_lhs(d * d, seg_mean)
    yn = d * lax.rsqrt(var + GN_EPS) * lng_ref[...] + lnb_ref[...]
    bonus = _dot3_lhs(r * k * rk_ref[...], seg) * v
    o_ref[...] = yn + bonus


def _rwkv(p32, params, batch, seq, r_blk, k_blk, v_blk, lo_blk, nch=4):
    width = params["w0"].shape[1]
    c = nch * RWKV_CHUNK
    nc = seq // c
    full = lambda a: pl.BlockSpec(a.shape, lambda b, i: (0,) * a.ndim)
    names = ("mu_r", "mu_k", "mu_v", "mu_lo", "w0", "w2", "a0", "a2", "kk", "ka", "rk",
             "lng", "lnb", "seg")
    return pl.pallas_call(
        functools.partial(_rwkv_kernel, nch=nch),
        grid=(batch, nc),
        in_specs=[pl.BlockSpec((c, width), lambda b, i: (b * nc + i, r_blk)),
                  pl.BlockSpec((c, width), lambda b, i: (b * nc + i, k_blk)),
                  pl.BlockSpec((c, width), lambda b, i: (b * nc + i, v_blk)),
                  pl.BlockSpec((c, LANES), lambda b, i: (b * nc + i, lo_blk))]
                 + [full(params[n]) for n in names],
        out_specs=pl.BlockSpec((c, width), lambda b, i: (b * nc + i, 0)),
        out_shape=jax.ShapeDtypeStruct((batch * seq, width), F32),
        scratch_shapes=[pltpu.VMEM((width // LANES, LANES, LANES), F32),
                        pltpu.VMEM((8, width), F32), pltpu.VMEM((8, width), F32),
                        pltpu.VMEM((8, width), F32), pltpu.VMEM((8, LANES), F32)],
        compiler_params=_cparams("parallel", "arbitrary"),
        name="rwkv7",
    )(p32, p32, p32, p32, *[params[n] for n in names])


def _merge_kernel(oc_ref, os_ref, ow_ref, sg_ref, rk_ref, x_ref, p_ref, g_ref, wl_ref,
                  wb0_ref, wb1_ref, wb2_ref, wo_ref, pleg_ref, wpg_ref, wpp_ref, fin_ref,
                  o_ref, *, last):
    d = x_ref.shape[1]
    half = sg_ref.shape[1]
    nzw = oc_ref.shape[1]
    h = _rmsnorm(x_ref[...], g_ref[...]).astype(BF16)
    late = lambda c0, n: _dot(h, wl_ref[:, c0:c0 + n])
    y_nsa = oc_ref[...] + os_ref[...] + ow_ref[...]
    z0 = _dot((y_nsa * _silu(late(3 * d, nzw))).astype(BF16), wb0_ref[...])
    merged = _sigmoid(late(0, d)) * z0
    z1 = _dot((sg_ref[...] * _silu(late(3 * d + nzw, half))).astype(BF16), wb1_ref[...])
    merged = merged + _sigmoid(late(d, d)) * z1
    z2 = _dot((rk_ref[...] * _silu(late(3 * d + nzw + half, half))).astype(BF16), wb2_ref[...])
    merged = merged + _sigmoid(late(2 * d, d)) * z2
    x1 = x_ref[...] + _dot(merged.astype(BF16), wo_ref[...])
    hp = _rmsnorm(x1, pleg_ref[...]).astype(BF16)
    x2 = x1 + _sigmoid(_dot(hp, wpg_ref[...])) * _dot(p_ref[...].astype(BF16), wpp_ref[...])
    o_ref[...] = _rmsnorm(x2, fin_ref[...]) if last else x2


def _merge(o_c, o_s, o_w, y_sg, y_rk, x2, p2, consts, last, tm=256):
    t, d = x2.shape
    half = y_sg.shape[1]
    tok = lambda w: pl.BlockSpec((tm, w), lambda i: (i, 0))
    full = lambda a: pl.BlockSpec(a.shape, lambda i: (0,) * a.ndim, pipeline_mode=pl.Buffered(1))
    names = ("g", "wl", "wb0", "wb1", "wb2", "wo", "pleg", "wpg", "wpp", "fin")
    return pl.pallas_call(
        functools.partial(_merge_kernel, last=last),
        grid=(t // tm,),
        in_specs=[tok(o_c.shape[1]), tok(o_s.shape[1]), tok(o_w.shape[1]),
                  tok(half), tok(half), tok(d), tok(p2.shape[1])]
                 + [full(consts[n]) for n in names],
        out_specs=tok(d),
        out_shape=jax.ShapeDtypeStruct((t, d), F32),
        compiler_params=_cparams("parallel"),
        name="merge_out_ple",
    )(o_c, o_s, o_w, y_sg, y_rk, x2, p2, *[consts[n] for n in names])


def _pad_cols(w, n):
    return jnp.pad(w, ((0, 0), (0, n - w.shape[1])))


def _layer_layout(d_model):
    half = d_model // 2
    kvw = NSA_GROUPS * HEAD_DIM
    sizes = dict(nq=half, nkv=6 * kvw, ngate=3 * NSA_HEADS, nz=half, su=half, sv=half, sz=half,
                 rs=3 * half + 2 * LORA, rz=half, mg=3 * d_model)
    off, out = 0, {}
    for name, n in sizes.items():
        out[name] = (off, n)
        off += n
    return out


def _prep_proj_weights(w_in, d_model):
    lay = _layer_layout(d_model)
    half = d_model // 2
    kvw = NSA_GROUPS * HEAD_DIM
    col = lambda name, a=0, n=None: w_in[:, lay[name][0] + a: lay[name][0] + a + (n or lay[name][1] - a)]
    heads = lambda w: jnp.concatenate(
        [_pad_cols(w[:, h * HEAD_DIM:(h + 1) * HEAD_DIM], LANES) for h in range(w.shape[1] // HEAD_DIM)], axis=1)
    w32 = jnp.concatenate([
        col("su"), col("sv"), col("rs", 0, half), col("rs", half, half), col("rs", 2 * half, half),
        col("rs", 3 * half, 2 * LORA), _pad_cols(col("ngate"), LANES)], axis=1)
    w_kcvc = col("nkv", 0, 2 * kvw)
    blocks, o = {}, 0
    for name in ("su", "sv", "rr", "rk", "rv"):
        blocks[name] = o // half
        o += half
    blocks["lora"] = o // LANES
    o += LANES
    blocks["gate"] = o // LANES
    w_late = jnp.concatenate([col("mg"), col("nz"), col("sz"), col("rz")], axis=1)
    zero = jnp.zeros((w_in.shape[0], LANES), w_in.dtype)
    grp = lambda w, g: _pad_cols(w[:, g * HEAD_DIM:(g + 1) * HEAD_DIM], LANES)
    ks, vs = col("nkv", 2 * kvw, kvw), col("nkv", 3 * kvw, kvw)
    kw, vw = col("nkv", 4 * kvw, kvw), col("nkv", 5 * kvw, kvw)
    w16 = jnp.concatenate(
        [heads(col("nq"))]
        + [x for g in range(NSA_GROUPS) for x in (grp(ks, g), zero)]
        + [grp(vs, g) for g in range(NSA_GROUPS)]
        + [grp(kw, g) for g in range(NSA_GROUPS)]
        + [grp(vw, g) for g in range(NSA_GROUPS)], axis=1)
    return w32.astype(BF16), w_kcvc.astype(BF16), w16.astype(BF16), w_late.astype(BF16), blocks


def _prep_compress_weights(cmp_w1, cmp_w2, cmp_pe):
    half_blk = CMP_BLOCK // 2
    w1r = cmp_w1.reshape(2, 2, half_blk, HEAD_DIM, CMP_HIDDEN)
    eye = jnp.eye(NSA_GROUPS, dtype=cmp_w1.dtype)
    eye_kv = jnp.eye(2, dtype=cmp_w1.dtype)
    big = jnp.einsum('kaldh,kK,gG->alkgdKGh', w1r, eye_kv, eye)
    big = big.reshape(2, half_blk * 2 * NSA_GROUPS * HEAD_DIM, 2 * NSA_GROUPS * CMP_HIDDEN)
    pe = cmp_pe.reshape(2, 2, half_blk, HEAD_DIM)
    pe2 = jnp.broadcast_to(pe.transpose(1, 2, 0, 3)[:, :, :, None, :],
                           (2, half_blk, 2, NSA_GROUPS, HEAD_DIM)).reshape(2, -1)
    w2p = jnp.pad(cmp_w2, ((0, 0), (0, 0), (0, LANES - HEAD_DIM)))
    return pe2, big[0].astype(BF16), big[1].astype(BF16), w2p.astype(BF16)


def _cover_t(seq):
    n_grp = seq // CMP_STRIDE
    n_slc = seq // SEL_BLOCK
    cmp_start = jnp.arange(n_grp) * CMP_STRIDE
    slc_start = jnp.arange(n_slc) * SEL_BLOCK
    cover = ((cmp_start[None, :] <= slc_start[:, None] + SEL_BLOCK - 1)
             & (cmp_start[None, :] + CMP_BLOCK - 1 >= slc_start[:, None]))
    return jnp.pad(cover, ((0, LANES - n_slc), (0, 0))).astype(BF16)


def kernel(x, p, norm_g, w_in, cmp_w1, cmp_w2, cmp_pe, sg_ln_g, sg_ln_b, sg_w, sg_b, rk_mu, rk_w0,
           rk_w2, rk_a0, rk_a2, rk_kk, rk_ka, rk_rk, rk_lnx_g, rk_lnx_b, w_branch, w_o, ple_norm_g,
           w_ple_gate, w_ple_proj, final_norm_g):
    batch, seq, d = x.shape
    depth = w_in.shape[0]
    half = d // 2
    t = batch * seq
    assert seq % (2 * SEL_KEY_TILE) == 0 and (seq // CMP_STRIDE) % CMP_WIDTH_STEP == 0
    assert seq >= WINDOW + Q_TILE and seq // SEL_BLOCK <= LANES
    x2 = x.reshape(t, d)
    cov_t = _cover_t(seq)
    seg = (jnp.arange(half)[:, None] // HEAD_DIM == jnp.arange(half)[None, :] // HEAD_DIM).astype(BF16)
    row = lambda v: v.reshape(1, -1)
    for i in range(depth):
        w32, w_kcvc, w16, w_late, blk = _prep_proj_weights(w_in[i], d)
        g = row(norm_g[i])
        p32, kcvc, p16 = _proj(x2, g, w32, w_kcvc, w16, seq)

        pe2, wa, wb, w2p = _prep_compress_weights(cmp_w1[i], cmp_w2[i], cmp_pe[i])
        xg = kcvc.reshape(batch, seq // CMP_STRIDE, CMP_STRIDE * 2 * NSA_GROUPS * HEAD_DIM)
        kc, vc = _compress(xg, pe2, wa, wb, w2p)
        o_c, bias = _cmp_select(p16, p32, blk["gate"], kc, vc, cov_t, batch, seq)
        o_s = _sel_attn(p16, p32, blk["gate"], bias, batch, seq)
        o_w = _win_attn(p16, p32, blk["gate"], batch, seq)

        bexp = jnp.repeat(sg_b[i].T, half // SG_GROUPS, axis=1)
        y_sg = _spatial_gating(p32, row(sg_ln_g[i]), row(sg_ln_b[i]), sg_w[i], bexp, blk["su"], blk["sv"])

        mu = rk_mu[i]
        rk_params = dict(
            mu_r=row(mu[0:half]), mu_k=row(mu[half:2 * half]), mu_v=row(mu[2 * half:3 * half]),
            mu_lo=row(mu[3 * half:]), w0=row(rk_w0[i]),
            w2=jnp.pad(rk_w2[i], ((0, LANES - LORA), (0, 0))).astype(BF16), a0=row(rk_a0[i]),
            a2=jnp.pad(rk_a2[i], ((LANES - LORA, 0), (0, 0))).astype(BF16),
            kk=row(rk_kk[i]), ka=row(rk_ka[i]), rk=row(rk_rk[i]), lng=row(rk_lnx_g[i]),
            lnb=row(rk_lnx_b[i]), seg=seg)
        y_rk = _rwkv(p32, rk_params, batch, seq, blk["rr"], blk["rk"], blk["rv"], blk["lora"])

        consts = dict(g=g, wl=w_late, wb0=w_branch[i, 0].astype(BF16), wb1=w_branch[i, 1].astype(BF16),
                      wb2=w_branch[i, 2].astype(BF16), wo=w_o[i].astype(BF16),
                      pleg=row(ple_norm_g[i]), wpg=w_ple_gate[i].astype(BF16),
                      wpp=w_ple_proj[i].astype(BF16), fin=row(final_norm_g))
        x2 = _merge(o_c, o_s, o_w, y_sg, y_rk, x2, p[i].reshape(t, -1), consts,
                    last=(i == depth - 1))
    return x2.reshape(batch, seq, d)
```

```python
import functools

import jax
import jax.numpy as jnp
from jax import lax
from jax.experimental import pallas as pl
from jax.experimental.pallas import tpu as pltpu

F32 = jnp.float32
BF16 = jnp.bfloat16

HEAD_DIM = 64
LANES = 128
NSA_HEADS = 8
NSA_GROUPS = 2
NSA_HPG = NSA_HEADS // NSA_GROUPS
CMP_BLOCK = 32
CMP_STRIDE = 16
CMP_HIDDEN = 128
CMP_WIDTH_STEP = 128
SEL_BLOCK = 64
SEL_TOPK = 16
WINDOW = 512
Q_TILE = 128
SEL_KEY_TILE = 512
SEL_TAIL_STEP = 256
SG_GROUPS = 8
SG_CHUNK = 128
RWKV_HEADS = 8
RWKV_CHUNK = 64
LORA = 64
NORM_EPS = 1e-6
LN_EPS = 1e-5
GN_EPS = 64e-5
MASK_NEG = -1e30
FORCE_BONUS = 1e4
VMEM_LIMIT = 56 * 1024 * 1024


def _cparams(*sem):
    return pltpu.CompilerParams(dimension_semantics=sem, vmem_limit_bytes=VMEM_LIMIT)


def _dot(a, b, precision=None):
    return jnp.dot(a, b, preferred_element_type=F32, precision=precision)


def _dot_nt(a, b, precision=None):
    return lax.dot_general(a, b, (((1,), (1,)), ((), ())), preferred_element_type=F32,
                           precision=precision)


def _dot_tn(a, b, precision=None):
    return lax.dot_general(a, b, (((0,), (0,)), ((), ())), preferred_element_type=F32,
                           precision=precision)


def _split3(x):
    hi = x.astype(BF16)
    r1 = x - hi.astype(F32)
    mid = r1.astype(BF16)
    lo = (r1 - mid.astype(F32)).astype(BF16)
    return hi, mid, lo


def _dot3_lhs(x, w):
    hi, mid, lo = _split3(x)
    return _dot(hi, w) + _dot(mid, w) + _dot(lo, w)


def _dot3_rhs(w, x):
    hi, mid, lo = _split3(x)
    return _dot(w, hi) + _dot(w, mid) + _dot(w, lo)


def _sigmoid(x):
    return 1.0 / (1.0 + jnp.exp(-x))


def _silu(x):
    return x * _sigmoid(x)


def _rmsnorm(x, g):
    return x * lax.rsqrt(jnp.mean(x * x, axis=-1, keepdims=True) + NORM_EPS) * g


P16_Q = 0
P16_KX = NSA_HEADS * LANES
P16_VX = P16_KX + NSA_GROUPS * 2 * LANES
P16_KW = P16_VX + NSA_GROUPS * LANES
P16_VW = P16_KW + NSA_GROUPS * LANES
P16_N = P16_VW + NSA_GROUPS * LANES


PROJ_CHUNK = 2 * LANES


def _proj_kernel(x_ref, g_ref, w32_ref, wkc_ref, w16_ref, o32_ref, okc_ref, o16_ref, ovxt_ref, *,
                 seq, tm):
    h = _rmsnorm(x_ref[...], g_ref[...]).astype(BF16)
    for c0 in range(0, o32_ref.shape[1], PROJ_CHUNK):
        o32_ref[:, c0:c0 + PROJ_CHUNK] = _dot(h, w32_ref[:, c0:c0 + PROJ_CHUNK])
    okc_ref[...] = _dot(h, wkc_ref[...])
    pos = (pl.program_id(0) * tm) % seq + lax.broadcasted_iota(jnp.int32, (tm, LANES), 0)
    lane = lax.broadcasted_iota(jnp.int32, (tm, LANES), 1)
    onehot_blk = jnp.where(pos // SEL_BLOCK == lane, 1.0, 0.0)
    ones_col = jnp.where(lane == HEAD_DIM, 1.0, 0.0)
    zeros = jnp.zeros((tm, LANES), F32)

    def constant(c0):
        if P16_KX <= c0 < P16_VX and ((c0 - P16_KX) // LANES) % 2 == 1:
            return onehot_blk
        if c0 >= P16_VX and not (P16_KW <= c0 < P16_VW):
            return ones_col
        return zeros

    for c0 in range(0, P16_N, PROJ_CHUNK):
        y = _dot(h, w16_ref[:, c0:c0 + PROJ_CHUNK])
        consts = [constant(c0 + i * LANES) for i in range(PROJ_CHUNK // LANES)]
        if any(c is not zeros for c in consts):
            y = y + jnp.concatenate(consts, axis=1)
        o16_ref[:, c0:c0 + PROJ_CHUNK] = y.astype(BF16)
        if c0 == P16_VX:
            for c in range(tm // SEL_TAIL_STEP):
                ovxt_ref[c] = y[c * SEL_TAIL_STEP:(c + 1) * SEL_TAIL_STEP, :].T.astype(BF16)


def _proj(x2, g, w32, wkc, w16, seq, tm=512):
    t, d = x2.shape
    n32, nkc = w32.shape[1], wkc.shape[1]
    const = lambda a: pl.BlockSpec(a.shape, lambda i: (0, 0), pipeline_mode=pl.Buffered(1))
    return pl.pallas_call(
        functools.partial(_proj_kernel, seq=seq, tm=tm),
        grid=(t // tm,),
        in_specs=[pl.BlockSpec((tm, d), lambda i: (i, 0)), const(g), const(w32), const(wkc),
                  const(w16)],
        out_specs=[pl.BlockSpec((tm, n32), lambda i: (i, 0)),
                   pl.BlockSpec((tm, nkc), lambda i: (i, 0)),
                   pl.BlockSpec((tm, P16_N), lambda i: (i, 0)),
                   pl.BlockSpec((tm // SEL_TAIL_STEP, PROJ_CHUNK, SEL_TAIL_STEP), lambda i: (i, 0, 0))],
        out_shape=[jax.ShapeDtypeStruct((t, n32), F32), jax.ShapeDtypeStruct((t, nkc), F32),
                   jax.ShapeDtypeStruct((t, P16_N), BF16),
                   jax.ShapeDtypeStruct((t // SEL_TAIL_STEP, PROJ_CHUNK, SEL_TAIL_STEP), BF16)],
        compiler_params=_cparams("parallel"),
        name="proj",
    )(x2, g, w32, wkc, w16)


def _compress_kernel(x_ref, pe_ref, wa_ref, wb_ref, w2_ref, k_ref, v_ref):
    x = x_ref[0]
    n_grp = x.shape[0]
    a = _dot((x + pe_ref[0:1, :]).astype(BF16), wa_ref[...])
    b = _dot((x + pe_ref[1:2, :]).astype(BF16), wb_ref[...])
    hid = _silu(a + pltpu.roll(b, n_grp - 1, 0)).astype(BF16)
    row = lax.broadcasted_iota(jnp.int32, (n_grp, LANES), 0)
    lane = lax.broadcasted_iota(jnp.int32, (n_grp, LANES), 1)
    live = row < n_grp - 1
    for kv, out_ref in enumerate((k_ref, v_ref)):
        for g in range(NSA_GROUPS):
            c0 = (kv * NSA_GROUPS + g) * CMP_HIDDEN
            y = _dot(hid[:, c0:c0 + CMP_HIDDEN], w2_ref[kv])
            if kv == 1:
                y = y + jnp.where(lane == HEAD_DIM, 1.0, 0.0)
            out_ref[0, :, g * LANES:(g + 1) * LANES] = jnp.where(live, y, 0.0).astype(BF16)


def _compress(xg, pe2, wa, wb, w2p):
    b, n_grp, width = xg.shape
    out = jax.ShapeDtypeStruct((b, n_grp, NSA_GROUPS * LANES), BF16)
    return pl.pallas_call(
        _compress_kernel,
        grid=(b,),
        in_specs=[pl.BlockSpec((1, n_grp, width), lambda i: (i, 0, 0)),
                  pl.BlockSpec(pe2.shape, lambda i: (0, 0)),
                  pl.BlockSpec(wa.shape, lambda i: (0, 0)),
                  pl.BlockSpec(wb.shape, lambda i: (0, 0)),
                  pl.BlockSpec(w2p.shape, lambda i: (0, 0, 0))],
        out_specs=[pl.BlockSpec((1, n_grp, NSA_GROUPS * LANES), lambda i: (i, 0, 0))] * 2,
        out_shape=[out, out],
        compiler_params=_cparams("parallel"),
        name="nsa_compress",
    )(xg, pe2, wa, wb, w2p)


def _stack_heads(q_ref, g):
    parts = [q_ref[:, (g * NSA_HPG + h) * LANES:(g * NSA_HPG + h + 1) * LANES]
             for h in range(NSA_HPG)]
    return jnp.concatenate(parts, axis=0) * jnp.asarray(HEAD_DIM ** -0.5, BF16)


def _store_heads(o_ref, gate_ref, branch, g, o):
    lane = lax.broadcasted_iota(jnp.int32, (Q_TILE, LANES), 1)
    gate = _sigmoid(gate_ref[...])

    def scaled(h):
        col = (g * NSA_HPG + h) * 3 + branch
        return jnp.broadcast_to(gate[:, col:col + 1], (Q_TILE, LANES)) * o[h * Q_TILE:(h + 1) * Q_TILE]

    for j in range(NSA_HPG // 2):
        tile = jnp.where(lane < HEAD_DIM, scaled(2 * j), pltpu.roll(scaled(2 * j + 1), HEAD_DIM, 1))
        c0 = (g * NSA_HPG // 2 + j) * LANES
        o_ref[:, c0:c0 + LANES] = tile


def _cmp_select_kernel(q_ref, gate_ref, kc_ref, vc_ref, cov_ref, o_ref, bias_ref, imp_ref, *, top_k):
    n_cmp = kc_ref.shape[1]
    n_slc = cov_ref.shape[0]
    q0 = pl.program_id(1) * Q_TILE
    rows = NSA_HPG * Q_TILE
    sel_shape = (n_slc, NSA_GROUPS * Q_TILE)
    j_idx = lax.broadcasted_iota(jnp.int32, sel_shape, 0)
    t_lane = q0 + lax.broadcasted_iota(jnp.int32, sel_shape, 1) % Q_TILE
    t_blk = t_lane // SEL_BLOCK
    forced = (j_idx == 0) | (j_idx == t_blk) | (j_idx == t_blk - 1)
    causal_blk = j_idx <= t_blk

    def attend(width):
        t_row = q0 + lax.broadcasted_iota(jnp.int32, (Q_TILE, width), 0)
        cmp_end = lax.broadcasted_iota(jnp.int32, (Q_TILE, width), 1) * CMP_STRIDE + CMP_BLOCK - 1
        mask = jnp.tile(jnp.where(cmp_end <= t_row, 0.0, MASK_NEG), (NSA_HPG, 1))
        any_valid = jnp.tile(jnp.where(t_row[:, 0:1] >= CMP_BLOCK - 1, 1.0, 0.0), (NSA_HPG, 1))
        s_all = [_dot_nt(_stack_heads(q_ref, g), kc_ref[0, 0:width, g * LANES:(g + 1) * LANES])
                 for g in range(NSA_GROUPS)]
        for g in range(NSA_GROUPS):
            s = s_all[g] + mask
            p = jnp.exp(s - jnp.max(s, axis=-1, keepdims=True))
            l = jnp.sum(p, axis=-1, keepdims=True)
            p = p * (any_valid / l)
            o = _dot(p.astype(BF16), vc_ref[0, 0:width, g * LANES:(g + 1) * LANES])
            _store_heads(o_ref, gate_ref, 0, g, o)
            p_sum = p[0:Q_TILE]
            for h in range(1, NSA_HPG):
                p_sum = p_sum + p[h * Q_TILE:(h + 1) * Q_TILE]
            hi, mid, lo = _split3(p_sum)
            cov = cov_ref[:, 0:width]
            imp_ref[:, g * Q_TILE:(g + 1) * Q_TILE] = (_dot_nt(cov, hi) + _dot_nt(cov, mid)
                                                       + _dot_nt(cov, lo))

    n_var = n_cmp // CMP_WIDTH_STEP
    for var in range(n_var):
        @pl.when(q0 // (CMP_WIDTH_STEP * CMP_STRIDE) == var)
        def _():
            attend((var + 1) * CMP_WIDTH_STEP)

    score = jnp.where(causal_blk, jnp.where(forced, FORCE_BONUS, imp_ref[...]), -FORCE_BONUS)

    def pick(_, carry):
        sc, sel = carry
        best = jnp.max(sc, axis=0, keepdims=True)
        first = jnp.min(jnp.where(sc == best, j_idx, n_slc), axis=0, keepdims=True)
        hit = j_idx == first
        return jnp.where(hit, -jnp.inf, sc), jnp.where(hit, 1.0, sel)

    _, sel = lax.fori_loop(0, top_k, pick, (score, jnp.zeros(sel_shape, F32)))
    bias_t = jnp.where((sel > 0.0) & causal_blk, 0.0, MASK_NEG)
    for g in range(NSA_GROUPS):
        bias_ref[:, g * n_slc:(g + 1) * n_slc] = bias_t[:, g * Q_TILE:(g + 1) * Q_TILE].T.astype(BF16)


def _cmp_select(p16, p32, gate_blk, kc, vc, cov_t, batch, seq):
    n_cmp = kc.shape[1]
    n_slc = cov_t.shape[0]
    nq = seq // Q_TILE
    return pl.pallas_call(
        functools.partial(_cmp_select_kernel, top_k=min(SEL_TOPK, seq // SEL_BLOCK)),
        grid=(batch, nq),
        in_specs=[pl.BlockSpec((Q_TILE, NSA_HEADS * LANES), lambda b, i: (b * nq + i, 0)),
                  pl.BlockSpec((Q_TILE, LANES), lambda b, i: (b * nq + i, gate_blk)),
                  pl.BlockSpec((1, n_cmp, NSA_GROUPS * LANES), lambda b, i: (b, 0, 0)),
                  pl.BlockSpec((1, n_cmp, NSA_GROUPS * LANES), lambda b, i: (b, 0, 0)),
                  pl.BlockSpec(cov_t.shape, lambda b, i: (0, 0))],
        out_specs=[pl.BlockSpec((Q_TILE, NSA_HEADS * HEAD_DIM), lambda b, i: (b * nq + i, 0)),
                   pl.BlockSpec((Q_TILE, NSA_GROUPS * n_slc), lambda b, i: (b * nq + i, 0))],
        out_shape=[jax.ShapeDtypeStruct((batch * seq, NSA_HEADS * HEAD_DIM), F32),
                   jax.ShapeDtypeStruct((batch * seq, NSA_GROUPS * n_slc), BF16)],
        scratch_shapes=[pltpu.VMEM((n_slc, NSA_GROUPS * Q_TILE), F32)],
        compiler_params=_cparams("parallel", "arbitrary"),
        name="nsa_cmp_select",
    )(p16, p32, kc, vc, cov_t)


def _sel_attn_kernel(q_ref, gate_ref, bias_ref, kx_ref, vxt_ref, o_ref, m_scr, acc_scr, *, tk):
    n_slc = bias_ref.shape[1] // NSA_GROUPS
    q0 = pl.program_id(1) * Q_TILE
    rows = NSA_HPG * Q_TILE
    wide = 2 * tk
    n_wide = q0 // wide
    qx = []
    for g in range(NSA_GROUPS):
        bias = bias_ref[:, g * n_slc:(g + 1) * n_slc]
        qx.append(jnp.concatenate([_stack_heads(q_ref, g),
                                   jnp.concatenate([bias] * NSA_HPG, axis=0)], axis=1))

    def load():
        return tuple((m_scr[g], acc_scr[g]) for g in range(NSA_GROUPS))

    def store(state):
        for g in range(NSA_GROUPS):
            m_scr[g], acc_scr[g] = state[g]

    def scores(k0, width):
        return [_dot_nt(kx_ref[pl.ds(k0, width), g * 2 * LANES:(g + 1) * 2 * LANES], qx[g])
                for g in range(NSA_GROUPS)]

    def update(k0, width, diagonal, s_all, state):
        c0 = k0 // SEL_TAIL_STEP
        probs, m_all = [], []
        for g in range(NSA_GROUPS):
            s = s_all[g]
            if diagonal:
                key = k0 + lax.broadcasted_iota(jnp.int32, (width, Q_TILE), 0)
                t_lane = q0 + lax.broadcasted_iota(jnp.int32, (width, Q_TILE), 1)
                s = s + jnp.tile(jnp.where(key <= t_lane, 0.0, MASK_NEG), (1, NSA_HPG))
            m_new = jnp.maximum(state[g][0], jnp.max(s, axis=0, keepdims=True))
            probs.append(jnp.exp(s - m_new[0:1, :]).astype(BF16))
            m_all.append(m_new)
        out = []
        for g in range(NSA_GROUPS):
            m_old, acc = state[g]
            pv = None
            for c in range(width // SEL_TAIL_STEP):
                part = _dot(vxt_ref[c0 + c, g * LANES:(g + 1) * LANES, :],
                            probs[g][c * SEL_TAIL_STEP:(c + 1) * SEL_TAIL_STEP, :])
                pv = part if pv is None else pv + part
            out.append((m_all[g], jnp.exp(m_old - m_all[g])[0:1, :] * acc + pv))
        return tuple(out)

    def step(k0, width, diagonal, state):
        return update(k0, width, diagonal, scores(k0, width), state)

    store(tuple((jnp.full((8, rows), MASK_NEG, F32), jnp.zeros((LANES, rows), F32))
                for _ in range(NSA_GROUPS)))

    def pair(j, _):
        k0 = pl.multiple_of(j * 2 * wide, 2 * wide)
        s_a, s_b = scores(k0, wide), scores(k0 + wide, wide)
        store(update(k0 + wide, wide, False, s_b, update(k0, wide, False, s_a, load())))
        return 0

    lax.fori_loop(0, n_wide // 2, pair, 0)

    @pl.when(n_wide % 2 == 1)
    def _():
        store(step(pl.multiple_of((n_wide - 1) * wide, wide), wide, False, load()))

    tail0 = pl.multiple_of(n_wide * wide, wide)
    tail_steps = (q0 + Q_TILE - tail0 + SEL_TAIL_STEP - 1) // SEL_TAIL_STEP
    for var in range(1, wide // SEL_TAIL_STEP + 1):
        @pl.when(tail_steps == var)
        def _():
            store(step(tail0, var * SEL_TAIL_STEP, True, load()))

    for g in range(NSA_GROUPS):
        acc = acc_scr[g]
        o_t = acc / acc[HEAD_DIM:HEAD_DIM + 1, :]
        o = jnp.concatenate([o_t[:, h * Q_TILE:(h + 1) * Q_TILE].T for h in range(NSA_HPG)], axis=0)
        _store_heads(o_ref, gate_ref, 1, g, o)


def _sel_attn(p16, vxt, p32, gate_blk, bias, batch, seq):
    nq = seq // Q_TILE
    tk = min(SEL_KEY_TILE, seq)
    kx_w = NSA_GROUPS * 2 * LANES
    vx_w = NSA_GROUPS * LANES
    return pl.pallas_call(
        functools.partial(_sel_attn_kernel, tk=tk),
        grid=(batch, nq),
        in_specs=[pl.BlockSpec((Q_TILE, NSA_HEADS * LANES), lambda b, i: (b * nq + i, 0)),
                  pl.BlockSpec((Q_TILE, LANES), lambda b, i: (b * nq + i, gate_blk)),
                  pl.BlockSpec((Q_TILE, bias.shape[1]), lambda b, i: (b * nq + i, 0)),
                  pl.BlockSpec((seq, kx_w), lambda b, i: (b, P16_KX // kx_w)),
                  pl.BlockSpec((seq // SEL_TAIL_STEP, vx_w, SEL_TAIL_STEP), lambda b, i: (b, 0, 0))],
        out_specs=pl.BlockSpec((Q_TILE, NSA_HEADS * HEAD_DIM), lambda b, i: (b * nq + i, 0)),
        out_shape=jax.ShapeDtypeStruct((batch * seq, NSA_HEADS * HEAD_DIM), F32),
        scratch_shapes=[pltpu.VMEM((NSA_GROUPS, 8, NSA_HPG * Q_TILE), F32),
                        pltpu.VMEM((NSA_GROUPS, LANES, NSA_HPG * Q_TILE), F32)],
        compiler_params=_cparams("parallel", "arbitrary"),
        name="nsa_sel_attn",
    )(p16, p32, bias, p16, vxt)


def _win_attn_kernel(q_ref, gate_ref, kw_ref, vw_ref, o_ref, *, span):
    seq = kw_ref.shape[0]
    q0 = pl.program_id(1) * Q_TILE
    rows = NSA_HPG * Q_TILE
    start = pl.multiple_of(jnp.clip(q0 + Q_TILE - span, 0, seq - span), Q_TILE)
    t_row = q0 + lax.broadcasted_iota(jnp.int32, (Q_TILE, span), 0)
    pos = start + lax.broadcasted_iota(jnp.int32, (Q_TILE, span), 1)
    mask = jnp.tile(jnp.where((pos <= t_row) & (pos > t_row - WINDOW), 0.0, MASK_NEG), (NSA_HPG, 1))
    s_all = [_dot_nt(_stack_heads(q_ref, g), kw_ref[pl.ds(start, span), g * LANES:(g + 1) * LANES])
             for g in range(NSA_GROUPS)]
    p_all = []
    for g in range(NSA_GROUPS):
        s = s_all[g] + mask
        p_all.append(jnp.exp(s - jnp.max(s, axis=-1, keepdims=True)).astype(BF16))
    for g in range(NSA_GROUPS):
        acc = _dot(p_all[g], vw_ref[pl.ds(start, span), g * LANES:(g + 1) * LANES])
        _store_heads(o_ref, gate_ref, 2, g, acc / acc[:, HEAD_DIM:HEAD_DIM + 1])


def _win_attn(p16, p32, gate_blk, batch, seq):
    nq = seq // Q_TILE
    span = min(WINDOW + Q_TILE, seq)
    w = NSA_GROUPS * LANES
    return pl.pallas_call(
        functools.partial(_win_attn_kernel, span=span),
        grid=(batch, nq),
        in_specs=[pl.BlockSpec((Q_TILE, NSA_HEADS * LANES), lambda b, i: (b * nq + i, 0)),
                  pl.BlockSpec((Q_TILE, LANES), lambda b, i: (b * nq + i, gate_blk)),
                  pl.BlockSpec((seq, w), lambda b, i: (b, P16_KW // w)),
                  pl.BlockSpec((seq, w), lambda b, i: (b, P16_VW // w))],
        out_specs=pl.BlockSpec((Q_TILE, NSA_HEADS * HEAD_DIM), lambda b, i: (b * nq + i, 0)),
        out_shape=jax.ShapeDtypeStruct((batch * seq, NSA_HEADS * HEAD_DIM), F32),
        compiler_params=_cparams("parallel", "arbitrary"),
        name="nsa_win_attn",
    )(p16, p32, p16, p16)


def _spatial_gating_kernel(u_ref, v_ref, g_ref, b_ref, w_ref, bexp_ref, o_ref, *, chunks):
    c = SG_CHUNK
    tri = (lax.broadcasted_iota(jnp.int32, (c, c), 1) <= lax.broadcasted_iota(jnp.int32, (c, c), 0))
    lane = lax.broadcasted_iota(jnp.int32, (c, LANES), 1)
    wm = [jnp.where(tri, w_ref[g], 0.0).astype(BF16) for g in range(SG_GROUPS)]
    for ci in range(chunks):
        v = v_ref[ci * c:(ci + 1) * c, :]
        mu = jnp.mean(v, axis=-1, keepdims=True)
        d = v - mu
        var = jnp.mean(d * d, axis=-1, keepdims=True)
        vn = (d * lax.rsqrt(var + LN_EPS) * g_ref[...] + b_ref[...]).astype(BF16)
        for pair in range(SG_GROUPS // 2):
            vp = vn[:, pair * LANES:(pair + 1) * LANES]
            mixed = jnp.where(lane < HEAD_DIM, _dot(wm[2 * pair], vp), _dot(wm[2 * pair + 1], vp))
            sl = (slice(ci * c, (ci + 1) * c), slice(pair * LANES, (pair + 1) * LANES))
            o_ref[sl] = u_ref[sl] * (mixed + bexp_ref[:, pair * LANES:(pair + 1) * LANES])


def _spatial_gating(p32, ln_g, ln_b, w_s, bexp, u_blk, v_blk, chunks=4):
    t = p32.shape[0]
    width = ln_g.shape[1]
    tm = chunks * SG_CHUNK
    return pl.pallas_call(
        functools.partial(_spatial_gating_kernel, chunks=chunks),
        grid=(t // tm,),
        in_specs=[pl.BlockSpec((tm, width), lambda i: (i, u_blk)),
                  pl.BlockSpec((tm, width), lambda i: (i, v_blk)),
                  pl.BlockSpec((1, width), lambda i: (0, 0)),
                  pl.BlockSpec((1, width), lambda i: (0, 0)),
                  pl.BlockSpec(w_s.shape, lambda i: (0, 0, 0)),
                  pl.BlockSpec(bexp.shape, lambda i: (0, 0))],
        out_specs=pl.BlockSpec((tm, width), lambda i: (i, 0)),
        out_shape=jax.ShapeDtypeStruct((t, width), F32),
        compiler_params=_cparams("parallel"),
        name="spatial_gating",
    )(p32, p32, ln_g, ln_b, w_s, bexp)


def _rwkv_kernel(r_ref, k_ref, v_ref, lo_ref, mu_r, mu_k, mu_v, mu_lo, w0_ref, w2_ref, a0_ref,
                 a2_ref, kk_ref, ka_ref, rk_ref, lng_ref, lnb_ref, seg_ref, o_ref,
                 state, prev_r, prev_k, prev_v, prev_lo, *, nch):
    c = RWKV_CHUNK
    rows = nch * c
    width = r_ref.shape[1]
    n_pairs = width // LANES

    @pl.when(pl.program_id(1) == 0)
    def _():
        state[...] = jnp.zeros_like(state)
        for ref in (prev_r, prev_k, prev_v, prev_lo):
            ref[...] = jnp.zeros_like(ref)

    def token_shift(x_ref, prev_ref, mu_ref):
        x = x_ref[...]
        first = lax.broadcasted_iota(jnp.int32, x.shape, 0) == 0
        prev = jnp.where(first, prev_ref[0:1, :], pltpu.roll(x, 1, 0))
        prev_ref[0:1, :] = x[rows - 1:rows, :]
        return x + (prev - x) * mu_ref[...]

    r = token_shift(r_ref, prev_r, mu_r)
    k = token_shift(k_ref, prev_k, mu_k)
    v = token_shift(v_ref, prev_v, mu_v)
    lo = token_shift(lo_ref, prev_lo, mu_lo)

    lane_lo = lax.broadcasted_iota(jnp.int32, lo.shape, 1)
    w_in = jnp.where(lane_lo < LORA, jnp.tanh(lo), 0.0).astype(BF16)
    a_in = jnp.where(lane_lo < LORA, 0.0, lo).astype(BF16)
    z = -(w0_ref[...] + _dot(w_in, w2_ref[...]))
    w = -(jnp.maximum(z, 0.0) + jnp.log(1.0 + jnp.exp(-jnp.abs(z)))) - 0.5
    a = _sigmoid(a0_ref[...] + _dot(a_in, a2_ref[...]))

    seg = seg_ref[...]
    kkf = k * kk_ref[...]
    kk = kkf / jnp.maximum(jnp.sqrt(_dot3_lhs(kkf * kkf, seg)), 1e-12)
    k = k * (1.0 + (a - 1.0) * ka_ref[...])
    logd = -jnp.exp(w)
    t_row = lax.broadcasted_iota(jnp.int32, (rows, rows), 0)
    t_col = lax.broadcasted_iota(jnp.int32, (rows, rows), 1)
    cum = (t_col <= t_row) & (t_col // c == t_row // c)
    cl = _dot3_rhs(jnp.where(cum, 1.0, 0.0).astype(BF16), logd)
    cl_end = cl.reshape(nch, c, width)[:, c - 1:c, :]
    e_neg = jnp.exp(-cl)
    e_end = jnp.exp(jnp.broadcast_to(cl_end, (nch, c, width)).reshape(rows, width) - cl)
    b = kk * a
    a_t = -kk * jnp.exp(cl - logd)
    b_t = b * e_neg
    k_t = k * e_neg
    r_t = r * jnp.exp(cl)
    b_e = b * e_end
    k_e = k * e_end
    p_end = jnp.exp(cl_end)

    lane = lax.broadcasted_iota(jnp.int32, (c, LANES), 1)
    row2 = lax.broadcasted_iota(jnp.int32, (2 * c, 2 * c), 0)
    col2 = lax.broadcasted_iota(jnp.int32, (2 * c, 2 * c), 1)
    same_head = (row2 // c) == (col2 // c)
    strict = same_head & (col2 < row2)
    incl = same_head & (col2 <= row2)
    eye = jnp.where(row2 == col2, 1.0, 0.0)
    zero16 = jnp.zeros((c, LANES), BF16)

    def stack(x, ci, pair):
        xp = x[ci * c:(ci + 1) * c, pair * LANES:(pair + 1) * LANES].astype(BF16)
        return jnp.concatenate([jnp.where(lane < HEAD_DIM, xp, zero16),
                                jnp.where(lane < HEAD_DIM, zero16, xp)], axis=0)

    def off(m):
        return (((row2 % (2 * m)) >= m) & ((col2 // (2 * m)) == (row2 // (2 * m)))
                & ((col2 % (2 * m)) < m))

    combos = [(ci, pair) for ci in range(nch) for pair in range(n_pairs)]
    a_st = {q: stack(a_t, *q) for q in combos}
    r_st = {q: stack(r_t, *q) for q in combos}
    v_st = {q: stack(v, *q) for q in combos}
    g = {q: _dot_nt(jnp.concatenate([a_st[q], r_st[q]], axis=0),
                    jnp.concatenate([stack(b_t, *q), stack(k_t, *q)], axis=0)) for q in combos}
    l_ab = {q: jnp.where(strict, g[q][0:2 * c, 0:2 * c], 0.0) for q in combos}
    inv = {q: eye + jnp.where(off(1), l_ab[q], 0.0) for q in combos}
    m = 2
    while m < c:
        inv16 = {q: inv[q].astype(BF16) for q in combos}
        x = {q: _dot(jnp.where(off(m), l_ab[q], 0.0).astype(BF16), inv16[q]) for q in combos}
        inv = {q: inv[q] + _dot(inv16[q], x[q].astype(BF16)) for q in combos}
        m *= 2
    lakv = {q: _dot(jnp.where(strict, g[q][0:2 * c, 2 * c:4 * c], 0.0).astype(BF16), v_st[q])
            for q in combos}
    wu = {q: _dot(inv[q].astype(BF16), jnp.concatenate([a_st[q], lakv[q].astype(BF16)], axis=1))
          for q in combos}
    m_rbk = {q: jnp.concatenate([jnp.where(incl, g[q][2 * c:4 * c, 0:2 * c], 0.0),
                                 jnp.where(incl, g[q][2 * c:4 * c, 2 * c:4 * c], 0.0)],
                                axis=1).astype(BF16) for q in combos}
    bke_st = {q: jnp.concatenate([stack(b_e, *q), stack(k_e, *q)], axis=0) for q in combos}

    st = [state[pair] for pair in range(n_pairs)]
    ys = [[None] * n_pairs for _ in range(nch)]
    for ci in range(nch):
        wr = {pair: _dot(jnp.concatenate([wu[ci, pair][:, 0:LANES].astype(BF16), r_st[ci, pair]],
                                         axis=0), st[pair].astype(BF16)) for pair in range(n_pairs)}
        for pair in range(n_pairs):
            q = (ci, pair)
            u_st = wr[pair][0:2 * c] + wu[q][:, LANES:2 * LANES]
            uv_st = jnp.concatenate([u_st.astype(BF16), v_st[q]], axis=0)
            y_st = wr[pair][2 * c:4 * c] + _dot(m_rbk[q], uv_st)
            decay_col = jnp.broadcast_to(p_end[ci, :, pair * LANES:(pair + 1) * LANES],
                                         (LANES, LANES)).T
            st[pair] = st[pair] * decay_col + _dot_tn(bke_st[q], uv_st)
            ys[ci][pair] = y_st[0:c] + y_st[c:2 * c]
    for pair in range(n_pairs):
        state[pair] = st[pair]
    y = jnp.concatenate([jnp.concatenate(yr, axis=1) for yr in ys], axis=0)

    seg_mean = (seg * (1.0 / HEAD_DIM)).astype(BF16)
    mean = _dot3_lhs(y, seg_mean)
    d = y - mean
    var = _dot3_lhs(d * d, seg_mean)
    yn = d * lax.rsqrt(var + GN_EPS) * lng_ref[...] + lnb_ref[...]
    bonus = _dot3_lhs(r * k * rk_ref[...], seg) * v
    o_ref[...] = yn + bonus


def _rwkv(p32, params, batch, seq, r_blk, k_blk, v_blk, lo_blk, nch=4):
    width = params["w0"].shape[1]
    c = nch * RWKV_CHUNK
    nc = seq // c
    full = lambda a: pl.BlockSpec(a.shape, lambda b, i: (0,) * a.ndim)
    names = ("mu_r", "mu_k", "mu_v", "mu_lo", "w0", "w2", "a0", "a2", "kk", "ka", "rk",
             "lng", "lnb", "seg")
    return pl.pallas_call(
        functools.partial(_rwkv_kernel, nch=nch),
        grid=(batch, nc),
        in_specs=[pl.BlockSpec((c, width), lambda b, i: (b * nc + i, r_blk)),
                  pl.BlockSpec((c, width), lambda b, i: (b * nc + i, k_blk)),
                  pl.BlockSpec((c, width), lambda b, i: (b * nc + i, v_blk)),
                  pl.BlockSpec((c, LANES), lambda b, i: (b * nc + i, lo_blk))]
                 + [full(params[n]) for n in names],
        out_specs=pl.BlockSpec((c, width), lambda b, i: (b * nc + i, 0)),
        out_shape=jax.ShapeDtypeStruct((batch * seq, width), F32),
        scratch_shapes=[pltpu.VMEM((width // LANES, LANES, LANES), F32),
                        pltpu.VMEM((8, width), F32), pltpu.VMEM((8, width), F32),
                        pltpu.VMEM((8, width), F32), pltpu.VMEM((8, LANES), F32)],
        compiler_params=_cparams("parallel", "arbitrary"),
        name="rwkv7",
    )(p32, p32, p32, p32, *[params[n] for n in names])


def _merge_kernel(oc_ref, os_ref, ow_ref, sg_ref, rk_ref, x_ref, p_ref, g_ref, wl_ref,
                  wb0_ref, wb1_ref, wb2_ref, wo_ref, pleg_ref, wpg_ref, wpp_ref, fin_ref,
                  o_ref, *, last):
    d = x_ref.shape[1]
    half = sg_ref.shape[1]
    nzw = oc_ref.shape[1]
    h = _rmsnorm(x_ref[...], g_ref[...]).astype(BF16)
    late = lambda c0, n: _dot(h, wl_ref[:, c0:c0 + n])
    y_nsa = oc_ref[...] + os_ref[...] + ow_ref[...]
    z0 = _dot((y_nsa * _silu(late(3 * d, nzw))).astype(BF16), wb0_ref[...])
    merged = _sigmoid(late(0, d)) * z0
    z1 = _dot((sg_ref[...] * _silu(late(3 * d + nzw, half))).astype(BF16), wb1_ref[...])
    merged = merged + _sigmoid(late(d, d)) * z1
    z2 = _dot((rk_ref[...] * _silu(late(3 * d + nzw + half, half))).astype(BF16), wb2_ref[...])
    merged = merged + _sigmoid(late(2 * d, d)) * z2
    x1 = x_ref[...] + _dot(merged.astype(BF16), wo_ref[...])
    hp = _rmsnorm(x1, pleg_ref[...]).astype(BF16)
    x2 = x1 + _sigmoid(_dot(hp, wpg_ref[...])) * _dot(p_ref[...].astype(BF16), wpp_ref[...])
    o_ref[...] = _rmsnorm(x2, fin_ref[...]) if last else x2


def _merge(o_c, o_s, o_w, y_sg, y_rk, x2, p2, consts, last, tm=256):
    t, d = x2.shape
    half = y_sg.shape[1]
    tok = lambda w: pl.BlockSpec((tm, w), lambda i: (i, 0))
    full = lambda a: pl.BlockSpec(a.shape, lambda i: (0,) * a.ndim, pipeline_mode=pl.Buffered(1))
    names = ("g", "wl", "wb0", "wb1", "wb2", "wo", "pleg", "wpg", "wpp", "fin")
    return pl.pallas_call(
        functools.partial(_merge_kernel, last=last),
        grid=(t // tm,),
        in_specs=[tok(o_c.shape[1]), tok(o_s.shape[1]), tok(o_w.shape[1]),
                  tok(half), tok(half), tok(d), tok(p2.shape[1])]
                 + [full(consts[n]) for n in names],
        out_specs=tok(d),
        out_shape=jax.ShapeDtypeStruct((t, d), F32),
        compiler_params=_cparams("parallel"),
        name="merge_out_ple",
    )(o_c, o_s, o_w, y_sg, y_rk, x2, p2, *[consts[n] for n in names])


def _pad_cols(w, n):
    return jnp.pad(w, ((0, 0), (0, n - w.shape[1])))


def _layer_layout(d_model):
    half = d_model // 2
    kvw = NSA_GROUPS * HEAD_DIM
    sizes = dict(nq=half, nkv=6 * kvw, ngate=3 * NSA_HEADS, nz=half, su=half, sv=half, sz=half,
                 rs=3 * half + 2 * LORA, rz=half, mg=3 * d_model)
    off, out = 0, {}
    for name, n in sizes.items():
        out[name] = (off, n)
        off += n
    return out


def _prep_proj_weights(w_in, d_model):
    lay = _layer_layout(d_model)
    half = d_model // 2
    kvw = NSA_GROUPS * HEAD_DIM
    col = lambda name, a=0, n=None: w_in[:, lay[name][0] + a: lay[name][0] + a + (n or lay[name][1] - a)]
    heads = lambda w: jnp.concatenate(
        [_pad_cols(w[:, h * HEAD_DIM:(h + 1) * HEAD_DIM], LANES) for h in range(w.shape[1] // HEAD_DIM)], axis=1)
    w32 = jnp.concatenate([
        col("su"), col("sv"), col("rs", 0, half), col("rs", half, half), col("rs", 2 * half, half),
        col("rs", 3 * half, 2 * LORA), _pad_cols(col("ngate"), LANES)], axis=1)
    w_kcvc = col("nkv", 0, 2 * kvw)
    blocks, o = {}, 0
    for name in ("su", "sv", "rr", "rk", "rv"):
        blocks[name] = o // half
        o += half
    blocks["lora"] = o // LANES
    o += LANES
    blocks["gate"] = o // LANES
    w_late = jnp.concatenate([col("mg"), col("nz"), col("sz"), col("rz")], axis=1)
    zero = jnp.zeros((w_in.shape[0], LANES), w_in.dtype)
    grp = lambda w, g: _pad_cols(w[:, g * HEAD_DIM:(g + 1) * HEAD_DIM], LANES)
    ks, vs = col("nkv", 2 * kvw, kvw), col("nkv", 3 * kvw, kvw)
    kw, vw = col("nkv", 4 * kvw, kvw), col("nkv", 5 * kvw, kvw)
    w16 = jnp.concatenate(
        [heads(col("nq"))]
        + [x for g in range(NSA_GROUPS) for x in (grp(ks, g), zero)]
        + [grp(vs, g) for g in range(NSA_GROUPS)]
        + [grp(kw, g) for g in range(NSA_GROUPS)]
        + [grp(vw, g) for g in range(NSA_GROUPS)], axis=1)
    return w32.astype(BF16), w_kcvc.astype(BF16), w16.astype(BF16), w_late.astype(BF16), blocks


def _prep_compress_weights(cmp_w1, cmp_w2, cmp_pe):
    half_blk = CMP_BLOCK // 2
    w1r = cmp_w1.reshape(2, 2, half_blk, HEAD_DIM, CMP_HIDDEN)
    eye = jnp.eye(NSA_GROUPS, dtype=cmp_w1.dtype)
    eye_kv = jnp.eye(2, dtype=cmp_w1.dtype)
    big = jnp.einsum('kaldh,kK,gG->alkgdKGh', w1r, eye_kv, eye)
    big = big.reshape(2, half_blk * 2 * NSA_GROUPS * HEAD_DIM, 2 * NSA_GROUPS * CMP_HIDDEN)
    pe = cmp_pe.reshape(2, 2, half_blk, HEAD_DIM)
    pe2 = jnp.broadcast_to(pe.transpose(1, 2, 0, 3)[:, :, :, None, :],
                           (2, half_blk, 2, NSA_GROUPS, HEAD_DIM)).reshape(2, -1)
    w2p = jnp.pad(cmp_w2, ((0, 0), (0, 0), (0, LANES - HEAD_DIM)))
    return pe2, big[0].astype(BF16), big[1].astype(BF16), w2p.astype(BF16)


def _cover_t(seq):
    n_grp = seq // CMP_STRIDE
    n_slc = seq // SEL_BLOCK
    cmp_start = jnp.arange(n_grp) * CMP_STRIDE
    slc_start = jnp.arange(n_slc) * SEL_BLOCK
    cover = ((cmp_start[None, :] <= slc_start[:, None] + SEL_BLOCK - 1)
             & (cmp_start[None, :] + CMP_BLOCK - 1 >= slc_start[:, None]))
    return jnp.pad(cover, ((0, LANES - n_slc), (0, 0))).astype(BF16)


def kernel(x, p, norm_g, w_in, cmp_w1, cmp_w2, cmp_pe, sg_ln_g, sg_ln_b, sg_w, sg_b, rk_mu, rk_w0,
           rk_w2, rk_a0, rk_a2, rk_kk, rk_ka, rk_rk, rk_lnx_g, rk_lnx_b, w_branch, w_o, ple_norm_g,
           w_ple_gate, w_ple_proj, final_norm_g):
    batch, seq, d = x.shape
    depth = w_in.shape[0]
    half = d // 2
    t = batch * seq
    assert seq % (2 * SEL_KEY_TILE) == 0 and (seq // CMP_STRIDE) % CMP_WIDTH_STEP == 0
    assert seq >= WINDOW + Q_TILE and seq // SEL_BLOCK <= LANES
    x2 = x.reshape(t, d)
    cov_t = _cover_t(seq)
    seg = (jnp.arange(half)[:, None] // HEAD_DIM == jnp.arange(half)[None, :] // HEAD_DIM).astype(BF16)
    row = lambda v: v.reshape(1, -1)
    for i in range(depth):
        w32, w_kcvc, w16, w_late, blk = _prep_proj_weights(w_in[i], d)
        g = row(norm_g[i])
        p32, kcvc, p16, vxt = _proj(x2, g, w32, w_kcvc, w16, seq)

        pe2, wa, wb, w2p = _prep_compress_weights(cmp_w1[i], cmp_w2[i], cmp_pe[i])
        xg = kcvc.reshape(batch, seq // CMP_STRIDE, CMP_STRIDE * 2 * NSA_GROUPS * HEAD_DIM)
        kc, vc = _compress(xg, pe2, wa, wb, w2p)
        o_c, bias = _cmp_select(p16, p32, blk["gate"], kc, vc, cov_t, batch, seq)
        o_s = _sel_attn(p16, vxt, p32, blk["gate"], bias, batch, seq)
        o_w = _win_attn(p16, p32, blk["gate"], batch, seq)

        bexp = jnp.repeat(sg_b[i].T, half // SG_GROUPS, axis=1)
        y_sg = _spatial_gating(p32, row(sg_ln_g[i]), row(sg_ln_b[i]), sg_w[i], bexp, blk["su"], blk["sv"])

        mu = rk_mu[i]
        rk_params = dict(
            mu_r=row(mu[0:half]), mu_k=row(mu[half:2 * half]), mu_v=row(mu[2 * half:3 * half]),
            mu_lo=row(mu[3 * half:]), w0=row(rk_w0[i]),
            w2=jnp.pad(rk_w2[i], ((0, LANES - LORA), (0, 0))).astype(BF16), a0=row(rk_a0[i]),
            a2=jnp.pad(rk_a2[i], ((LANES - LORA, 0), (0, 0))).astype(BF16),
            kk=row(rk_kk[i]), ka=row(rk_ka[i]), rk=row(rk_rk[i]), lng=row(rk_lnx_g[i]),
            lnb=row(rk_lnx_b[i]), seg=seg)
        y_rk = _rwkv(p32, rk_params, batch, seq, blk["rr"], blk["rk"], blk["rv"], blk["lora"])

        consts = dict(g=g, wl=w_late, wb0=w_branch[i, 0].astype(BF16), wb1=w_branch[i, 1].astype(BF16),
                      wb2=w_branch[i, 2].astype(BF16), wo=w_o[i].astype(BF16),
                      pleg=row(ple_norm_g[i]), wpg=w_ple_gate[i].astype(BF16),
                      wpp=w_ple_proj[i].astype(BF16), fin=row(final_norm_g))
        x2 = _merge(o_c, o_s, o_w, y_sg, y_rk, x2, p[i].reshape(t, -1), consts,
                    last=(i == depth - 1))
    return x2.reshape(batch, seq, d)
```

```python
import functools

import jax
import jax.numpy as jnp
from jax import lax
from jax.experimental import pallas as pl
from jax.experimental.pallas import tpu as pltpu

F32 = jnp.float32
BF16 = jnp.bfloat16

HEAD_DIM = 64
LANES = 128
NSA_HEADS = 8
NSA_GROUPS = 2
NSA_HPG = NSA_HEADS // NSA_GROUPS
CMP_BLOCK = 32
CMP_STRIDE = 16
CMP_HIDDEN = 128
CMP_WIDTH_STEP = 128
SEL_BLOCK = 64
SEL_TOPK = 16
WINDOW = 512
Q_TILE = 128
SEL_KEY_TILE = 512
SEL_TAIL_STEP = 256
SG_GROUPS = 8
SG_CHUNK = 128
RWKV_HEADS = 8
RWKV_CHUNK = 64
LORA = 64
NORM_EPS = 1e-6
LN_EPS = 1e-5
GN_EPS = 64e-5
MASK_NEG = -1e30
FORCE_BONUS = 1e4
VMEM_LIMIT = 56 * 1024 * 1024


def _cparams(*sem):
    return pltpu.CompilerParams(dimension_semantics=sem, vmem_limit_bytes=VMEM_LIMIT)


def _dot(a, b, precision=None):
    return jnp.dot(a, b, preferred_element_type=F32, precision=precision)


def _dot_nt(a, b, precision=None):
    return lax.dot_general(a, b, (((1,), (1,)), ((), ())), preferred_element_type=F32,
                           precision=precision)


def _dot_tn(a, b, precision=None):
    return lax.dot_general(a, b, (((0,), (0,)), ((), ())), preferred_element_type=F32,
                           precision=precision)


def _split3(x):
    hi = x.astype(BF16)
    r1 = x - hi.astype(F32)
    mid = r1.astype(BF16)
    lo = (r1 - mid.astype(F32)).astype(BF16)
    return hi, mid, lo


def _dot3_lhs(x, w):
    hi, mid, lo = _split3(x)
    return _dot(hi, w) + _dot(mid, w) + _dot(lo, w)


def _dot3_rhs(w, x):
    hi, mid, lo = _split3(x)
    return _dot(w, hi) + _dot(w, mid) + _dot(w, lo)


def _sigmoid(x):
    return 1.0 / (1.0 + jnp.exp(-x))


def _silu(x):
    return x * _sigmoid(x)


def _rmsnorm(x, g):
    return x * lax.rsqrt(jnp.mean(x * x, axis=-1, keepdims=True) + NORM_EPS) * g


P16_Q = 0
P16_KX = NSA_HEADS * LANES
P16_VX = P16_KX + NSA_GROUPS * 2 * LANES
P16_KW = P16_VX + NSA_GROUPS * LANES
P16_VW = P16_KW + NSA_GROUPS * LANES
P16_N = P16_VW + NSA_GROUPS * LANES


PROJ_CHUNK = 2 * LANES


def _proj_kernel(x_ref, g_ref, w32_ref, wkc_ref, w16_ref, o32_ref, okc_ref, o16_ref, *, seq, tm):
    h = _rmsnorm(x_ref[...], g_ref[...]).astype(BF16)
    for c0 in range(0, o32_ref.shape[1], PROJ_CHUNK):
        o32_ref[:, c0:c0 + PROJ_CHUNK] = _dot(h, w32_ref[:, c0:c0 + PROJ_CHUNK])
    okc_ref[...] = _dot(h, wkc_ref[...])
    pos = (pl.program_id(0) * tm) % seq + lax.broadcasted_iota(jnp.int32, (tm, LANES), 0)
    lane = lax.broadcasted_iota(jnp.int32, (tm, LANES), 1)
    onehot_blk = jnp.where(pos // SEL_BLOCK == lane, 1.0, 0.0)
    ones_col = jnp.where(lane == HEAD_DIM, 1.0, 0.0)
    zeros = jnp.zeros((tm, LANES), F32)

    def constant(c0):
        if P16_KX <= c0 < P16_VX and ((c0 - P16_KX) // LANES) % 2 == 1:
            return onehot_blk
        if c0 >= P16_VX and not (P16_KW <= c0 < P16_VW):
            return ones_col
        return zeros

    for c0 in range(0, P16_N, PROJ_CHUNK):
        y = _dot(h, w16_ref[:, c0:c0 + PROJ_CHUNK])
        consts = [constant(c0 + i * LANES) for i in range(PROJ_CHUNK // LANES)]
        if any(c is not zeros for c in consts):
            y = y + jnp.concatenate(consts, axis=1)
        o16_ref[:, c0:c0 + PROJ_CHUNK] = y.astype(BF16)


def _proj(x2, g, w32, wkc, w16, seq, tm=512):
    t, d = x2.shape
    n32, nkc = w32.shape[1], wkc.shape[1]
    const = lambda a: pl.BlockSpec(a.shape, lambda i: (0, 0), pipeline_mode=pl.Buffered(1))
    return pl.pallas_call(
        functools.partial(_proj_kernel, seq=seq, tm=tm),
        grid=(t // tm,),
        in_specs=[pl.BlockSpec((tm, d), lambda i: (i, 0)), const(g), const(w32), const(wkc),
                  const(w16)],
        out_specs=[pl.BlockSpec((tm, n32), lambda i: (i, 0)),
                   pl.BlockSpec((tm, nkc), lambda i: (i, 0)),
                   pl.BlockSpec((tm, P16_N), lambda i: (i, 0))],
        out_shape=[jax.ShapeDtypeStruct((t, n32), F32), jax.ShapeDtypeStruct((t, nkc), F32),
                   jax.ShapeDtypeStruct((t, P16_N), BF16)],
        compiler_params=_cparams("parallel"),
        name="proj",
    )(x2, g, w32, wkc, w16)


def _compress_kernel(x_ref, pe_ref, wa_ref, wb_ref, w2_ref, k_ref, v_ref):
    x = x_ref[0]
    n_grp = x.shape[0]
    a = _dot((x + pe_ref[0:1, :]).astype(BF16), wa_ref[...])
    b = _dot((x + pe_ref[1:2, :]).astype(BF16), wb_ref[...])
    hid = _silu(a + pltpu.roll(b, n_grp - 1, 0)).astype(BF16)
    row = lax.broadcasted_iota(jnp.int32, (n_grp, LANES), 0)
    lane = lax.broadcasted_iota(jnp.int32, (n_grp, LANES), 1)
    live = row < n_grp - 1
    for kv, out_ref in enumerate((k_ref, v_ref)):
        for g in range(NSA_GROUPS):
            c0 = (kv * NSA_GROUPS + g) * CMP_HIDDEN
            y = _dot(hid[:, c0:c0 + CMP_HIDDEN], w2_ref[kv])
            if kv == 1:
                y = y + jnp.where(lane == HEAD_DIM, 1.0, 0.0)
            out_ref[0, :, g * LANES:(g + 1) * LANES] = jnp.where(live, y, 0.0).astype(BF16)


def _compress(xg, pe2, wa, wb, w2p):
    b, n_grp, width = xg.shape
    out = jax.ShapeDtypeStruct((b, n_grp, NSA_GROUPS * LANES), BF16)
    return pl.pallas_call(
        _compress_kernel,
        grid=(b,),
        in_specs=[pl.BlockSpec((1, n_grp, width), lambda i: (i, 0, 0)),
                  pl.BlockSpec(pe2.shape, lambda i: (0, 0)),
                  pl.BlockSpec(wa.shape, lambda i: (0, 0)),
                  pl.BlockSpec(wb.shape, lambda i: (0, 0)),
                  pl.BlockSpec(w2p.shape, lambda i: (0, 0, 0))],
        out_specs=[pl.BlockSpec((1, n_grp, NSA_GROUPS * LANES), lambda i: (i, 0, 0))] * 2,
        out_shape=[out, out],
        compiler_params=_cparams("parallel"),
        name="nsa_compress",
    )(xg, pe2, wa, wb, w2p)


def _stack_heads(q_ref, g):
    parts = [q_ref[:, (g * NSA_HPG + h) * LANES:(g * NSA_HPG + h + 1) * LANES]
             for h in range(NSA_HPG)]
    return jnp.concatenate(parts, axis=0) * jnp.asarray(HEAD_DIM ** -0.5, BF16)


def _store_heads(o_ref, gate_ref, branch, g, o, accumulate=False):
    lane = lax.broadcasted_iota(jnp.int32, (Q_TILE, LANES), 1)
    gate = _sigmoid(gate_ref[...])

    def scaled(h):
        col = (g * NSA_HPG + h) * 3 + branch
        return jnp.broadcast_to(gate[:, col:col + 1], (Q_TILE, LANES)) * o[h * Q_TILE:(h + 1) * Q_TILE]

    for j in range(NSA_HPG // 2):
        tile = jnp.where(lane < HEAD_DIM, scaled(2 * j), pltpu.roll(scaled(2 * j + 1), HEAD_DIM, 1))
        c0 = (g * NSA_HPG // 2 + j) * LANES
        o_ref[:, c0:c0 + LANES] = o_ref[:, c0:c0 + LANES] + tile if accumulate else tile


def _cmp_select_kernel(q_ref, gate_ref, kc_ref, vc_ref, cov_ref, kw_ref, vw_ref, o_ref, bias_ref,
                       imp_ref, *, top_k):
    n_cmp = kc_ref.shape[1]
    n_slc = cov_ref.shape[0]
    q0 = pl.program_id(1) * Q_TILE
    rows = NSA_HPG * Q_TILE
    sel_shape = (n_slc, NSA_GROUPS * Q_TILE)
    j_idx = lax.broadcasted_iota(jnp.int32, sel_shape, 0)
    t_lane = q0 + lax.broadcasted_iota(jnp.int32, sel_shape, 1) % Q_TILE
    t_blk = t_lane // SEL_BLOCK
    forced = (j_idx == 0) | (j_idx == t_blk) | (j_idx == t_blk - 1)
    causal_blk = j_idx <= t_blk

    def attend(width):
        t_row = q0 + lax.broadcasted_iota(jnp.int32, (Q_TILE, width), 0)
        cmp_end = lax.broadcasted_iota(jnp.int32, (Q_TILE, width), 1) * CMP_STRIDE + CMP_BLOCK - 1
        mask = jnp.tile(jnp.where(cmp_end <= t_row, 0.0, MASK_NEG), (NSA_HPG, 1))
        any_valid = jnp.tile(jnp.where(t_row[:, 0:1] >= CMP_BLOCK - 1, 1.0, 0.0), (NSA_HPG, 1))
        s_all = [_dot_nt(_stack_heads(q_ref, g), kc_ref[0, 0:width, g * LANES:(g + 1) * LANES])
                 for g in range(NSA_GROUPS)]
        for g in range(NSA_GROUPS):
            s = s_all[g] + mask
            p = jnp.exp(s - jnp.max(s, axis=-1, keepdims=True))
            l = jnp.sum(p, axis=-1, keepdims=True)
            p = p * (any_valid / l)
            o = _dot(p.astype(BF16), vc_ref[0, 0:width, g * LANES:(g + 1) * LANES])
            _store_heads(o_ref, gate_ref, 0, g, o, accumulate=True)
            p_sum = p[0:Q_TILE]
            for h in range(1, NSA_HPG):
                p_sum = p_sum + p[h * Q_TILE:(h + 1) * Q_TILE]
            hi, mid, lo = _split3(p_sum)
            cov = cov_ref[:, 0:width]
            imp_ref[:, g * Q_TILE:(g + 1) * Q_TILE] = (_dot_nt(cov, hi) + _dot_nt(cov, mid)
                                                       + _dot_nt(cov, lo))

    _window_attention(q_ref, gate_ref, kw_ref, vw_ref, o_ref, q0)
    n_var = n_cmp // CMP_WIDTH_STEP
    for var in range(n_var):
        @pl.when(q0 // (CMP_WIDTH_STEP * CMP_STRIDE) == var)
        def _():
            attend((var + 1) * CMP_WIDTH_STEP)

    score = jnp.where(causal_blk, jnp.where(forced, FORCE_BONUS, imp_ref[...]), -FORCE_BONUS)

    j_grp = j_idx[:, 0:Q_TILE]

    def pick(_, carry):
        out = []
        for sc in carry:
            best = jnp.max(sc, axis=0, keepdims=True)
            first = jnp.min(jnp.where(sc == best, j_grp, n_slc), axis=0, keepdims=True)
            out.append(jnp.where(j_grp == first, -jnp.inf, sc))
        return tuple(out)

    init = tuple(score[:, g * Q_TILE:(g + 1) * Q_TILE] for g in range(NSA_GROUPS))
    picked = lax.fori_loop(0, top_k, pick, init)
    for g in range(NSA_GROUPS):
        chosen = (picked[g] == -jnp.inf) & causal_blk[:, g * Q_TILE:(g + 1) * Q_TILE]
        bias_ref[:, g * n_slc:(g + 1) * n_slc] = jnp.where(chosen, 0.0, MASK_NEG).T.astype(BF16)


def _cmp_select(p16, p32, gate_blk, kc, vc, cov_t, batch, seq):
    n_cmp = kc.shape[1]
    n_slc = cov_t.shape[0]
    nq = seq // Q_TILE
    w = NSA_GROUPS * LANES
    return pl.pallas_call(
        functools.partial(_cmp_select_kernel, top_k=min(SEL_TOPK, seq // SEL_BLOCK)),
        grid=(batch, nq),
        in_specs=[pl.BlockSpec((Q_TILE, NSA_HEADS * LANES), lambda b, i: (b * nq + i, 0)),
                  pl.BlockSpec((Q_TILE, LANES), lambda b, i: (b * nq + i, gate_blk)),
                  pl.BlockSpec((1, n_cmp, w), lambda b, i: (b, 0, 0)),
                  pl.BlockSpec((1, n_cmp, w), lambda b, i: (b, 0, 0)),
                  pl.BlockSpec(cov_t.shape, lambda b, i: (0, 0)),
                  pl.BlockSpec((seq, w), lambda b, i: (b, P16_KW // w)),
                  pl.BlockSpec((seq, w), lambda b, i: (b, P16_VW // w))],
        out_specs=[pl.BlockSpec((Q_TILE, NSA_HEADS * HEAD_DIM), lambda b, i: (b * nq + i, 0)),
                   pl.BlockSpec((Q_TILE, NSA_GROUPS * n_slc), lambda b, i: (b * nq + i, 0))],
        out_shape=[jax.ShapeDtypeStruct((batch * seq, NSA_HEADS * HEAD_DIM), F32),
                   jax.ShapeDtypeStruct((batch * seq, NSA_GROUPS * n_slc), BF16)],
        scratch_shapes=[pltpu.VMEM((n_slc, NSA_GROUPS * Q_TILE), F32)],
        compiler_params=_cparams("parallel", "arbitrary"),
        name="nsa_cmp_select",
    )(p16, p32, kc, vc, cov_t, p16, p16)


def _sel_attn_kernel(q_ref, gate_ref, bias_ref, kx_ref, vx_ref, o_ref, m_scr, acc_scr, *, tk):
    n_slc = bias_ref.shape[1] // NSA_GROUPS
    q0 = pl.program_id(1) * Q_TILE
    rows = NSA_HPG * Q_TILE
    wide = 2 * tk
    n_wide = q0 // wide
    qx = []
    for g in range(NSA_GROUPS):
        bias = bias_ref[:, g * n_slc:(g + 1) * n_slc]
        qx.append(jnp.concatenate([_stack_heads(q_ref, g),
                                   jnp.concatenate([bias] * NSA_HPG, axis=0)], axis=1))

    def load():
        return tuple((m_scr[g], acc_scr[g]) for g in range(NSA_GROUPS))

    def store(state):
        for g in range(NSA_GROUPS):
            m_scr[g], acc_scr[g] = state[g]

    def scores(k0, width):
        return [_dot_nt(qx[g], kx_ref[pl.ds(k0, width), g * 2 * LANES:(g + 1) * 2 * LANES])
                for g in range(NSA_GROUPS)]

    def update(k0, width, diagonal, s_all, state):
        probs, m_all = [], []
        for g in range(NSA_GROUPS):
            s = s_all[g]
            if diagonal:
                t_row = q0 + lax.broadcasted_iota(jnp.int32, (Q_TILE, width), 0)
                key = k0 + lax.broadcasted_iota(jnp.int32, (Q_TILE, width), 1)
                s = s + jnp.tile(jnp.where(key <= t_row, 0.0, MASK_NEG), (NSA_HPG, 1))
            m_new = jnp.maximum(state[g][0], jnp.max(s, axis=-1, keepdims=True))
            probs.append(jnp.exp(s - jnp.tile(m_new, (1, width // LANES))).astype(BF16))
            m_all.append(m_new)
        out = []
        for g in range(NSA_GROUPS):
            m_old, acc = state[g]
            pv = _dot(probs[g], vx_ref[pl.ds(k0, width), g * LANES:(g + 1) * LANES])
            out.append((m_all[g], jnp.exp(m_old - m_all[g]) * acc + pv))
        return tuple(out)

    def step(k0, width, diagonal, state):
        return update(k0, width, diagonal, scores(k0, width), state)

    store(tuple((jnp.full((rows, LANES), MASK_NEG, F32), jnp.zeros((rows, LANES), F32))
                for _ in range(NSA_GROUPS)))

    def pair(j, _):
        k0 = pl.multiple_of(j * 2 * wide, 2 * wide)
        s_a, s_b = scores(k0, wide), scores(k0 + wide, wide)
        store(update(k0 + wide, wide, False, s_b, update(k0, wide, False, s_a, load())))
        return 0

    lax.fori_loop(0, n_wide // 2, pair, 0)

    @pl.when(n_wide % 2 == 1)
    def _():
        store(step(pl.multiple_of((n_wide - 1) * wide, wide), wide, False, load()))

    tail0 = pl.multiple_of(n_wide * wide, wide)
    tail_steps = (q0 + Q_TILE - tail0 + SEL_TAIL_STEP - 1) // SEL_TAIL_STEP
    for var in range(1, wide // SEL_TAIL_STEP + 1):
        @pl.when(tail_steps == var)
        def _():
            store(step(tail0, var * SEL_TAIL_STEP, True, load()))

    for g in range(NSA_GROUPS):
        acc = acc_scr[g]
        _store_heads(o_ref, gate_ref, 1, g, acc / acc[:, HEAD_DIM:HEAD_DIM + 1])


def _sel_attn(p16, p32, gate_blk, bias, batch, seq):
    nq = seq // Q_TILE
    tk = min(SEL_KEY_TILE, seq)
    kx_w = NSA_GROUPS * 2 * LANES
    vx_w = NSA_GROUPS * LANES
    return pl.pallas_call(
        functools.partial(_sel_attn_kernel, tk=tk),
        grid=(batch, nq),
        in_specs=[pl.BlockSpec((Q_TILE, NSA_HEADS * LANES), lambda b, i: (b * nq + i, 0)),
                  pl.BlockSpec((Q_TILE, LANES), lambda b, i: (b * nq + i, gate_blk)),
                  pl.BlockSpec((Q_TILE, bias.shape[1]), lambda b, i: (b * nq + i, 0)),
                  pl.BlockSpec((seq, kx_w), lambda b, i: (b, P16_KX // kx_w)),
                  pl.BlockSpec((seq, vx_w), lambda b, i: (b, P16_VX // vx_w))],
        out_specs=pl.BlockSpec((Q_TILE, NSA_HEADS * HEAD_DIM), lambda b, i: (b * nq + i, 0)),
        out_shape=jax.ShapeDtypeStruct((batch * seq, NSA_HEADS * HEAD_DIM), F32),
        scratch_shapes=[pltpu.VMEM((NSA_GROUPS, NSA_HPG * Q_TILE, LANES), F32),
                        pltpu.VMEM((NSA_GROUPS, NSA_HPG * Q_TILE, LANES), F32)],
        compiler_params=_cparams("parallel", "arbitrary"),
        name="nsa_sel_attn",
    )(p16, p32, bias, p16, p16)


def _window_attention(q_ref, gate_ref, kw_ref, vw_ref, o_ref, q0):
    seq = kw_ref.shape[0]
    span = min(WINDOW + Q_TILE, seq)
    start = pl.multiple_of(jnp.clip(q0 + Q_TILE - span, 0, seq - span), Q_TILE)
    t_row = q0 + lax.broadcasted_iota(jnp.int32, (Q_TILE, span), 0)
    pos = start + lax.broadcasted_iota(jnp.int32, (Q_TILE, span), 1)
    mask = jnp.tile(jnp.where((pos <= t_row) & (pos > t_row - WINDOW), 0.0, MASK_NEG), (NSA_HPG, 1))
    s_all = [_dot_nt(_stack_heads(q_ref, g), kw_ref[pl.ds(start, span), g * LANES:(g + 1) * LANES])
             for g in range(NSA_GROUPS)]
    p_all = []
    for g in range(NSA_GROUPS):
        s = s_all[g] + mask
        p_all.append(jnp.exp(s - jnp.max(s, axis=-1, keepdims=True)).astype(BF16))
    for g in range(NSA_GROUPS):
        acc = _dot(p_all[g], vw_ref[pl.ds(start, span), g * LANES:(g + 1) * LANES])
        _store_heads(o_ref, gate_ref, 2, g, acc / acc[:, HEAD_DIM:HEAD_DIM + 1])


def _spatial_gating_rows(u_ref, v_ref, g_ref, b_ref, w_ref, bexp_ref):
    c = SG_CHUNK
    tri = (lax.broadcasted_iota(jnp.int32, (c, c), 1) <= lax.broadcasted_iota(jnp.int32, (c, c), 0))
    lane = lax.broadcasted_iota(jnp.int32, (c, LANES), 1)
    wm = [jnp.where(tri, w_ref[g], 0.0).astype(BF16) for g in range(SG_GROUPS)]
    out = []
    for ci in range(u_ref.shape[0] // c):
        v = v_ref[ci * c:(ci + 1) * c, :]
        mu = jnp.mean(v, axis=-1, keepdims=True)
        d = v - mu
        var = jnp.mean(d * d, axis=-1, keepdims=True)
        vn = (d * lax.rsqrt(var + LN_EPS) * g_ref[...] + b_ref[...]).astype(BF16)
        mixed = []
        for pair in range(SG_GROUPS // 2):
            vp = vn[:, pair * LANES:(pair + 1) * LANES]
            mixed.append(jnp.where(lane < HEAD_DIM, _dot(wm[2 * pair], vp), _dot(wm[2 * pair + 1], vp)))
        out.append(u_ref[ci * c:(ci + 1) * c, :] * (jnp.concatenate(mixed, axis=1) + bexp_ref[...]))
    return jnp.concatenate(out, axis=0)


def _rwkv_kernel(r_ref, k_ref, v_ref, lo_ref, mu_r, mu_k, mu_v, mu_lo, w0_ref, w2_ref, a0_ref,
                 a2_ref, kk_ref, ka_ref, rk_ref, lng_ref, lnb_ref, seg_ref, o_ref,
                 state, prev_r, prev_k, prev_v, prev_lo, *, nch):
    c = RWKV_CHUNK
    rows = nch * c
    width = r_ref.shape[1]
    n_pairs = width // LANES

    @pl.when(pl.program_id(1) == 0)
    def _():
        state[...] = jnp.zeros_like(state)
        for ref in (prev_r, prev_k, prev_v, prev_lo):
            ref[...] = jnp.zeros_like(ref)

    def token_shift(x_ref, prev_ref, mu_ref):
        x = x_ref[...]
        first = lax.broadcasted_iota(jnp.int32, x.shape, 0) == 0
        prev = jnp.where(first, prev_ref[0:1, :], pltpu.roll(x, 1, 0))
        prev_ref[0:1, :] = x[rows - 1:rows, :]
        return x + (prev - x) * mu_ref[...]

    r = token_shift(r_ref, prev_r, mu_r)
    k = token_shift(k_ref, prev_k, mu_k)
    v = token_shift(v_ref, prev_v, mu_v)
    lo = token_shift(lo_ref, prev_lo, mu_lo)

    lane_lo = lax.broadcasted_iota(jnp.int32, lo.shape, 1)
    w_in = jnp.where(lane_lo < LORA, jnp.tanh(lo), 0.0).astype(BF16)
    a_in = jnp.where(lane_lo < LORA, 0.0, lo).astype(BF16)
    z = -(w0_ref[...] + _dot(w_in, w2_ref[...]))
    w = -(jnp.maximum(z, 0.0) + jnp.log(1.0 + jnp.exp(-jnp.abs(z)))) - 0.5
    a = _sigmoid(a0_ref[...] + _dot(a_in, a2_ref[...]))

    seg = seg_ref[...]
    kkf = k * kk_ref[...]
    kk = kkf / jnp.maximum(jnp.sqrt(_dot3_lhs(kkf * kkf, seg)), 1e-12)
    k = k * (1.0 + (a - 1.0) * ka_ref[...])
    logd = -jnp.exp(w)
    t_row = lax.broadcasted_iota(jnp.int32, (rows, rows), 0)
    t_col = lax.broadcasted_iota(jnp.int32, (rows, rows), 1)
    cum = (t_col <= t_row) & (t_col // c == t_row // c)
    cl = _dot3_rhs(jnp.where(cum, 1.0, 0.0).astype(BF16), logd)
    cl_end = cl.reshape(nch, c, width)[:, c - 1:c, :]
    e_neg = jnp.exp(-cl)
    e_end = jnp.exp(jnp.broadcast_to(cl_end, (nch, c, width)).reshape(rows, width) - cl)
    b = kk * a
    a_t = -kk * jnp.exp(cl - logd)
    b_t = b * e_neg
    k_t = k * e_neg
    r_t = r * jnp.exp(cl)
    b_e = b * e_end
    k_e = k * e_end
    p_end = jnp.exp(cl_end)

    lane = lax.broadcasted_iota(jnp.int32, (c, LANES), 1)
    row2 = lax.broadcasted_iota(jnp.int32, (2 * c, 2 * c), 0)
    col2 = lax.broadcasted_iota(jnp.int32, (2 * c, 2 * c), 1)
    same_head = (row2 // c) == (col2 // c)
    strict = same_head & (col2 < row2)
    incl = same_head & (col2 <= row2)
    eye = jnp.where(row2 == col2, 1.0, 0.0)
    zero16 = jnp.zeros((c, LANES), BF16)

    def stack(x, ci, pair):
        xp = x[ci * c:(ci + 1) * c, pair * LANES:(pair + 1) * LANES].astype(BF16)
        return jnp.concatenate([jnp.where(lane < HEAD_DIM, xp, zero16),
                                jnp.where(lane < HEAD_DIM, zero16, xp)], axis=0)

    def off(m):
        return (((row2 % (2 * m)) >= m) & ((col2 // (2 * m)) == (row2 // (2 * m)))
                & ((col2 % (2 * m)) < m))

    combos = [(ci, pair) for ci in range(nch) for pair in range(n_pairs)]
    a_st = {q: stack(a_t, *q) for q in combos}
    r_st = {q: stack(r_t, *q) for q in combos}
    v_st = {q: stack(v, *q) for q in combos}
    g = {q: _dot_nt(jnp.concatenate([a_st[q], r_st[q]], axis=0),
                    jnp.concatenate([stack(b_t, *q), stack(k_t, *q)], axis=0)) for q in combos}
    l_ab = {q: jnp.where(strict, g[q][0:2 * c, 0:2 * c], 0.0) for q in combos}
    inv = {q: eye + jnp.where(off(1), l_ab[q], 0.0) for q in combos}
    m = 2
    while m < c:
        inv16 = {q: inv[q].astype(BF16) for q in combos}
        x = {q: _dot(jnp.where(off(m), l_ab[q], 0.0).astype(BF16), inv16[q]) for q in combos}
        inv = {q: inv[q] + _dot(inv16[q], x[q].astype(BF16)) for q in combos}
        m *= 2
    lakv = {q: _dot(jnp.where(strict, g[q][0:2 * c, 2 * c:4 * c], 0.0).astype(BF16), v_st[q])
            for q in combos}
    wu = {q: _dot(inv[q].astype(BF16), jnp.concatenate([a_st[q], lakv[q].astype(BF16)], axis=1))
          for q in combos}
    m_rbk = {q: jnp.concatenate([jnp.where(incl, g[q][2 * c:4 * c, 0:2 * c], 0.0),
                                 jnp.where(incl, g[q][2 * c:4 * c, 2 * c:4 * c], 0.0)],
                                axis=1).astype(BF16) for q in combos}
    bke_st = {q: jnp.concatenate([stack(b_e, *q), stack(k_e, *q)], axis=0) for q in combos}

    st = [state[pair] for pair in range(n_pairs)]
    ys = [[None] * n_pairs for _ in range(nch)]
    for ci in range(nch):
        wr = {pair: _dot(jnp.concatenate([wu[ci, pair][:, 0:LANES].astype(BF16), r_st[ci, pair]],
                                         axis=0), st[pair].astype(BF16)) for pair in range(n_pairs)}
        for pair in range(n_pairs):
            q = (ci, pair)
            u_st = wr[pair][0:2 * c] + wu[q][:, LANES:2 * LANES]
            uv_st = jnp.concatenate([u_st.astype(BF16), v_st[q]], axis=0)
            y_st = wr[pair][2 * c:4 * c] + _dot(m_rbk[q], uv_st)
            decay_col = jnp.broadcast_to(p_end[ci, :, pair * LANES:(pair + 1) * LANES],
                                         (LANES, LANES)).T
            st[pair] = st[pair] * decay_col + _dot_tn(bke_st[q], uv_st)
            ys[ci][pair] = y_st[0:c] + y_st[c:2 * c]
    for pair in range(n_pairs):
        state[pair] = st[pair]
    y = jnp.concatenate([jnp.concatenate(yr, axis=1) for yr in ys], axis=0)

    seg_mean = (seg * (1.0 / HEAD_DIM)).astype(BF16)
    mean = _dot3_lhs(y, seg_mean)
    d = y - mean
    var = _dot3_lhs(d * d, seg_mean)
    yn = d * lax.rsqrt(var + GN_EPS) * lng_ref[...] + lnb_ref[...]
    bonus = _dot3_lhs(r * k * rk_ref[...], seg) * v
    o_ref[...] = yn + bonus


def _rwkv(p32, params, batch, seq, r_blk, k_blk, v_blk, lo_blk, nch=4):
    width = params["w0"].shape[1]
    c = nch * RWKV_CHUNK
    nc = seq // c
    full = lambda a: pl.BlockSpec(a.shape, lambda b, i: (0,) * a.ndim)
    names = ("mu_r", "mu_k", "mu_v", "mu_lo", "w0", "w2", "a0", "a2", "kk", "ka", "rk",
             "lng", "lnb", "seg")
    return pl.pallas_call(
        functools.partial(_rwkv_kernel, nch=nch),
        grid=(batch, nc),
        in_specs=[pl.BlockSpec((c, width), lambda b, i: (b * nc + i, r_blk)),
                  pl.BlockSpec((c, width), lambda b, i: (b * nc + i, k_blk)),
                  pl.BlockSpec((c, width), lambda b, i: (b * nc + i, v_blk)),
                  pl.BlockSpec((c, LANES), lambda b, i: (b * nc + i, lo_blk))]
                 + [full(params[n]) for n in names],
        out_specs=pl.BlockSpec((c, width), lambda b, i: (b * nc + i, 0)),
        out_shape=jax.ShapeDtypeStruct((batch * seq, width), F32),
        scratch_shapes=[pltpu.VMEM((width // LANES, LANES, LANES), F32),
                        pltpu.VMEM((8, width), F32), pltpu.VMEM((8, width), F32),
                        pltpu.VMEM((8, width), F32), pltpu.VMEM((8, LANES), F32)],
        compiler_params=_cparams("parallel", "arbitrary"),
        name="rwkv7",
    )(p32, p32, p32, p32, *[params[n] for n in names])


def _merge_kernel(ocw_ref, os_ref, su_ref, sv_ref, rk_ref, x_ref, p_ref, g_ref, wl_ref,
                  sgg_ref, sgb_ref, sgw_ref, bexp_ref,
                  wb0_ref, wb1_ref, wb2_ref, wo_ref, pleg_ref, wpg_ref, wpp_ref, fin_ref,
                  o_ref, *, last):
    d = x_ref.shape[1]
    half = rk_ref.shape[1]
    nzw = os_ref.shape[1]
    h = _rmsnorm(x_ref[...], g_ref[...]).astype(BF16)
    late = lambda c0, n: _dot(h, wl_ref[:, c0:c0 + n])
    y_nsa = ocw_ref[...] + os_ref[...]
    z0 = _dot((y_nsa * _silu(late(3 * d, nzw))).astype(BF16), wb0_ref[...])
    merged = _sigmoid(late(0, d)) * z0
    y_sg = _spatial_gating_rows(su_ref, sv_ref, sgg_ref, sgb_ref, sgw_ref, bexp_ref)
    z1 = _dot((y_sg * _silu(late(3 * d + nzw, half))).astype(BF16), wb1_ref[...])
    merged = merged + _sigmoid(late(d, d)) * z1
    z2 = _dot((rk_ref[...] * _silu(late(3 * d + nzw + half, half))).astype(BF16), wb2_ref[...])
    merged = merged + _sigmoid(late(2 * d, d)) * z2
    x1 = x_ref[...] + _dot(merged.astype(BF16), wo_ref[...])
    hp = _rmsnorm(x1, pleg_ref[...]).astype(BF16)
    x2 = x1 + _sigmoid(_dot(hp, wpg_ref[...])) * _dot(p_ref[...].astype(BF16), wpp_ref[...])
    o_ref[...] = _rmsnorm(x2, fin_ref[...]) if last else x2


def _merge(o_cw, o_s, p32, blocks, y_rk, x2, p2, consts, last, tm=256):
    t, d = x2.shape
    half = y_rk.shape[1]
    tok = lambda w, j=0: pl.BlockSpec((tm, w), lambda i, j=j: (i, j))
    full = lambda a: pl.BlockSpec(a.shape, lambda i: (0,) * a.ndim, pipeline_mode=pl.Buffered(1))
    names = ("g", "wl", "sgg", "sgb", "sgw", "bexp", "wb0", "wb1", "wb2", "wo", "pleg", "wpg",
             "wpp", "fin")
    return pl.pallas_call(
        functools.partial(_merge_kernel, last=last),
        grid=(t // tm,),
        in_specs=[tok(o_cw.shape[1]), tok(o_s.shape[1]), tok(half, blocks["su"]),
                  tok(half, blocks["sv"]), tok(half), tok(d), tok(p2.shape[1])]
                 + [full(consts[n]) for n in names],
        out_specs=tok(d),
        out_shape=jax.ShapeDtypeStruct((t, d), F32),
        compiler_params=_cparams("parallel"),
        name="merge_out_ple",
    )(o_cw, o_s, p32, p32, y_rk, x2, p2, *[consts[n] for n in names])


def _pad_cols(w, n):
    return jnp.pad(w, ((0, 0), (0, n - w.shape[1])))


def _layer_layout(d_model):
    half = d_model // 2
    kvw = NSA_GROUPS * HEAD_DIM
    sizes = dict(nq=half, nkv=6 * kvw, ngate=3 * NSA_HEADS, nz=half, su=half, sv=half, sz=half,
                 rs=3 * half + 2 * LORA, rz=half, mg=3 * d_model)
    off, out = 0, {}
    for name, n in sizes.items():
        out[name] = (off, n)
        off += n
    return out


def _prep_proj_weights(w_in, d_model):
    lay = _layer_layout(d_model)
    half = d_model // 2
    kvw = NSA_GROUPS * HEAD_DIM
    col = lambda name, a=0, n=None: w_in[:, lay[name][0] + a: lay[name][0] + a + (n or lay[name][1] - a)]
    heads = lambda w: jnp.concatenate(
        [_pad_cols(w[:, h * HEAD_DIM:(h + 1) * HEAD_DIM], LANES) for h in range(w.shape[1] // HEAD_DIM)], axis=1)
    w32 = jnp.concatenate([
        col("su"), col("sv"), col("rs", 0, half), col("rs", half, half), col("rs", 2 * half, half),
        col("rs", 3 * half, 2 * LORA), _pad_cols(col("ngate"), LANES)], axis=1)
    w_kcvc = col("nkv", 0, 2 * kvw)
    blocks, o = {}, 0
    for name in ("su", "sv", "rr", "rk", "rv"):
        blocks[name] = o // half
        o += half
    blocks["lora"] = o // LANES
    o += LANES
    blocks["gate"] = o // LANES
    w_late = jnp.concatenate([col("mg"), col("nz"), col("sz"), col("rz")], axis=1)
    zero = jnp.zeros((w_in.shape[0], LANES), w_in.dtype)
    grp = lambda w, g: _pad_cols(w[:, g * HEAD_DIM:(g + 1) * HEAD_DIM], LANES)
    ks, vs = col("nkv", 2 * kvw, kvw), col("nkv", 3 * kvw, kvw)
    kw, vw = col("nkv", 4 * kvw, kvw), col("nkv", 5 * kvw, kvw)
    w16 = jnp.concatenate(
        [heads(col("nq"))]
        + [x for g in range(NSA_GROUPS) for x in (grp(ks, g), zero)]
        + [grp(vs, g) for g in range(NSA_GROUPS)]
        + [grp(kw, g) for g in range(NSA_GROUPS)]
        + [grp(vw, g) for g in range(NSA_GROUPS)], axis=1)
    return w32.astype(BF16), w_kcvc.astype(BF16), w16.astype(BF16), w_late.astype(BF16), blocks


def _prep_compress_weights(cmp_w1, cmp_w2, cmp_pe):
    half_blk = CMP_BLOCK // 2
    w1r = cmp_w1.reshape(2, 2, half_blk, HEAD_DIM, CMP_HIDDEN)
    eye = jnp.eye(NSA_GROUPS, dtype=cmp_w1.dtype)
    eye_kv = jnp.eye(2, dtype=cmp_w1.dtype)
    big = jnp.einsum('kaldh,kK,gG->alkgdKGh', w1r, eye_kv, eye)
    big = big.reshape(2, half_blk * 2 * NSA_GROUPS * HEAD_DIM, 2 * NSA_GROUPS * CMP_HIDDEN)
    pe = cmp_pe.reshape(2, 2, half_blk, HEAD_DIM)
    pe2 = jnp.broadcast_to(pe.transpose(1, 2, 0, 3)[:, :, :, None, :],
                           (2, half_blk, 2, NSA_GROUPS, HEAD_DIM)).reshape(2, -1)
    w2p = jnp.pad(cmp_w2, ((0, 0), (0, 0), (0, LANES - HEAD_DIM)))
    return pe2, big[0].astype(BF16), big[1].astype(BF16), w2p.astype(BF16)


def _cover_t(seq):
    n_grp = seq // CMP_STRIDE
    n_slc = seq // SEL_BLOCK
    cmp_start = jnp.arange(n_grp) * CMP_STRIDE
    slc_start = jnp.arange(n_slc) * SEL_BLOCK
    cover = ((cmp_start[None, :] <= slc_start[:, None] + SEL_BLOCK - 1)
             & (cmp_start[None, :] + CMP_BLOCK - 1 >= slc_start[:, None]))
    return jnp.pad(cover, ((0, LANES - n_slc), (0, 0))).astype(BF16)


def kernel(x, p, norm_g, w_in, cmp_w1, cmp_w2, cmp_pe, sg_ln_g, sg_ln_b, sg_w, sg_b, rk_mu, rk_w0,
           rk_w2, rk_a0, rk_a2, rk_kk, rk_ka, rk_rk, rk_lnx_g, rk_lnx_b, w_branch, w_o, ple_norm_g,
           w_ple_gate, w_ple_proj, final_norm_g):
    batch, seq, d = x.shape
    depth = w_in.shape[0]
    half = d // 2
    t = batch * seq
    assert seq % (2 * SEL_KEY_TILE) == 0 and (seq // CMP_STRIDE) % CMP_WIDTH_STEP == 0
    assert seq >= WINDOW + Q_TILE and seq // SEL_BLOCK <= LANES
    x2 = x.reshape(t, d)
    cov_t = _cover_t(seq)
    seg = (jnp.arange(half)[:, None] // HEAD_DIM == jnp.arange(half)[None, :] // HEAD_DIM).astype(BF16)
    row = lambda v: v.reshape(1, -1)
    for i in range(depth):
        w32, w_kcvc, w16, w_late, blk = _prep_proj_weights(w_in[i], d)
        g = row(norm_g[i])
        p32, kcvc, p16 = _proj(x2, g, w32, w_kcvc, w16, seq)

        pe2, wa, wb, w2p = _prep_compress_weights(cmp_w1[i], cmp_w2[i], cmp_pe[i])
        xg = kcvc.reshape(batch, seq // CMP_STRIDE, CMP_STRIDE * 2 * NSA_GROUPS * HEAD_DIM)
        kc, vc = _compress(xg, pe2, wa, wb, w2p)
        o_cw, bias = _cmp_select(p16, p32, blk["gate"], kc, vc, cov_t, batch, seq)
        o_s = _sel_attn(p16, p32, blk["gate"], bias, batch, seq)

        mu = rk_mu[i]
        rk_params = dict(
            mu_r=row(mu[0:half]), mu_k=row(mu[half:2 * half]), mu_v=row(mu[2 * half:3 * half]),
            mu_lo=row(mu[3 * half:]), w0=row(rk_w0[i]),
            w2=jnp.pad(rk_w2[i], ((0, LANES - LORA), (0, 0))).astype(BF16), a0=row(rk_a0[i]),
            a2=jnp.pad(rk_a2[i], ((LANES - LORA, 0), (0, 0))).astype(BF16),
            kk=row(rk_kk[i]), ka=row(rk_ka[i]), rk=row(rk_rk[i]), lng=row(rk_lnx_g[i]),
            lnb=row(rk_lnx_b[i]), seg=seg)
        y_rk = _rwkv(p32, rk_params, batch, seq, blk["rr"], blk["rk"], blk["rv"], blk["lora"])

        consts = dict(g=g, wl=w_late, sgg=row(sg_ln_g[i]), sgb=row(sg_ln_b[i]), sgw=sg_w[i],
                      bexp=jnp.repeat(sg_b[i].T, half // SG_GROUPS, axis=1),
                      wb0=w_branch[i, 0].astype(BF16), wb1=w_branch[i, 1].astype(BF16),
                      wb2=w_branch[i, 2].astype(BF16), wo=w_o[i].astype(BF16),
                      pleg=row(ple_norm_g[i]), wpg=w_ple_gate[i].astype(BF16),
                      wpp=w_ple_proj[i].astype(BF16), fin=row(final_norm_g))
        x2 = _merge(o_cw, o_s, p32, blk, y_rk, x2, p[i].reshape(t, -1), consts,
                    last=(i == depth - 1))
    return x2.reshape(batch, seq, d)
```

```python
import functools

import jax
import jax.numpy as jnp
from jax import lax
from jax.experimental import pallas as pl
from jax.experimental.pallas import tpu as pltpu

F32 = jnp.float32
BF16 = jnp.bfloat16

HEAD_DIM = 64
LANES = 128
NSA_HEADS = 8
NSA_GROUPS = 2
NSA_HPG = NSA_HEADS // NSA_GROUPS
CMP_BLOCK = 32
CMP_STRIDE = 16
CMP_HIDDEN = 128
SEL_BLOCK = 64
SEL_TOPK = 16
WINDOW = 512
Q_TILE = 128
SEL_KEY_TILE = 512
SEL_TAIL_STEP = 256
SG_GROUPS = 8
SG_CHUNK = 128
RWKV_HEADS = 8
RWKV_CHUNK = 64
LORA = 64
NORM_EPS = 1e-6
LN_EPS = 1e-5
GN_EPS = 64e-5
MASK_NEG = -1e30
FORCE_BONUS = 1e4
VMEM_LIMIT = 56 * 1024 * 1024


def _cparams(*sem):
    return pltpu.CompilerParams(dimension_semantics=sem, vmem_limit_bytes=VMEM_LIMIT)


def _dot(a, b, precision=None):
    return jnp.dot(a, b, preferred_element_type=F32, precision=precision)


def _dot_nt(a, b, precision=None):
    return lax.dot_general(a, b, (((1,), (1,)), ((), ())), preferred_element_type=F32,
                           precision=precision)


def _dot_tn(a, b, precision=None):
    return lax.dot_general(a, b, (((0,), (0,)), ((), ())), preferred_element_type=F32,
                           precision=precision)


def _split3(x):
    hi = x.astype(BF16)
    r1 = x - hi.astype(F32)
    mid = r1.astype(BF16)
    lo = (r1 - mid.astype(F32)).astype(BF16)
    return hi, mid, lo


def _dot3_lhs(x, w):
    hi, mid, lo = _split3(x)
    return _dot(hi, w) + _dot(mid, w) + _dot(lo, w)


def _dot3_rhs(w, x):
    hi, mid, lo = _split3(x)
    return _dot(w, hi) + _dot(w, mid) + _dot(w, lo)


def _sigmoid(x):
    return 1.0 / (1.0 + jnp.exp(-x))


def _silu(x):
    return x * _sigmoid(x)


def _rmsnorm(x, g):
    return x * lax.rsqrt(jnp.mean(x * x, axis=-1, keepdims=True) + NORM_EPS) * g


P16_Q = 0
P16_KX = NSA_HEADS * LANES
P16_VX = P16_KX + NSA_GROUPS * 2 * LANES
P16_KW = P16_VX + NSA_GROUPS * LANES
P16_VW = P16_KW + NSA_GROUPS * LANES
P16_N = P16_VW + NSA_GROUPS * LANES


PROJ_CHUNK = 2 * LANES


def _proj_kernel(x_ref, g_ref, w32_ref, wkc_ref, w16_ref, o32_ref, okc_ref, ovc_ref, o16_ref, *,
                 seq, tm):
    h = _rmsnorm(x_ref[...], g_ref[...]).astype(BF16)
    for c0 in range(0, o32_ref.shape[1], PROJ_CHUNK):
        o32_ref[:, c0:c0 + PROJ_CHUNK] = _dot(h, w32_ref[:, c0:c0 + PROJ_CHUNK])
    kcvc = _dot(h, wkc_ref[...])
    okc_ref[...] = kcvc[:, 0:LANES]
    ovc_ref[...] = kcvc[:, LANES:2 * LANES]
    pos = (pl.program_id(0) * tm) % seq + lax.broadcasted_iota(jnp.int32, (tm, LANES), 0)
    lane = lax.broadcasted_iota(jnp.int32, (tm, LANES), 1)
    onehot_blk = jnp.where(pos // SEL_BLOCK == lane, 1.0, 0.0)
    ones_col = jnp.where(lane == HEAD_DIM, 1.0, 0.0)
    zeros = jnp.zeros((tm, LANES), F32)

    def constant(c0):
        if P16_KX <= c0 < P16_VX and ((c0 - P16_KX) // LANES) % 2 == 1:
            return onehot_blk
        if c0 >= P16_VX and not (P16_KW <= c0 < P16_VW):
            return ones_col
        return zeros

    for c0 in range(0, P16_N, PROJ_CHUNK):
        y = _dot(h, w16_ref[:, c0:c0 + PROJ_CHUNK])
        consts = [constant(c0 + i * LANES) for i in range(PROJ_CHUNK // LANES)]
        if any(c is not zeros for c in consts):
            y = y + jnp.concatenate(consts, axis=1)
        o16_ref[:, c0:c0 + PROJ_CHUNK] = y.astype(BF16)


def _proj(x2, g, w32, wkc, w16, seq, tm=512):
    t, d = x2.shape
    n32 = w32.shape[1]
    assert wkc.shape[1] == 2 * LANES
    const = lambda a: pl.BlockSpec(a.shape, lambda i: (0, 0), pipeline_mode=pl.Buffered(1))
    return pl.pallas_call(
        functools.partial(_proj_kernel, seq=seq, tm=tm),
        grid=(t // tm,),
        in_specs=[pl.BlockSpec((tm, d), lambda i: (i, 0)), const(g), const(w32), const(wkc),
                  const(w16)],
        out_specs=[pl.BlockSpec((tm, n32), lambda i: (i, 0)),
                   pl.BlockSpec((tm, LANES), lambda i: (i, 0)),
                   pl.BlockSpec((tm, LANES), lambda i: (i, 0)),
                   pl.BlockSpec((tm, P16_N), lambda i: (i, 0))],
        out_shape=[jax.ShapeDtypeStruct((t, n32), F32), jax.ShapeDtypeStruct((t, LANES), F32),
                   jax.ShapeDtypeStruct((t, LANES), F32), jax.ShapeDtypeStruct((t, P16_N), BF16)],
        compiler_params=_cparams("parallel"),
        name="proj",
    )(x2, g, w32, wkc, w16)


def _compress_kernel(xk_ref, xv_ref, pe_ref, wa_ref, wb_ref, w2_ref, k_ref, v_ref):
    n_grp = xk_ref.shape[1] // CMP_STRIDE
    x = jnp.concatenate([ref[0, pl.ds(l, n_grp, stride=CMP_STRIDE), :]
                         for l in range(CMP_STRIDE) for ref in (xk_ref, xv_ref)], axis=1)
    a = _dot((x + pe_ref[0:1, :]).astype(BF16), wa_ref[...])
    b = _dot((x + pe_ref[1:2, :]).astype(BF16), wb_ref[...])
    hid = _silu(a + pltpu.roll(b, n_grp - 1, 0)).astype(BF16)
    row = lax.broadcasted_iota(jnp.int32, (n_grp, LANES), 0)
    lane = lax.broadcasted_iota(jnp.int32, (n_grp, LANES), 1)
    live = row < n_grp - 1
    for kv, out_ref in enumerate((k_ref, v_ref)):
        for g in range(NSA_GROUPS):
            c0 = (kv * NSA_GROUPS + g) * CMP_HIDDEN
            y = _dot(hid[:, c0:c0 + CMP_HIDDEN], w2_ref[kv])
            if kv == 1:
                y = y + jnp.where(lane == HEAD_DIM, 1.0, 0.0)
            out_ref[0, :, g * LANES:(g + 1) * LANES] = jnp.where(live, y, 0.0).astype(BF16)


def _compress(xk, xv, pe2, wa, wb, w2p):
    b, seq, width = xk.shape
    n_grp = seq // CMP_STRIDE
    out = jax.ShapeDtypeStruct((b, n_grp, NSA_GROUPS * LANES), BF16)
    return pl.pallas_call(
        _compress_kernel,
        grid=(b,),
        in_specs=[pl.BlockSpec((1, seq, width), lambda i: (i, 0, 0)),
                  pl.BlockSpec((1, seq, width), lambda i: (i, 0, 0)),
                  pl.BlockSpec(pe2.shape, lambda i: (0, 0)),
                  pl.BlockSpec(wa.shape, lambda i: (0, 0)),
                  pl.BlockSpec(wb.shape, lambda i: (0, 0)),
                  pl.BlockSpec(w2p.shape, lambda i: (0, 0, 0))],
        out_specs=[pl.BlockSpec((1, n_grp, NSA_GROUPS * LANES), lambda i: (i, 0, 0))] * 2,
        out_shape=[out, out],
        compiler_params=_cparams("parallel"),
        name="nsa_compress",
    )(xk, xv, pe2, wa, wb, w2p)


def _stack_heads(q_ref, g):
    parts = [q_ref[:, (g * NSA_HPG + h) * LANES:(g * NSA_HPG + h + 1) * LANES]
             for h in range(NSA_HPG)]
    return jnp.concatenate(parts, axis=0) * jnp.asarray(HEAD_DIM ** -0.5, BF16)


def _store_heads(o_ref, gate_ref, branch, g, o, accumulate=False):
    lane = lax.broadcasted_iota(jnp.int32, (Q_TILE, LANES), 1)
    gate = _sigmoid(gate_ref[...])

    def scaled(h):
        col = (g * NSA_HPG + h) * 3 + branch
        return jnp.broadcast_to(gate[:, col:col + 1], (Q_TILE, LANES)) * o[h * Q_TILE:(h + 1) * Q_TILE]

    for j in range(NSA_HPG // 2):
        tile = jnp.where(lane < HEAD_DIM, scaled(2 * j), pltpu.roll(scaled(2 * j + 1), HEAD_DIM, 1))
        c0 = (g * NSA_HPG // 2 + j) * LANES
        o_ref[:, c0:c0 + LANES] = o_ref[:, c0:c0 + LANES] + tile if accumulate else tile


def _cmp_select_kernel(q_ref, gate_ref, kc_ref, vc_ref, cov_ref, kw_ref, vw_ref, o_ref, bias_ref,
                       imp_ref, *, top_k):
    n_cmp = kc_ref.shape[1]
    n_slc = cov_ref.shape[0]
    q0 = pl.program_id(1) * Q_TILE
    rows = NSA_HPG * Q_TILE
    sel_shape = (n_slc, NSA_GROUPS * Q_TILE)
    j_idx = lax.broadcasted_iota(jnp.int32, sel_shape, 0)
    t_lane = q0 + lax.broadcasted_iota(jnp.int32, sel_shape, 1) % Q_TILE
    t_blk = t_lane // SEL_BLOCK
    forced = (j_idx == 0) | (j_idx == t_blk) | (j_idx == t_blk - 1)
    causal_blk = j_idx <= t_blk

    def attend(width):
        t_row = q0 + lax.broadcasted_iota(jnp.int32, (Q_TILE, width), 0)
        cmp_end = lax.broadcasted_iota(jnp.int32, (Q_TILE, width), 1) * CMP_STRIDE + CMP_BLOCK - 1
        mask = jnp.tile(jnp.where(cmp_end <= t_row, 0.0, MASK_NEG), (NSA_HPG, 1))
        any_valid = jnp.tile(jnp.where(t_row[:, 0:1] >= CMP_BLOCK - 1, 1.0, 0.0), (NSA_HPG, 1))
        s_all = [_dot_nt(_stack_heads(q_ref, g), kc_ref[0, 0:width, g * LANES:(g + 1) * LANES])
                 for g in range(NSA_GROUPS)]
        for g in range(NSA_GROUPS):
            s = s_all[g] + mask
            p = jnp.exp(s - jnp.max(s, axis=-1, keepdims=True))
            l = jnp.sum(p, axis=-1, keepdims=True)
            p = p * (any_valid / l)
            o = _dot(p.astype(BF16), vc_ref[0, 0:width, g * LANES:(g + 1) * LANES])
            _store_heads(o_ref, gate_ref, 0, g, o, accumulate=True)
            p_sum = p[0:Q_TILE]
            for h in range(1, NSA_HPG):
                p_sum = p_sum + p[h * Q_TILE:(h + 1) * Q_TILE]
            hi, mid, lo = _split3(p_sum)
            cov = cov_ref[:, 0:width]
            imp_ref[:, g * Q_TILE:(g + 1) * Q_TILE] = (_dot_nt(cov, hi) + _dot_nt(cov, mid)
                                                       + _dot_nt(cov, lo))

    _window_attention(q_ref, gate_ref, kw_ref, vw_ref, o_ref, q0)
    attend(n_cmp)

    score = jnp.where(causal_blk, jnp.where(forced, FORCE_BONUS, imp_ref[...]), -FORCE_BONUS)

    j_grp = j_idx[:, 0:Q_TILE]

    def pick(_, carry):
        out = []
        for sc in carry:
            best = jnp.max(sc, axis=0, keepdims=True)
            first = jnp.min(jnp.where(sc == best, j_grp, n_slc), axis=0, keepdims=True)
            out.append(jnp.where(j_grp == first, -jnp.inf, sc))
        return tuple(out)

    init = tuple(score[:, g * Q_TILE:(g + 1) * Q_TILE] for g in range(NSA_GROUPS))
    picked = lax.fori_loop(0, top_k, pick, init)
    for g in range(NSA_GROUPS):
        chosen = (picked[g] == -jnp.inf) & causal_blk[:, g * Q_TILE:(g + 1) * Q_TILE]
        bias_ref[:, g * n_slc:(g + 1) * n_slc] = jnp.where(chosen, 0.0, MASK_NEG).T.astype(BF16)


def _cmp_select(p16, p32, gate_blk, kc, vc, cov_t, batch, seq):
    n_cmp = kc.shape[1]
    n_slc = cov_t.shape[0]
    nq = seq // Q_TILE
    w = NSA_GROUPS * LANES
    return pl.pallas_call(
        functools.partial(_cmp_select_kernel, top_k=min(SEL_TOPK, seq // SEL_BLOCK)),
        grid=(batch, nq),
        in_specs=[pl.BlockSpec((Q_TILE, NSA_HEADS * LANES), lambda b, i: (b * nq + i, 0)),
                  pl.BlockSpec((Q_TILE, LANES), lambda b, i: (b * nq + i, gate_blk)),
                  pl.BlockSpec((1, n_cmp, w), lambda b, i: (b, 0, 0)),
                  pl.BlockSpec((1, n_cmp, w), lambda b, i: (b, 0, 0)),
                  pl.BlockSpec(cov_t.shape, lambda b, i: (0, 0)),
                  pl.BlockSpec((seq, w), lambda b, i: (b, P16_KW // w)),
                  pl.BlockSpec((seq, w), lambda b, i: (b, P16_VW // w))],
        out_specs=[pl.BlockSpec((Q_TILE, NSA_HEADS * HEAD_DIM), lambda b, i: (b * nq + i, 0)),
                   pl.BlockSpec((Q_TILE, NSA_GROUPS * n_slc), lambda b, i: (b * nq + i, 0))],
        out_shape=[jax.ShapeDtypeStruct((batch * seq, NSA_HEADS * HEAD_DIM), F32),
                   jax.ShapeDtypeStruct((batch * seq, NSA_GROUPS * n_slc), BF16)],
        scratch_shapes=[pltpu.VMEM((n_slc, NSA_GROUPS * Q_TILE), F32)],
        compiler_params=_cparams("parallel", "arbitrary"),
        name="nsa_cmp_select",
    )(p16, p32, kc, vc, cov_t, p16, p16)


def _sel_attn_kernel(q_ref, gate_ref, bias_ref, kx_ref, vx_ref, o_ref, m_scr, acc_scr, *, tk):
    n_slc = bias_ref.shape[1] // NSA_GROUPS
    q0 = pl.program_id(1) * Q_TILE
    rows = NSA_HPG * Q_TILE
    wide = 2 * tk
    n_wide = q0 // wide
    qx = []
    for g in range(NSA_GROUPS):
        bias = bias_ref[:, g * n_slc:(g + 1) * n_slc]
        qx.append(jnp.concatenate([_stack_heads(q_ref, g),
                                   jnp.concatenate([bias] * NSA_HPG, axis=0)], axis=1))

    def load():
        return tuple((m_scr[g], acc_scr[g]) for g in range(NSA_GROUPS))

    def store(state):
        for g in range(NSA_GROUPS):
            m_scr[g], acc_scr[g] = state[g]

    def scores(k0, width):
        return [_dot_nt(qx[g], kx_ref[pl.ds(k0, width), g * 2 * LANES:(g + 1) * 2 * LANES])
                for g in range(NSA_GROUPS)]

    def update(k0, width, diagonal, s_all, state):
        probs, m_all = [], []
        for g in range(NSA_GROUPS):
            s = s_all[g]
            if diagonal:
                t_row = q0 + lax.broadcasted_iota(jnp.int32, (Q_TILE, width), 0)
                key = k0 + lax.broadcasted_iota(jnp.int32, (Q_TILE, width), 1)
                s = s + jnp.tile(jnp.where(key <= t_row, 0.0, MASK_NEG), (NSA_HPG, 1))
            m_new = jnp.maximum(state[g][0], jnp.max(s, axis=-1, keepdims=True))
            probs.append(jnp.exp(s - jnp.tile(m_new, (1, width // LANES))).astype(BF16))
            m_all.append(m_new)
        out = []
        for g in range(NSA_GROUPS):
            m_old, acc = state[g]
            pv = _dot(probs[g], vx_ref[pl.ds(k0, width), g * LANES:(g + 1) * LANES])
            out.append((m_all[g], jnp.exp(m_old - m_all[g]) * acc + pv))
        return tuple(out)

    def step(k0, width, diagonal, state):
        return update(k0, width, diagonal, scores(k0, width), state)

    store(tuple((jnp.full((rows, LANES), MASK_NEG, F32), jnp.zeros((rows, LANES), F32))
                for _ in range(NSA_GROUPS)))

    def pair(j, _):
        k0 = pl.multiple_of(j * 2 * wide, 2 * wide)
        s_a, s_b = scores(k0, wide), scores(k0 + wide, wide)
        store(update(k0 + wide, wide, False, s_b, update(k0, wide, False, s_a, load())))
        return 0

    lax.fori_loop(0, n_wide // 2, pair, 0)

    @pl.when(n_wide % 2 == 1)
    def _():
        store(step(pl.multiple_of((n_wide - 1) * wide, wide), wide, False, load()))

    tail0 = pl.multiple_of(n_wide * wide, wide)
    tail_steps = (q0 + Q_TILE - tail0 + SEL_TAIL_STEP - 1) // SEL_TAIL_STEP
    for var in range(1, wide // SEL_TAIL_STEP + 1):
        @pl.when(tail_steps == var)
        def _():
            store(step(tail0, var * SEL_TAIL_STEP, True, load()))

    for g in range(NSA_GROUPS):
        acc = acc_scr[g]
        _store_heads(o_ref, gate_ref, 1, g, acc / acc[:, HEAD_DIM:HEAD_DIM + 1])


def _sel_attn(p16, p32, gate_blk, bias, batch, seq):
    nq = seq // Q_TILE
    tk = min(SEL_KEY_TILE, seq)
    kx_w = NSA_GROUPS * 2 * LANES
    vx_w = NSA_GROUPS * LANES
    return pl.pallas_call(
        functools.partial(_sel_attn_kernel, tk=tk),
        grid=(batch, nq),
        in_specs=[pl.BlockSpec((Q_TILE, NSA_HEADS * LANES), lambda b, i: (b * nq + i, 0)),
                  pl.BlockSpec((Q_TILE, LANES), lambda b, i: (b * nq + i, gate_blk)),
                  pl.BlockSpec((Q_TILE, bias.shape[1]), lambda b, i: (b * nq + i, 0)),
                  pl.BlockSpec((seq, kx_w), lambda b, i: (b, P16_KX // kx_w)),
                  pl.BlockSpec((seq, vx_w), lambda b, i: (b, P16_VX // vx_w))],
        out_specs=pl.BlockSpec((Q_TILE, NSA_HEADS * HEAD_DIM), lambda b, i: (b * nq + i, 0)),
        out_shape=jax.ShapeDtypeStruct((batch * seq, NSA_HEADS * HEAD_DIM), F32),
        scratch_shapes=[pltpu.VMEM((NSA_GROUPS, NSA_HPG * Q_TILE, LANES), F32),
                        pltpu.VMEM((NSA_GROUPS, NSA_HPG * Q_TILE, LANES), F32)],
        compiler_params=_cparams("parallel", "arbitrary"),
        name="nsa_sel_attn",
    )(p16, p32, bias, p16, p16)


def _window_attention(q_ref, gate_ref, kw_ref, vw_ref, o_ref, q0):
    seq = kw_ref.shape[0]
    span = min(WINDOW + Q_TILE, seq)
    start = pl.multiple_of(jnp.clip(q0 + Q_TILE - span, 0, seq - span), Q_TILE)
    t_row = q0 + lax.broadcasted_iota(jnp.int32, (Q_TILE, span), 0)
    pos = start + lax.broadcasted_iota(jnp.int32, (Q_TILE, span), 1)
    mask = jnp.tile(jnp.where((pos <= t_row) & (pos > t_row - WINDOW), 0.0, MASK_NEG), (NSA_HPG, 1))
    s_all = [_dot_nt(_stack_heads(q_ref, g), kw_ref[pl.ds(start, span), g * LANES:(g + 1) * LANES])
             for g in range(NSA_GROUPS)]
    p_all = []
    for g in range(NSA_GROUPS):
        s = s_all[g] + mask
        p_all.append(jnp.exp(s - jnp.max(s, axis=-1, keepdims=True)).astype(BF16))
    for g in range(NSA_GROUPS):
        acc = _dot(p_all[g], vw_ref[pl.ds(start, span), g * LANES:(g + 1) * LANES])
        _store_heads(o_ref, gate_ref, 2, g, acc / acc[:, HEAD_DIM:HEAD_DIM + 1])


def _spatial_gating_rows(u_ref, v_ref, g_ref, b_ref, w_ref, bexp_ref):
    c = SG_CHUNK
    tri = (lax.broadcasted_iota(jnp.int32, (c, c), 1) <= lax.broadcasted_iota(jnp.int32, (c, c), 0))
    lane = lax.broadcasted_iota(jnp.int32, (c, LANES), 1)
    wm = [jnp.where(tri, w_ref[g], 0.0).astype(BF16) for g in range(SG_GROUPS)]
    out = []
    for ci in range(u_ref.shape[0] // c):
        v = v_ref[ci * c:(ci + 1) * c, :]
        mu = jnp.mean(v, axis=-1, keepdims=True)
        d = v - mu
        var = jnp.mean(d * d, axis=-1, keepdims=True)
        vn = (d * lax.rsqrt(var + LN_EPS) * g_ref[...] + b_ref[...]).astype(BF16)
        mixed = []
        for pair in range(SG_GROUPS // 2):
            vp = vn[:, pair * LANES:(pair + 1) * LANES]
            mixed.append(jnp.where(lane < HEAD_DIM, _dot(wm[2 * pair], vp), _dot(wm[2 * pair + 1], vp)))
        out.append(u_ref[ci * c:(ci + 1) * c, :] * (jnp.concatenate(mixed, axis=1) + bexp_ref[...]))
    return jnp.concatenate(out, axis=0)


def _rwkv_kernel(r_ref, k_ref, v_ref, lo_ref, mu_r, mu_k, mu_v, mu_lo, w0_ref, w2_ref, a0_ref,
                 a2_ref, kk_ref, ka_ref, rk_ref, lng_ref, lnb_ref, seg_ref, o_ref,
                 state, prev_r, prev_k, prev_v, prev_lo, *, nch):
    c = RWKV_CHUNK
    rows = nch * c
    width = r_ref.shape[1]
    n_pairs = width // LANES

    @pl.when(pl.program_id(1) == 0)
    def _():
        state[...] = jnp.zeros_like(state)
        for ref in (prev_r, prev_k, prev_v, prev_lo):
            ref[...] = jnp.zeros_like(ref)

    def token_shift(x_ref, prev_ref, mu_ref):
        x = x_ref[...]
        first = lax.broadcasted_iota(jnp.int32, x.shape, 0) == 0
        prev = jnp.where(first, prev_ref[0:1, :], pltpu.roll(x, 1, 0))
        prev_ref[0:1, :] = x[rows - 1:rows, :]
        return x + (prev - x) * mu_ref[...]

    r = token_shift(r_ref, prev_r, mu_r)
    k = token_shift(k_ref, prev_k, mu_k)
    v = token_shift(v_ref, prev_v, mu_v)
    lo = token_shift(lo_ref, prev_lo, mu_lo)

    lane_lo = lax.broadcasted_iota(jnp.int32, lo.shape, 1)
    w_in = jnp.where(lane_lo < LORA, jnp.tanh(lo), 0.0).astype(BF16)
    a_in = jnp.where(lane_lo < LORA, 0.0, lo).astype(BF16)
    z = -(w0_ref[...] + _dot(w_in, w2_ref[...]))
    w = -(jnp.maximum(z, 0.0) + jnp.log(1.0 + jnp.exp(-jnp.abs(z)))) - 0.5
    a = _sigmoid(a0_ref[...] + _dot(a_in, a2_ref[...]))

    seg = seg_ref[...]
    kkf = k * kk_ref[...]
    kk = kkf / jnp.maximum(jnp.sqrt(_dot3_lhs(kkf * kkf, seg)), 1e-12)
    k = k * (1.0 + (a - 1.0) * ka_ref[...])
    logd = -jnp.exp(w)
    t_row = lax.broadcasted_iota(jnp.int32, (rows, rows), 0)
    t_col = lax.broadcasted_iota(jnp.int32, (rows, rows), 1)
    cum = (t_col <= t_row) & (t_col // c == t_row // c)
    cl = _dot3_rhs(jnp.where(cum, 1.0, 0.0).astype(BF16), logd)
    cl_end = cl.reshape(nch, c, width)[:, c - 1:c, :]
    e_neg = jnp.exp(-cl)
    e_end = jnp.exp(jnp.broadcast_to(cl_end, (nch, c, width)).reshape(rows, width) - cl)
    b = kk * a
    a_t = -kk * jnp.exp(cl - logd)
    b_t = b * e_neg
    k_t = k * e_neg
    r_t = r * jnp.exp(cl)
    b_e = b * e_end
    k_e = k * e_end
    p_end = jnp.exp(cl_end)

    lane = lax.broadcasted_iota(jnp.int32, (c, LANES), 1)
    row2 = lax.broadcasted_iota(jnp.int32, (2 * c, 2 * c), 0)
    col2 = lax.broadcasted_iota(jnp.int32, (2 * c, 2 * c), 1)
    same_head = (row2 // c) == (col2 // c)
    strict = same_head & (col2 < row2)
    incl = same_head & (col2 <= row2)
    eye = jnp.where(row2 == col2, 1.0, 0.0)
    zero16 = jnp.zeros((c, LANES), BF16)

    def stack(x, ci, pair):
        xp = x[ci * c:(ci + 1) * c, pair * LANES:(pair + 1) * LANES].astype(BF16)
        return jnp.concatenate([jnp.where(lane < HEAD_DIM, xp, zero16),
                                jnp.where(lane < HEAD_DIM, zero16, xp)], axis=0)

    def off(m):
        return (((row2 % (2 * m)) >= m) & ((col2 // (2 * m)) == (row2 // (2 * m)))
                & ((col2 % (2 * m)) < m))

    combos = [(ci, pair) for ci in range(nch) for pair in range(n_pairs)]
    a_st = {q: stack(a_t, *q) for q in combos}
    r_st = {q: stack(r_t, *q) for q in combos}
    v_st = {q: stack(v, *q) for q in combos}
    g = {q: _dot_nt(jnp.concatenate([a_st[q], r_st[q]], axis=0),
                    jnp.concatenate([stack(b_t, *q), stack(k_t, *q)], axis=0)) for q in combos}
    l_ab = {q: jnp.where(strict, g[q][0:2 * c, 0:2 * c], 0.0) for q in combos}
    inv = {q: eye + jnp.where(off(1), l_ab[q], 0.0) for q in combos}
    m = 2
    while m < c:
        inv16 = {q: inv[q].astype(BF16) for q in combos}
        x = {q: _dot(jnp.where(off(m), l_ab[q], 0.0).astype(BF16), inv16[q]) for q in combos}
        inv = {q: inv[q] + _dot(inv16[q], x[q].astype(BF16)) for q in combos}
        m *= 2
    lakv = {q: _dot(jnp.where(strict, g[q][0:2 * c, 2 * c:4 * c], 0.0).astype(BF16), v_st[q])
            for q in combos}
    wu = {q: _dot(inv[q].astype(BF16), jnp.concatenate([a_st[q], lakv[q].astype(BF16)], axis=1))
          for q in combos}
    m_rbk = {q: jnp.concatenate([jnp.where(incl, g[q][2 * c:4 * c, 0:2 * c], 0.0),
                                 jnp.where(incl, g[q][2 * c:4 * c, 2 * c:4 * c], 0.0)],
                                axis=1).astype(BF16) for q in combos}
    bke_st = {q: jnp.concatenate([stack(b_e, *q), stack(k_e, *q)], axis=0) for q in combos}

    st = [state[pair] for pair in range(n_pairs)]
    ys = [[None] * n_pairs for _ in range(nch)]
    for ci in range(nch):
        wr = {pair: _dot(jnp.concatenate([wu[ci, pair][:, 0:LANES].astype(BF16), r_st[ci, pair]],
                                         axis=0), st[pair].astype(BF16)) for pair in range(n_pairs)}
        for pair in range(n_pairs):
            q = (ci, pair)
            u_st = wr[pair][0:2 * c] + wu[q][:, LANES:2 * LANES]
            uv_st = jnp.concatenate([u_st.astype(BF16), v_st[q]], axis=0)
            y_st = wr[pair][2 * c:4 * c] + _dot(m_rbk[q], uv_st)
            decay_col = jnp.broadcast_to(p_end[ci, :, pair * LANES:(pair + 1) * LANES],
                                         (LANES, LANES)).T
            st[pair] = st[pair] * decay_col + _dot_tn(bke_st[q], uv_st)
            ys[ci][pair] = y_st[0:c] + y_st[c:2 * c]
    for pair in range(n_pairs):
        state[pair] = st[pair]
    y = jnp.concatenate([jnp.concatenate(yr, axis=1) for yr in ys], axis=0)

    seg_mean = (seg * (1.0 / HEAD_DIM)).astype(BF16)
    mean = _dot3_lhs(y, seg_mean)
    d = y - mean
    var = _dot3_lhs(d * d, seg_mean)
    yn = d * lax.rsqrt(var + GN_EPS) * lng_ref[...] + lnb_ref[...]
    bonus = _dot3_lhs(r * k * rk_ref[...], seg) * v
    o_ref[...] = yn + bonus


def _rwkv(p32, params, batch, seq, r_blk, k_blk, v_blk, lo_blk, nch=4):
    width = params["w0"].shape[1]
    c = nch * RWKV_CHUNK
    nc = seq // c
    full = lambda a: pl.BlockSpec(a.shape, lambda b, i: (0,) * a.ndim)
    names = ("mu_r", "mu_k", "mu_v", "mu_lo", "w0", "w2", "a0", "a2", "kk", "ka", "rk",
             "lng", "lnb", "seg")
    return pl.pallas_call(
        functools.partial(_rwkv_kernel, nch=nch),
        grid=(batch, nc),
        in_specs=[pl.BlockSpec((c, width), lambda b, i: (b * nc + i, r_blk)),
                  pl.BlockSpec((c, width), lambda b, i: (b * nc + i, k_blk)),
                  pl.BlockSpec((c, width), lambda b, i: (b * nc + i, v_blk)),
                  pl.BlockSpec((c, LANES), lambda b, i: (b * nc + i, lo_blk))]
                 + [full(params[n]) for n in names],
        out_specs=pl.BlockSpec((c, width), lambda b, i: (b * nc + i, 0)),
        out_shape=jax.ShapeDtypeStruct((batch * seq, width), F32),
        scratch_shapes=[pltpu.VMEM((width // LANES, LANES, LANES), F32),
                        pltpu.VMEM((8, width), F32), pltpu.VMEM((8, width), F32),
                        pltpu.VMEM((8, width), F32), pltpu.VMEM((8, LANES), F32)],
        compiler_params=_cparams("parallel", "arbitrary"),
        name="rwkv7",
    )(p32, p32, p32, p32, *[params[n] for n in names])


def _merge_kernel(ocw_ref, os_ref, su_ref, sv_ref, rk_ref, x_ref, p_ref, g_ref, wl_ref,
                  sgg_ref, sgb_ref, sgw_ref, bexp_ref,
                  wb0_ref, wb1_ref, wb2_ref, wo_ref, pleg_ref, wpg_ref, wpp_ref, fin_ref,
                  o_ref, *, last):
    d = x_ref.shape[1]
    half = rk_ref.shape[1]
    nzw = os_ref.shape[1]
    h = _rmsnorm(x_ref[...], g_ref[...]).astype(BF16)
    late = lambda c0, n: _dot(h, wl_ref[:, c0:c0 + n])
    y_nsa = ocw_ref[...] + os_ref[...]
    z0 = _dot((y_nsa * _silu(late(3 * d, nzw))).astype(BF16), wb0_ref[...])
    merged = _sigmoid(late(0, d)) * z0
    y_sg = _spatial_gating_rows(su_ref, sv_ref, sgg_ref, sgb_ref, sgw_ref, bexp_ref)
    z1 = _dot((y_sg * _silu(late(3 * d + nzw, half))).astype(BF16), wb1_ref[...])
    merged = merged + _sigmoid(late(d, d)) * z1
    z2 = _dot((rk_ref[...] * _silu(late(3 * d + nzw + half, half))).astype(BF16), wb2_ref[...])
    merged = merged + _sigmoid(late(2 * d, d)) * z2
    x1 = x_ref[...] + _dot(merged.astype(BF16), wo_ref[...])
    hp = _rmsnorm(x1, pleg_ref[...]).astype(BF16)
    x2 = x1 + _sigmoid(_dot(hp, wpg_ref[...])) * _dot(p_ref[...].astype(BF16), wpp_ref[...])
    o_ref[...] = _rmsnorm(x2, fin_ref[...]) if last else x2


def _merge(o_cw, o_s, p32, blocks, y_rk, x2, p2, consts, last, tm=256):
    t, d = x2.shape
    half = y_rk.shape[1]
    tok = lambda w, j=0: pl.BlockSpec((tm, w), lambda i, j=j: (i, j))
    full = lambda a: pl.BlockSpec(a.shape, lambda i: (0,) * a.ndim, pipeline_mode=pl.Buffered(1))
    names = ("g", "wl", "sgg", "sgb", "sgw", "bexp", "wb0", "wb1", "wb2", "wo", "pleg", "wpg",
             "wpp", "fin")
    return pl.pallas_call(
        functools.partial(_merge_kernel, last=last),
        grid=(t // tm,),
        in_specs=[tok(o_cw.shape[1]), tok(o_s.shape[1]), tok(half, blocks["su"]),
                  tok(half, blocks["sv"]), tok(half), tok(d), tok(p2.shape[1])]
                 + [full(consts[n]) for n in names],
        out_specs=tok(d),
        out_shape=jax.ShapeDtypeStruct((t, d), F32),
        compiler_params=_cparams("parallel"),
        name="merge_out_ple",
    )(o_cw, o_s, p32, p32, y_rk, x2, p2, *[consts[n] for n in names])


def _pad_cols(w, n):
    return jnp.pad(w, ((0, 0), (0, n - w.shape[1])))


def _layer_layout(d_model):
    half = d_model // 2
    kvw = NSA_GROUPS * HEAD_DIM
    sizes = dict(nq=half, nkv=6 * kvw, ngate=3 * NSA_HEADS, nz=half, su=half, sv=half, sz=half,
                 rs=3 * half + 2 * LORA, rz=half, mg=3 * d_model)
    off, out = 0, {}
    for name, n in sizes.items():
        out[name] = (off, n)
        off += n
    return out


def _prep_proj_weights(w_in, d_model):
    lay = _layer_layout(d_model)
    half = d_model // 2
    kvw = NSA_GROUPS * HEAD_DIM
    col = lambda name, a=0, n=None: w_in[:, lay[name][0] + a: lay[name][0] + a + (n or lay[name][1] - a)]
    heads = lambda w: jnp.concatenate(
        [_pad_cols(w[:, h * HEAD_DIM:(h + 1) * HEAD_DIM], LANES) for h in range(w.shape[1] // HEAD_DIM)], axis=1)
    w32 = jnp.concatenate([
        col("su"), col("sv"), col("rs", 0, half), col("rs", half, half), col("rs", 2 * half, half),
        col("rs", 3 * half, 2 * LORA), _pad_cols(col("ngate"), LANES)], axis=1)
    w_kcvc = col("nkv", 0, 2 * kvw)
    blocks, o = {}, 0
    for name in ("su", "sv", "rr", "rk", "rv"):
        blocks[name] = o // half
        o += half
    blocks["lora"] = o // LANES
    o += LANES
    blocks["gate"] = o // LANES
    w_late = jnp.concatenate([col("mg"), col("nz"), col("sz"), col("rz")], axis=1)
    zero = jnp.zeros((w_in.shape[0], LANES), w_in.dtype)
    grp = lambda w, g: _pad_cols(w[:, g * HEAD_DIM:(g + 1) * HEAD_DIM], LANES)
    ks, vs = col("nkv", 2 * kvw, kvw), col("nkv", 3 * kvw, kvw)
    kw, vw = col("nkv", 4 * kvw, kvw), col("nkv", 5 * kvw, kvw)
    w16 = jnp.concatenate(
        [heads(col("nq"))]
        + [x for g in range(NSA_GROUPS) for x in (grp(ks, g), zero)]
        + [grp(vs, g) for g in range(NSA_GROUPS)]
        + [grp(kw, g) for g in range(NSA_GROUPS)]
        + [grp(vw, g) for g in range(NSA_GROUPS)], axis=1)
    return w32.astype(BF16), w_kcvc.astype(BF16), w16.astype(BF16), w_late.astype(BF16), blocks


def _prep_compress_weights(cmp_w1, cmp_w2, cmp_pe):
    half_blk = CMP_BLOCK // 2
    w1r = cmp_w1.reshape(2, 2, half_blk, HEAD_DIM, CMP_HIDDEN)
    eye = jnp.eye(NSA_GROUPS, dtype=cmp_w1.dtype)
    eye_kv = jnp.eye(2, dtype=cmp_w1.dtype)
    big = jnp.einsum('kaldh,kK,gG->alkgdKGh', w1r, eye_kv, eye)
    big = big.reshape(2, half_blk * 2 * NSA_GROUPS * HEAD_DIM, 2 * NSA_GROUPS * CMP_HIDDEN)
    pe = cmp_pe.reshape(2, 2, half_blk, HEAD_DIM)
    pe2 = jnp.broadcast_to(pe.transpose(1, 2, 0, 3)[:, :, :, None, :],
                           (2, half_blk, 2, NSA_GROUPS, HEAD_DIM)).reshape(2, -1)
    w2p = jnp.pad(cmp_w2, ((0, 0), (0, 0), (0, LANES - HEAD_DIM)))
    return pe2, big[0].astype(BF16), big[1].astype(BF16), w2p.astype(BF16)


def _cover_t(seq):
    n_grp = seq // CMP_STRIDE
    n_slc = seq // SEL_BLOCK
    cmp_start = jnp.arange(n_grp) * CMP_STRIDE
    slc_start = jnp.arange(n_slc) * SEL_BLOCK
    cover = ((cmp_start[None, :] <= slc_start[:, None] + SEL_BLOCK - 1)
             & (cmp_start[None, :] + CMP_BLOCK - 1 >= slc_start[:, None]))
    return jnp.pad(cover, ((0, LANES - n_slc), (0, 0))).astype(BF16)


def kernel(x, p, norm_g, w_in, cmp_w1, cmp_w2, cmp_pe, sg_ln_g, sg_ln_b, sg_w, sg_b, rk_mu, rk_w0,
           rk_w2, rk_a0, rk_a2, rk_kk, rk_ka, rk_rk, rk_lnx_g, rk_lnx_b, w_branch, w_o, ple_norm_g,
           w_ple_gate, w_ple_proj, final_norm_g):
    batch, seq, d = x.shape
    depth = w_in.shape[0]
    half = d // 2
    t = batch * seq
    assert seq % (2 * SEL_KEY_TILE) == 0
    assert seq >= WINDOW + Q_TILE and seq // SEL_BLOCK <= LANES
    x2 = x.reshape(t, d)
    cov_t = _cover_t(seq)
    seg = (jnp.arange(half)[:, None] // HEAD_DIM == jnp.arange(half)[None, :] // HEAD_DIM).astype(BF16)
    row = lambda v: v.reshape(1, -1)
    for i in range(depth):
        w32, w_kcvc, w16, w_late, blk = _prep_proj_weights(w_in[i], d)
        g = row(norm_g[i])
        p32, kc_raw, vc_raw, p16 = _proj(x2, g, w32, w_kcvc, w16, seq)

        pe2, wa, wb, w2p = _prep_compress_weights(cmp_w1[i], cmp_w2[i], cmp_pe[i])
        kc, vc = _compress(kc_raw.reshape(batch, seq, -1), vc_raw.reshape(batch, seq, -1),
                           pe2, wa, wb, w2p)
        o_cw, bias = _cmp_select(p16, p32, blk["gate"], kc, vc, cov_t, batch, seq)
        o_s = _sel_attn(p16, p32, blk["gate"], bias, batch, seq)

        mu = rk_mu[i]
        rk_params = dict(
            mu_r=row(mu[0:half]), mu_k=row(mu[half:2 * half]), mu_v=row(mu[2 * half:3 * half]),
            mu_lo=row(mu[3 * half:]), w0=row(rk_w0[i]),
            w2=jnp.pad(rk_w2[i], ((0, LANES - LORA), (0, 0))).astype(BF16), a0=row(rk_a0[i]),
            a2=jnp.pad(rk_a2[i], ((LANES - LORA, 0), (0, 0))).astype(BF16),
            kk=row(rk_kk[i]), ka=row(rk_ka[i]), rk=row(rk_rk[i]), lng=row(rk_lnx_g[i]),
            lnb=row(rk_lnx_b[i]), seg=seg)
        y_rk = _rwkv(p32, rk_params, batch, seq, blk["rr"], blk["rk"], blk["rv"], blk["lora"])

        consts = dict(g=g, wl=w_late, sgg=row(sg_ln_g[i]), sgb=row(sg_ln_b[i]), sgw=sg_w[i],
                      bexp=jnp.repeat(sg_b[i].T, half // SG_GROUPS, axis=1),
                      wb0=w_branch[i, 0].astype(BF16), wb1=w_branch[i, 1].astype(BF16),
                      wb2=w_branch[i, 2].astype(BF16), wo=w_o[i].astype(BF16),
                      pleg=row(ple_norm_g[i]), wpg=w_ple_gate[i].astype(BF16),
                      wpp=w_ple_proj[i].astype(BF16), fin=row(final_norm_g))
        x2 = _merge(o_cw, o_s, p32, blk, y_rk, x2, p[i].reshape(t, -1), consts,
                    last=(i == depth - 1))
    return x2.reshape(batch, seq, d)
```

```python
import functools

import jax
import jax.numpy as jnp
from jax import lax
from jax.experimental import pallas as pl
from jax.experimental.pallas import tpu as pltpu

F32 = jnp.float32
BF16 = jnp.bfloat16

HEAD_DIM = 64
LANES = 128
NSA_HEADS = 8
NSA_GROUPS = 2
NSA_HPG = NSA_HEADS // NSA_GROUPS
CMP_BLOCK = 32
CMP_STRIDE = 16
CMP_HIDDEN = 128
CMP_WIDTH_STEP = 128
SEL_BLOCK = 64
SEL_TOPK = 16
N_FORCED = 3
WINDOW = 512
Q_TILE = 128
SEL_KEY_TILE = 512
SEL_TAIL_STEP = 256
SG_GROUPS = 8
SG_CHUNK = 128
RWKV_HEADS = 8
RWKV_CHUNK = 64
LORA = 64
NORM_EPS = 1e-6
LN_EPS = 1e-5
GN_EPS = 64e-5
MASK_NEG = -1e30
FORCE_BONUS = 1e4
VMEM_LIMIT = 56 * 1024 * 1024


def _cparams(*sem):
    return pltpu.CompilerParams(dimension_semantics=sem, vmem_limit_bytes=VMEM_LIMIT)


def _dot(a, b, precision=None):
    return jnp.dot(a, b, preferred_element_type=F32, precision=precision)


def _dot_nt(a, b, precision=None):
    return lax.dot_general(a, b, (((1,), (1,)), ((), ())), preferred_element_type=F32,
                           precision=precision)


def _dot_tn(a, b, precision=None):
    return lax.dot_general(a, b, (((0,), (0,)), ((), ())), preferred_element_type=F32,
                           precision=precision)


def _split3(x):
    hi = x.astype(BF16)
    r1 = x - hi.astype(F32)
    mid = r1.astype(BF16)
    lo = (r1 - mid.astype(F32)).astype(BF16)
    return hi, mid, lo


def _dot3_lhs(x, w):
    hi, mid, lo = _split3(x)
    return _dot(hi, w) + _dot(mid, w) + _dot(lo, w)


def _dot3_rhs(w, x):
    hi, mid, lo = _split3(x)
    return _dot(w, hi) + _dot(w, mid) + _dot(w, lo)


def _sigmoid(x):
    return 1.0 / (1.0 + jnp.exp(-x))


def _silu(x):
    return x * _sigmoid(x)


def _rmsnorm(x, g):
    return x * lax.rsqrt(jnp.mean(x * x, axis=-1, keepdims=True) + NORM_EPS) * g


P16_Q = 0
P16_KX = NSA_HEADS * LANES
P16_VX = P16_KX + NSA_GROUPS * 2 * LANES
P16_KW = P16_VX + NSA_GROUPS * LANES
P16_VW = P16_KW + NSA_GROUPS * LANES
P16_N = P16_VW + NSA_GROUPS * LANES


PROJ_CHUNK = 2 * LANES


def _proj_kernel(x_ref, g_ref, w32_ref, wkc_ref, w16_ref, o32_ref, okc_ref, ovc_ref, o16_ref, *,
                 seq, tm):
    h = _rmsnorm(x_ref[...], g_ref[...]).astype(BF16)
    for c0 in range(0, o32_ref.shape[1], PROJ_CHUNK):
        o32_ref[:, c0:c0 + PROJ_CHUNK] = _dot(h, w32_ref[:, c0:c0 + PROJ_CHUNK])
    kcvc = _dot(h, wkc_ref[...])
    okc_ref[...] = kcvc[:, 0:LANES]
    ovc_ref[...] = kcvc[:, LANES:2 * LANES]
    pos = (pl.program_id(0) * tm) % seq + lax.broadcasted_iota(jnp.int32, (tm, LANES), 0)
    lane = lax.broadcasted_iota(jnp.int32, (tm, LANES), 1)
    onehot_blk = jnp.where(pos // SEL_BLOCK == lane, 1.0, 0.0)
    ones_col = jnp.where(lane == HEAD_DIM, 1.0, 0.0)
    zeros = jnp.zeros((tm, LANES), F32)

    def constant(c0):
        if P16_KX <= c0 < P16_VX and ((c0 - P16_KX) // LANES) % 2 == 1:
            return onehot_blk
        if c0 >= P16_VX and not (P16_KW <= c0 < P16_VW):
            return ones_col
        return zeros

    for c0 in range(0, P16_N, PROJ_CHUNK):
        y = _dot(h, w16_ref[:, c0:c0 + PROJ_CHUNK])
        consts = [constant(c0 + i * LANES) for i in range(PROJ_CHUNK // LANES)]
        if any(c is not zeros for c in consts):
            y = y + jnp.concatenate(consts, axis=1)
        o16_ref[:, c0:c0 + PROJ_CHUNK] = y.astype(BF16)


def _proj(x2, g, w32, wkc, w16, seq, tm=512):
    t, d = x2.shape
    n32 = w32.shape[1]
    assert wkc.shape[1] == 2 * LANES
    const = lambda a: pl.BlockSpec(a.shape, lambda i: (0, 0), pipeline_mode=pl.Buffered(1))
    return pl.pallas_call(
        functools.partial(_proj_kernel, seq=seq, tm=tm),
        grid=(t // tm,),
        in_specs=[pl.BlockSpec((tm, d), lambda i: (i, 0)), const(g), const(w32), const(wkc),
                  const(w16)],
        out_specs=[pl.BlockSpec((tm, n32), lambda i: (i, 0)),
                   pl.BlockSpec((tm, LANES), lambda i: (i, 0)),
                   pl.BlockSpec((tm, LANES), lambda i: (i, 0)),
                   pl.BlockSpec((tm, P16_N), lambda i: (i, 0))],
        out_shape=[jax.ShapeDtypeStruct((t, n32), F32), jax.ShapeDtypeStruct((t, LANES), F32),
                   jax.ShapeDtypeStruct((t, LANES), F32), jax.ShapeDtypeStruct((t, P16_N), BF16)],
        compiler_params=_cparams("parallel"),
        name="proj",
    )(x2, g, w32, wkc, w16)


def _compress_kernel(xk_ref, xv_ref, pe_ref, wa_ref, wb_ref, w2_ref, k_ref, v_ref):
    n_grp = xk_ref.shape[1] // CMP_STRIDE
    x = jnp.concatenate([ref[0, pl.ds(l, n_grp, stride=CMP_STRIDE), :]
                         for l in range(CMP_STRIDE) for ref in (xk_ref, xv_ref)], axis=1)
    a = _dot((x + pe_ref[0:1, :]).astype(BF16), wa_ref[...])
    b = _dot((x + pe_ref[1:2, :]).astype(BF16), wb_ref[...])
    hid = _silu(a + pltpu.roll(b, n_grp - 1, 0)).astype(BF16)
    row = lax.broadcasted_iota(jnp.int32, (n_grp, LANES), 0)
    lane = lax.broadcasted_iota(jnp.int32, (n_grp, LANES), 1)
    live = row < n_grp - 1
    for kv, out_ref in enumerate((k_ref, v_ref)):
        for g in range(NSA_GROUPS):
            c0 = (kv * NSA_GROUPS + g) * CMP_HIDDEN
            y = _dot(hid[:, c0:c0 + CMP_HIDDEN], w2_ref[kv])
            if kv == 1:
                y = y + jnp.where(lane == HEAD_DIM, 1.0, 0.0)
            out_ref[0, :, g * LANES:(g + 1) * LANES] = jnp.where(live, y, 0.0).astype(BF16)


def _compress(xk, xv, pe2, wa, wb, w2p):
    b, seq, width = xk.shape
    n_grp = seq // CMP_STRIDE
    out = jax.ShapeDtypeStruct((b, n_grp, NSA_GROUPS * LANES), BF16)
    return pl.pallas_call(
        _compress_kernel,
        grid=(b,),
        in_specs=[pl.BlockSpec((1, seq, width), lambda i: (i, 0, 0)),
                  pl.BlockSpec((1, seq, width), lambda i: (i, 0, 0)),
                  pl.BlockSpec(pe2.shape, lambda i: (0, 0)),
                  pl.BlockSpec(wa.shape, lambda i: (0, 0)),
                  pl.BlockSpec(wb.shape, lambda i: (0, 0)),
                  pl.BlockSpec(w2p.shape, lambda i: (0, 0, 0))],
        out_specs=[pl.BlockSpec((1, n_grp, NSA_GROUPS * LANES), lambda i: (i, 0, 0))] * 2,
        out_shape=[out, out],
        compiler_params=_cparams("parallel"),
        name="nsa_compress",
    )(xk, xv, pe2, wa, wb, w2p)


def _stack_heads(q_ref, g):
    parts = [q_ref[:, (g * NSA_HPG + h) * LANES:(g * NSA_HPG + h + 1) * LANES]
             for h in range(NSA_HPG)]
    return jnp.concatenate(parts, axis=0) * jnp.asarray(HEAD_DIM ** -0.5, BF16)


def _store_heads(o_ref, gate_ref, branch, g, o, accumulate=False):
    lane = lax.broadcasted_iota(jnp.int32, (Q_TILE, LANES), 1)
    gate = _sigmoid(gate_ref[...])

    def scaled(h):
        col = (g * NSA_HPG + h) * 3 + branch
        return jnp.broadcast_to(gate[:, col:col + 1], (Q_TILE, LANES)) * o[h * Q_TILE:(h + 1) * Q_TILE]

    for j in range(NSA_HPG // 2):
        tile = jnp.where(lane < HEAD_DIM, scaled(2 * j), pltpu.roll(scaled(2 * j + 1), HEAD_DIM, 1))
        c0 = (g * NSA_HPG // 2 + j) * LANES
        o_ref[:, c0:c0 + LANES] = o_ref[:, c0:c0 + LANES] + tile if accumulate else tile


def _cmp_select_kernel(q_ref, gate_ref, kc_ref, vc_ref, cov_ref, kw_ref, vw_ref, o_ref, bias_ref,
                       imp_ref, *, top_k):
    n_cmp = kc_ref.shape[1]
    n_slc = cov_ref.shape[0]
    q0 = pl.program_id(1) * Q_TILE
    rows = NSA_HPG * Q_TILE
    sel_shape = (n_slc, NSA_GROUPS * Q_TILE)
    j_idx = lax.broadcasted_iota(jnp.int32, sel_shape, 0)
    t_lane = q0 + lax.broadcasted_iota(jnp.int32, sel_shape, 1) % Q_TILE
    t_blk = t_lane // SEL_BLOCK
    forced = (j_idx == 0) | (j_idx == t_blk) | (j_idx == t_blk - 1)
    causal_blk = j_idx <= t_blk

    def attend(width):
        t_row = q0 + lax.broadcasted_iota(jnp.int32, (Q_TILE, width), 0)
        cmp_end = lax.broadcasted_iota(jnp.int32, (Q_TILE, width), 1) * CMP_STRIDE + CMP_BLOCK - 1
        mask = jnp.tile(jnp.where(cmp_end <= t_row, 0.0, MASK_NEG), (NSA_HPG, 1))
        any_valid = jnp.tile(jnp.where(t_row[:, 0:1] >= CMP_BLOCK - 1, 1.0, 0.0), (NSA_HPG, 1))
        s_all = [_dot_nt(_stack_heads(q_ref, g), kc_ref[0, 0:width, g * LANES:(g + 1) * LANES])
                 for g in range(NSA_GROUPS)]
        for g in range(NSA_GROUPS):
            s = s_all[g] + mask
            p = jnp.exp(s - jnp.max(s, axis=-1, keepdims=True))
            l = jnp.sum(p, axis=-1, keepdims=True)
            p = p * (any_valid / l)
            o = _dot(p.astype(BF16), vc_ref[0, 0:width, g * LANES:(g + 1) * LANES])
            _store_heads(o_ref, gate_ref, 0, g, o, accumulate=True)
            p_sum = p[0:Q_TILE]
            for h in range(1, NSA_HPG):
                p_sum = p_sum + p[h * Q_TILE:(h + 1) * Q_TILE]
            hi, mid, lo = _split3(p_sum)
            cov = cov_ref[:, 0:width]
            imp_ref[:, g * Q_TILE:(g + 1) * Q_TILE] = (_dot_nt(cov, hi) + _dot_nt(cov, mid)
                                                       + _dot_nt(cov, lo))

    _window_attention(q_ref, gate_ref, kw_ref, vw_ref, o_ref, q0)
    for var in range(n_cmp // CMP_WIDTH_STEP):
        @pl.when(q0 // (CMP_WIDTH_STEP * CMP_STRIDE) == var)
        def _():
            attend((var + 1) * CMP_WIDTH_STEP)

    score = jnp.where(forced, -jnp.inf, jnp.where(causal_blk, imp_ref[...], -FORCE_BONUS))

    j_grp = j_idx[:, 0:Q_TILE]

    def pick(_, carry):
        out = []
        for sc in carry:
            best = jnp.max(sc, axis=0, keepdims=True)
            first = jnp.min(jnp.where(sc == best, j_grp, n_slc), axis=0, keepdims=True)
            out.append(jnp.where(j_grp == first, -jnp.inf, sc))
        return tuple(out)

    init = tuple(score[:, g * Q_TILE:(g + 1) * Q_TILE] for g in range(NSA_GROUPS))
    picked = lax.fori_loop(0, top_k - N_FORCED, pick, init)
    for g in range(NSA_GROUPS):
        chosen = (picked[g] == -jnp.inf) & causal_blk[:, g * Q_TILE:(g + 1) * Q_TILE]
        bias_ref[:, g * n_slc:(g + 1) * n_slc] = jnp.where(chosen, 0.0, MASK_NEG).T.astype(BF16)


def _cmp_select(p16, p32, gate_blk, kc, vc, cov_t, batch, seq):
    n_cmp = kc.shape[1]
    n_slc = cov_t.shape[0]
    nq = seq // Q_TILE
    w = NSA_GROUPS * LANES
    return pl.pallas_call(
        functools.partial(_cmp_select_kernel, top_k=min(SEL_TOPK, seq // SEL_BLOCK)),
        grid=(batch, nq),
        in_specs=[pl.BlockSpec((Q_TILE, NSA_HEADS * LANES), lambda b, i: (b * nq + i, 0)),
                  pl.BlockSpec((Q_TILE, LANES), lambda b, i: (b * nq + i, gate_blk)),
                  pl.BlockSpec((1, n_cmp, w), lambda b, i: (b, 0, 0)),
                  pl.BlockSpec((1, n_cmp, w), lambda b, i: (b, 0, 0)),
                  pl.BlockSpec(cov_t.shape, lambda b, i: (0, 0)),
                  pl.BlockSpec((seq, w), lambda b, i: (b, P16_KW // w)),
                  pl.BlockSpec((seq, w), lambda b, i: (b, P16_VW // w))],
        out_specs=[pl.BlockSpec((Q_TILE, NSA_HEADS * HEAD_DIM), lambda b, i: (b * nq + i, 0)),
                   pl.BlockSpec((Q_TILE, NSA_GROUPS * n_slc), lambda b, i: (b * nq + i, 0))],
        out_shape=[jax.ShapeDtypeStruct((batch * seq, NSA_HEADS * HEAD_DIM), F32),
                   jax.ShapeDtypeStruct((batch * seq, NSA_GROUPS * n_slc), BF16)],
        scratch_shapes=[pltpu.VMEM((n_slc, NSA_GROUPS * Q_TILE), F32)],
        compiler_params=_cparams("parallel", "arbitrary"),
        name="nsa_cmp_select",
    )(p16, p32, kc, vc, cov_t, p16, p16)


def _sel_attn_kernel(q_ref, gate_ref, bias_ref, kx_ref, vx_ref, o_ref, m_scr, acc_scr, *, tk):
    n_slc = bias_ref.shape[1] // NSA_GROUPS
    q0 = pl.program_id(1) * Q_TILE
    rows = NSA_HPG * Q_TILE
    wide = 2 * tk
    n_wide = q0 // wide
    qx = []
    for g in range(NSA_GROUPS):
        bias = bias_ref[:, g * n_slc:(g + 1) * n_slc]
        qx.append(jnp.concatenate([_stack_heads(q_ref, g),
                                   jnp.concatenate([bias] * NSA_HPG, axis=0)], axis=1))

    def load():
        return tuple((m_scr[g], acc_scr[g]) for g in range(NSA_GROUPS))

    def store(state):
        for g in range(NSA_GROUPS):
            m_scr[g], acc_scr[g] = state[g]

    def scores(k0, width):
        return [_dot_nt(qx[g], kx_ref[pl.ds(k0, width), g * 2 * LANES:(g + 1) * 2 * LANES])
                for g in range(NSA_GROUPS)]

    def update(k0, width, diagonal, s_all, state):
        probs, m_all = [], []
        for g in range(NSA_GROUPS):
            s = s_all[g]
            if diagonal:
                t_row = q0 + lax.broadcasted_iota(jnp.int32, (Q_TILE, width), 0)
                key = k0 + lax.broadcasted_iota(jnp.int32, (Q_TILE, width), 1)
                s = s + jnp.tile(jnp.where(key <= t_row, 0.0, MASK_NEG), (NSA_HPG, 1))
            m_new = jnp.maximum(state[g][0], jnp.max(s, axis=-1, keepdims=True))
            probs.append(jnp.exp(s - jnp.tile(m_new, (1, width // LANES))).astype(BF16))
            m_all.append(m_new)
        out = []
        for g in range(NSA_GROUPS):
            m_old, acc = state[g]
            pv = _dot(probs[g], vx_ref[pl.ds(k0, width), g * LANES:(g + 1) * LANES])
            out.append((m_all[g], jnp.exp(m_old - m_all[g]) * acc + pv))
        return tuple(out)

    def step(k0, width, diagonal, state):
        return update(k0, width, diagonal, scores(k0, width), state)

    store(tuple((jnp.full((rows, LANES), MASK_NEG, F32), jnp.zeros((rows, LANES), F32))
                for _ in range(NSA_GROUPS)))

    def pair(j, _):
        k0 = pl.multiple_of(j * 2 * wide, 2 * wide)
        s_a, s_b = scores(k0, wide), scores(k0 + wide, wide)
        store(update(k0 + wide, wide, False, s_b, update(k0, wide, False, s_a, load())))
        return 0

    lax.fori_loop(0, n_wide // 2, pair, 0)

    @pl.when(n_wide % 2 == 1)
    def _():
        store(step(pl.multiple_of((n_wide - 1) * wide, wide), wide, False, load()))

    tail0 = pl.multiple_of(n_wide * wide, wide)
    tail_steps = (q0 + Q_TILE - tail0 + SEL_TAIL_STEP - 1) // SEL_TAIL_STEP
    for var in range(1, wide // SEL_TAIL_STEP + 1):
        @pl.when(tail_steps == var)
        def _():
            store(step(tail0, var * SEL_TAIL_STEP, True, load()))

    for g in range(NSA_GROUPS):
        acc = acc_scr[g]
        _store_heads(o_ref, gate_ref, 1, g, acc / acc[:, HEAD_DIM:HEAD_DIM + 1])


def _sel_attn(p16, p32, gate_blk, bias, batch, seq):
    nq = seq // Q_TILE
    tk = min(SEL_KEY_TILE, seq)
    kx_w = NSA_GROUPS * 2 * LANES
    vx_w = NSA_GROUPS * LANES
    return pl.pallas_call(
        functools.partial(_sel_attn_kernel, tk=tk),
        grid=(batch, nq),
        in_specs=[pl.BlockSpec((Q_TILE, NSA_HEADS * LANES), lambda b, i: (b * nq + i, 0)),
                  pl.BlockSpec((Q_TILE, LANES), lambda b, i: (b * nq + i, gate_blk)),
                  pl.BlockSpec((Q_TILE, bias.shape[1]), lambda b, i: (b * nq + i, 0)),
                  pl.BlockSpec((seq, kx_w), lambda b, i: (b, P16_KX // kx_w)),
                  pl.BlockSpec((seq, vx_w), lambda b, i: (b, P16_VX // vx_w))],
        out_specs=pl.BlockSpec((Q_TILE, NSA_HEADS * HEAD_DIM), lambda b, i: (b * nq + i, 0)),
        out_shape=jax.ShapeDtypeStruct((batch * seq, NSA_HEADS * HEAD_DIM), F32),
        scratch_shapes=[pltpu.VMEM((NSA_GROUPS, NSA_HPG * Q_TILE, LANES), F32),
                        pltpu.VMEM((NSA_GROUPS, NSA_HPG * Q_TILE, LANES), F32)],
        compiler_params=_cparams("parallel", "arbitrary"),
        name="nsa_sel_attn",
    )(p16, p32, bias, p16, p16)


def _window_attention(q_ref, gate_ref, kw_ref, vw_ref, o_ref, q0):
    seq = kw_ref.shape[0]
    span = min(WINDOW + Q_TILE, seq)
    start = pl.multiple_of(jnp.clip(q0 + Q_TILE - span, 0, seq - span), Q_TILE)
    t_row = q0 + lax.broadcasted_iota(jnp.int32, (Q_TILE, span), 0)
    pos = start + lax.broadcasted_iota(jnp.int32, (Q_TILE, span), 1)
    mask = jnp.tile(jnp.where((pos <= t_row) & (pos > t_row - WINDOW), 0.0, MASK_NEG), (NSA_HPG, 1))
    s_all = [_dot_nt(_stack_heads(q_ref, g), kw_ref[pl.ds(start, span), g * LANES:(g + 1) * LANES])
             for g in range(NSA_GROUPS)]
    p_all = []
    for g in range(NSA_GROUPS):
        s = s_all[g] + mask
        p_all.append(jnp.exp(s - jnp.max(s, axis=-1, keepdims=True)).astype(BF16))
    for g in range(NSA_GROUPS):
        acc = _dot(p_all[g], vw_ref[pl.ds(start, span), g * LANES:(g + 1) * LANES])
        _store_heads(o_ref, gate_ref, 2, g, acc / acc[:, HEAD_DIM:HEAD_DIM + 1])


def _spatial_gating_rows(u_ref, v_ref, g_ref, b_ref, w_ref, bexp_ref):
    c = SG_CHUNK
    tri = (lax.broadcasted_iota(jnp.int32, (c, c), 1) <= lax.broadcasted_iota(jnp.int32, (c, c), 0))
    lane = lax.broadcasted_iota(jnp.int32, (c, LANES), 1)
    wm = [jnp.where(tri, w_ref[g], 0.0).astype(BF16) for g in range(SG_GROUPS)]
    out = []
    for ci in range(u_ref.shape[0] // c):
        v = v_ref[ci * c:(ci + 1) * c, :]
        mu = jnp.mean(v, axis=-1, keepdims=True)
        d = v - mu
        var = jnp.mean(d * d, axis=-1, keepdims=True)
        vn = (d * lax.rsqrt(var + LN_EPS) * g_ref[...] + b_ref[...]).astype(BF16)
        mixed = []
        for pair in range(SG_GROUPS // 2):
            vp = vn[:, pair * LANES:(pair + 1) * LANES]
            mixed.append(jnp.where(lane < HEAD_DIM, _dot(wm[2 * pair], vp), _dot(wm[2 * pair + 1], vp)))
        out.append(u_ref[ci * c:(ci + 1) * c, :] * (jnp.concatenate(mixed, axis=1) + bexp_ref[...]))
    return jnp.concatenate(out, axis=0)


def _rwkv_kernel(r_ref, k_ref, v_ref, lo_ref, mu_r, mu_k, mu_v, mu_lo, w0_ref, w2_ref, a0_ref,
                 a2_ref, kk_ref, ka_ref, rk_ref, lng_ref, lnb_ref, seg_ref, o_ref,
                 state, prev_r, prev_k, prev_v, prev_lo, *, nch):
    c = RWKV_CHUNK
    rows = nch * c
    width = r_ref.shape[1]
    n_pairs = width // LANES

    @pl.when(pl.program_id(1) == 0)
    def _():
        state[...] = jnp.zeros_like(state)
        for ref in (prev_r, prev_k, prev_v, prev_lo):
            ref[...] = jnp.zeros_like(ref)

    def token_shift(x_ref, prev_ref, mu_ref):
        x = x_ref[...]
        first = lax.broadcasted_iota(jnp.int32, x.shape, 0) == 0
        prev = jnp.where(first, prev_ref[0:1, :], pltpu.roll(x, 1, 0))
        prev_ref[0:1, :] = x[rows - 1:rows, :]
        return x + (prev - x) * mu_ref[...]

    r = token_shift(r_ref, prev_r, mu_r)
    k = token_shift(k_ref, prev_k, mu_k)
    v = token_shift(v_ref, prev_v, mu_v)
    lo = token_shift(lo_ref, prev_lo, mu_lo)

    lane_lo = lax.broadcasted_iota(jnp.int32, lo.shape, 1)
    w_in = jnp.where(lane_lo < LORA, jnp.tanh(lo), 0.0).astype(BF16)
    a_in = jnp.where(lane_lo < LORA, 0.0, lo).astype(BF16)
    z = -(w0_ref[...] + _dot(w_in, w2_ref[...]))
    w = -(jnp.maximum(z, 0.0) + jnp.log(1.0 + jnp.exp(-jnp.abs(z)))) - 0.5
    a = _sigmoid(a0_ref[...] + _dot(a_in, a2_ref[...]))

    seg = seg_ref[...]
    kkf = k * kk_ref[...]
    kk = kkf / jnp.maximum(jnp.sqrt(_dot3_lhs(kkf * kkf, seg)), 1e-12)
    k = k * (1.0 + (a - 1.0) * ka_ref[...])
    logd = -jnp.exp(w)
    t_row = lax.broadcasted_iota(jnp.int32, (rows, rows), 0)
    t_col = lax.broadcasted_iota(jnp.int32, (rows, rows), 1)
    cum = (t_col <= t_row) & (t_col // c == t_row // c)
    cl = _dot3_rhs(jnp.where(cum, 1.0, 0.0).astype(BF16), logd)
    cl_end = cl.reshape(nch, c, width)[:, c - 1:c, :]
    e_neg = jnp.exp(-cl)
    e_end = jnp.exp(jnp.broadcast_to(cl_end, (nch, c, width)).reshape(rows, width) - cl)
    b = kk * a
    a_t = -kk * jnp.exp(cl - logd)
    b_t = b * e_neg
    k_t = k * e_neg
    r_t = r * jnp.exp(cl)
    b_e = b * e_end
    k_e = k * e_end
    p_end = jnp.exp(cl_end)

    lane = lax.broadcasted_iota(jnp.int32, (c, LANES), 1)
    row2 = lax.broadcasted_iota(jnp.int32, (2 * c, 2 * c), 0)
    col2 = lax.broadcasted_iota(jnp.int32, (2 * c, 2 * c), 1)
    same_head = (row2 // c) == (col2 // c)
    strict = same_head & (col2 < row2)
    incl = same_head & (col2 <= row2)
    eye = jnp.where(row2 == col2, 1.0, 0.0)
    zero16 = jnp.zeros((c, LANES), BF16)

    def stack(x, ci, pair):
        xp = x[ci * c:(ci + 1) * c, pair * LANES:(pair + 1) * LANES].astype(BF16)
        return jnp.concatenate([jnp.where(lane < HEAD_DIM, xp, zero16),
                                jnp.where(lane < HEAD_DIM, zero16, xp)], axis=0)

    def off(m):
        return (((row2 % (2 * m)) >= m) & ((col2 // (2 * m)) == (row2 // (2 * m)))
                & ((col2 % (2 * m)) < m))

    combos = [(ci, pair) for ci in range(nch) for pair in range(n_pairs)]
    a_st = {q: stack(a_t, *q) for q in combos}
    r_st = {q: stack(r_t, *q) for q in combos}
    v_st = {q: stack(v, *q) for q in combos}
    g = {q: _dot_nt(jnp.concatenate([a_st[q], r_st[q]], axis=0),
                    jnp.concatenate([stack(b_t, *q), stack(k_t, *q)], axis=0)) for q in combos}
    l_ab = {q: jnp.where(strict, g[q][0:2 * c, 0:2 * c], 0.0) for q in combos}
    inv = {q: eye + jnp.where(off(1), l_ab[q], 0.0) for q in combos}
    m = 2
    while m < c:
        inv16 = {q: inv[q].astype(BF16) for q in combos}
        x = {q: _dot(jnp.where(off(m), l_ab[q], 0.0).astype(BF16), inv16[q]) for q in combos}
        inv = {q: inv[q] + _dot(inv16[q], x[q].astype(BF16)) for q in combos}
        m *= 2
    lakv = {q: _dot(jnp.where(strict, g[q][0:2 * c, 2 * c:4 * c], 0.0).astype(BF16), v_st[q])
            for q in combos}
    wu = {q: _dot(inv[q].astype(BF16), jnp.concatenate([a_st[q], lakv[q].astype(BF16)], axis=1))
          for q in combos}
    m_rbk = {q: jnp.concatenate([jnp.where(incl, g[q][2 * c:4 * c, 0:2 * c], 0.0),
                                 jnp.where(incl, g[q][2 * c:4 * c, 2 * c:4 * c], 0.0)],
                                axis=1).astype(BF16) for q in combos}
    bke_st = {q: jnp.concatenate([stack(b_e, *q), stack(k_e, *q)], axis=0) for q in combos}

    st = [state[pair] for pair in range(n_pairs)]
    ys = [[None] * n_pairs for _ in range(nch)]
    for ci in range(nch):
        wr = {pair: _dot(jnp.concatenate([wu[ci, pair][:, 0:LANES].astype(BF16), r_st[ci, pair]],
                                         axis=0), st[pair].astype(BF16)) for pair in range(n_pairs)}
        for pair in range(n_pairs):
            q = (ci, pair)
            u_st = wr[pair][0:2 * c] + wu[q][:, LANES:2 * LANES]
            uv_st = jnp.concatenate([u_st.astype(BF16), v_st[q]], axis=0)
            y_st = wr[pair][2 * c:4 * c] + _dot(m_rbk[q], uv_st)
            decay_col = jnp.broadcast_to(p_end[ci, :, pair * LANES:(pair + 1) * LANES],
                                         (LANES, LANES)).T
            st[pair] = st[pair] * decay_col + _dot_tn(bke_st[q], uv_st)
            ys[ci][pair] = y_st[0:c] + y_st[c:2 * c]
    for pair in range(n_pairs):
        state[pair] = st[pair]
    y = jnp.concatenate([jnp.concatenate(yr, axis=1) for yr in ys], axis=0)

    seg_mean = (seg * (1.0 / HEAD_DIM)).astype(BF16)
    mean = _dot3_lhs(y, seg_mean)
    d = y - mean
    var = _dot3_lhs(d * d, seg_mean)
    yn = d * lax.rsqrt(var + GN_EPS) * lng_ref[...] + lnb_ref[...]
    bonus = _dot3_lhs(r * k * rk_ref[...], seg) * v
    o_ref[...] = yn + bonus


def _rwkv(p32, params, batch, seq, r_blk, k_blk, v_blk, lo_blk, nch=4):
    width = params["w0"].shape[1]
    c = nch * RWKV_CHUNK
    nc = seq // c
    full = lambda a: pl.BlockSpec(a.shape, lambda b, i: (0,) * a.ndim)
    names = ("mu_r", "mu_k", "mu_v", "mu_lo", "w0", "w2", "a0", "a2", "kk", "ka", "rk",
             "lng", "lnb", "seg")
    return pl.pallas_call(
        functools.partial(_rwkv_kernel, nch=nch),
        grid=(batch, nc),
        in_specs=[pl.BlockSpec((c, width), lambda b, i: (b * nc + i, r_blk)),
                  pl.BlockSpec((c, width), lambda b, i: (b * nc + i, k_blk)),
                  pl.BlockSpec((c, width), lambda b, i: (b * nc + i, v_blk)),
                  pl.BlockSpec((c, LANES), lambda b, i: (b * nc + i, lo_blk))]
                 + [full(params[n]) for n in names],
        out_specs=pl.BlockSpec((c, width), lambda b, i: (b * nc + i, 0)),
        out_shape=jax.ShapeDtypeStruct((batch * seq, width), F32),
        scratch_shapes=[pltpu.VMEM((width // LANES, LANES, LANES), F32),
                        pltpu.VMEM((8, width), F32), pltpu.VMEM((8, width), F32),
                        pltpu.VMEM((8, width), F32), pltpu.VMEM((8, LANES), F32)],
        compiler_params=_cparams("parallel", "arbitrary"),
        name="rwkv7",
    )(p32, p32, p32, p32, *[params[n] for n in names])


def _merge_kernel(ocw_ref, os_ref, su_ref, sv_ref, rk_ref, x_ref, p_ref, g_ref, wl_ref,
                  sgg_ref, sgb_ref, sgw_ref, bexp_ref,
                  wb0_ref, wb1_ref, wb2_ref, wo_ref, pleg_ref, wpg_ref, wpp_ref, fin_ref,
                  o_ref, *, last):
    d = x_ref.shape[1]
    half = rk_ref.shape[1]
    nzw = os_ref.shape[1]
    h = _rmsnorm(x_ref[...], g_ref[...]).astype(BF16)
    late = lambda c0, n: _dot(h, wl_ref[:, c0:c0 + n])
    y_nsa = ocw_ref[...] + os_ref[...]
    z0 = _dot((y_nsa * _silu(late(3 * d, nzw))).astype(BF16), wb0_ref[...])
    merged = _sigmoid(late(0, d)) * z0
    y_sg = _spatial_gating_rows(su_ref, sv_ref, sgg_ref, sgb_ref, sgw_ref, bexp_ref)
    z1 = _dot((y_sg * _silu(late(3 * d + nzw, half))).astype(BF16), wb1_ref[...])
    merged = merged + _sigmoid(late(d, d)) * z1
    z2 = _dot((rk_ref[...] * _silu(late(3 * d + nzw + half, half))).astype(BF16), wb2_ref[...])
    merged = merged + _sigmoid(late(2 * d, d)) * z2
    x1 = x_ref[...] + _dot(merged.astype(BF16), wo_ref[...])
    hp = _rmsnorm(x1, pleg_ref[...]).astype(BF16)
    x2 = x1 + _sigmoid(_dot(hp, wpg_ref[...])) * _dot(p_ref[...].astype(BF16), wpp_ref[...])
    o_ref[...] = _rmsnorm(x2, fin_ref[...]) if last else x2


def _merge(o_cw, o_s, p32, blocks, y_rk, x2, p2, consts, last, tm=256):
    t, d = x2.shape
    half = y_rk.shape[1]
    tok = lambda w, j=0: pl.BlockSpec((tm, w), lambda i, j=j: (i, j))
    full = lambda a: pl.BlockSpec(a.shape, lambda i: (0,) * a.ndim, pipeline_mode=pl.Buffered(1))
    names = ("g", "wl", "sgg", "sgb", "sgw", "bexp", "wb0", "wb1", "wb2", "wo", "pleg", "wpg",
             "wpp", "fin")
    return pl.pallas_call(
        functools.partial(_merge_kernel, last=last),
        grid=(t // tm,),
        in_specs=[tok(o_cw.shape[1]), tok(o_s.shape[1]), tok(half, blocks["su"]),
                  tok(half, blocks["sv"]), tok(half), tok(d), tok(p2.shape[1])]
                 + [full(consts[n]) for n in names],
        out_specs=tok(d),
        out_shape=jax.ShapeDtypeStruct((t, d), F32),
        compiler_params=_cparams("parallel"),
        name="merge_out_ple",
    )(o_cw, o_s, p32, p32, y_rk, x2, p2, *[consts[n] for n in names])


def _pad_cols(w, n):
    return jnp.pad(w, ((0, 0), (0, n - w.shape[1])))


def _layer_layout(d_model):
    half = d_model // 2
    kvw = NSA_GROUPS * HEAD_DIM
    sizes = dict(nq=half, nkv=6 * kvw, ngate=3 * NSA_HEADS, nz=half, su=half, sv=half, sz=half,
                 rs=3 * half + 2 * LORA, rz=half, mg=3 * d_model)
    off, out = 0, {}
    for name, n in sizes.items():
        out[name] = (off, n)
        off += n
    return out


def _prep_proj_weights(w_in, d_model):
    lay = _layer_layout(d_model)
    half = d_model // 2
    kvw = NSA_GROUPS * HEAD_DIM
    col = lambda name, a=0, n=None: w_in[:, lay[name][0] + a: lay[name][0] + a + (n or lay[name][1] - a)]
    heads = lambda w: jnp.concatenate(
        [_pad_cols(w[:, h * HEAD_DIM:(h + 1) * HEAD_DIM], LANES) for h in range(w.shape[1] // HEAD_DIM)], axis=1)
    w32 = jnp.concatenate([
        col("su"), col("sv"), col("rs", 0, half), col("rs", half, half), col("rs", 2 * half, half),
        col("rs", 3 * half, 2 * LORA), _pad_cols(col("ngate"), LANES)], axis=1)
    w_kcvc = col("nkv", 0, 2 * kvw)
    blocks, o = {}, 0
    for name in ("su", "sv", "rr", "rk", "rv"):
        blocks[name] = o // half
        o += half
    blocks["lora"] = o // LANES
    o += LANES
    blocks["gate"] = o // LANES
    w_late = jnp.concatenate([col("mg"), col("nz"), col("sz"), col("rz")], axis=1)
    zero = jnp.zeros((w_in.shape[0], LANES), w_in.dtype)
    grp = lambda w, g: _pad_cols(w[:, g * HEAD_DIM:(g + 1) * HEAD_DIM], LANES)
    ks, vs = col("nkv", 2 * kvw, kvw), col("nkv", 3 * kvw, kvw)
    kw, vw = col("nkv", 4 * kvw, kvw), col("nkv", 5 * kvw, kvw)
    w16 = jnp.concatenate(
        [heads(col("nq"))]
        + [x for g in range(NSA_GROUPS) for x in (grp(ks, g), zero)]
        + [grp(vs, g) for g in range(NSA_GROUPS)]
        + [grp(kw, g) for g in range(NSA_GROUPS)]
        + [grp(vw, g) for g in range(NSA_GROUPS)], axis=1)
    return w32.astype(BF16), w_kcvc.astype(BF16), w16.astype(BF16), w_late.astype(BF16), blocks


def _prep_compress_weights(cmp_w1, cmp_w2, cmp_pe):
    half_blk = CMP_BLOCK // 2
    w1r = cmp_w1.reshape(2, 2, half_blk, HEAD_DIM, CMP_HIDDEN)
    eye = jnp.eye(NSA_GROUPS, dtype=cmp_w1.dtype)
    eye_kv = jnp.eye(2, dtype=cmp_w1.dtype)
    big = jnp.einsum('kaldh,kK,gG->alkgdKGh', w1r, eye_kv, eye)
    big = big.reshape(2, half_blk * 2 * NSA_GROUPS * HEAD_DIM, 2 * NSA_GROUPS * CMP_HIDDEN)
    pe = cmp_pe.reshape(2, 2, half_blk, HEAD_DIM)
    pe2 = jnp.broadcast_to(pe.transpose(1, 2, 0, 3)[:, :, :, None, :],
                           (2, half_blk, 2, NSA_GROUPS, HEAD_DIM)).reshape(2, -1)
    w2p = jnp.pad(cmp_w2, ((0, 0), (0, 0), (0, LANES - HEAD_DIM)))
    return pe2, big[0].astype(BF16), big[1].astype(BF16), w2p.astype(BF16)


def _cover_t(seq):
    n_grp = seq // CMP_STRIDE
    n_slc = seq // SEL_BLOCK
    cmp_start = jnp.arange(n_grp) * CMP_STRIDE
    slc_start = jnp.arange(n_slc) * SEL_BLOCK
    cover = ((cmp_start[None, :] <= slc_start[:, None] + SEL_BLOCK - 1)
             & (cmp_start[None, :] + CMP_BLOCK - 1 >= slc_start[:, None]))
    return jnp.pad(cover, ((0, LANES - n_slc), (0, 0))).astype(BF16)


def kernel(x, p, norm_g, w_in, cmp_w1, cmp_w2, cmp_pe, sg_ln_g, sg_ln_b, sg_w, sg_b, rk_mu, rk_w0,
           rk_w2, rk_a0, rk_a2, rk_kk, rk_ka, rk_rk, rk_lnx_g, rk_lnx_b, w_branch, w_o, ple_norm_g,
           w_ple_gate, w_ple_proj, final_norm_g):
    batch, seq, d = x.shape
    depth = w_in.shape[0]
    half = d // 2
    t = batch * seq
    assert seq % (2 * SEL_KEY_TILE) == 0 and (seq // CMP_STRIDE) % CMP_WIDTH_STEP == 0
    assert seq >= WINDOW + Q_TILE and seq // SEL_BLOCK <= LANES
    x2 = x.reshape(t, d)
    cov_t = _cover_t(seq)
    seg = (jnp.arange(half)[:, None] // HEAD_DIM == jnp.arange(half)[None, :] // HEAD_DIM).astype(BF16)
    row = lambda v: v.reshape(1, -1)
    for i in range(depth):
        w32, w_kcvc, w16, w_late, blk = _prep_proj_weights(w_in[i], d)
        g = row(norm_g[i])
        p32, kc_raw, vc_raw, p16 = _proj(x2, g, w32, w_kcvc, w16, seq)

        pe2, wa, wb, w2p = _prep_compress_weights(cmp_w1[i], cmp_w2[i], cmp_pe[i])
        kc, vc = _compress(kc_raw.reshape(batch, seq, -1), vc_raw.reshape(batch, seq, -1),
                           pe2, wa, wb, w2p)
        o_cw, bias = _cmp_select(p16, p32, blk["gate"], kc, vc, cov_t, batch, seq)
        o_s = _sel_attn(p16, p32, blk["gate"], bias, batch, seq)

        mu = rk_mu[i]
        rk_params = dict(
            mu_r=row(mu[0:half]), mu_k=row(mu[half:2 * half]), mu_v=row(mu[2 * half:3 * half]),
            mu_lo=row(mu[3 * half:]), w0=row(rk_w0[i]),
            w2=jnp.pad(rk_w2[i], ((0, LANES - LORA), (0, 0))).astype(BF16), a0=row(rk_a0[i]),
            a2=jnp.pad(rk_a2[i], ((LANES - LORA, 0), (0, 0))).astype(BF16),
            kk=row(rk_kk[i]), ka=row(rk_ka[i]), rk=row(rk_rk[i]), lng=row(rk_lnx_g[i]),
            lnb=row(rk_lnx_b[i]), seg=seg)
        y_rk = _rwkv(p32, rk_params, batch, seq, blk["rr"], blk["rk"], blk["rv"], blk["lora"])

        consts = dict(g=g, wl=w_late, sgg=row(sg_ln_g[i]), sgb=row(sg_ln_b[i]), sgw=sg_w[i],
                      bexp=jnp.repeat(sg_b[i].T, half // SG_GROUPS, axis=1),
                      wb0=w_branch[i, 0].astype(BF16), wb1=w_branch[i, 1].astype(BF16),
                      wb2=w_branch[i, 2].astype(BF16), wo=w_o[i].astype(BF16),
                      pleg=row(ple_norm_g[i]), wpg=w_ple_gate[i].astype(BF16),
                      wpp=w_ple_proj[i].astype(BF16), fin=row(final_norm_g))
        x2 = _merge(o_cw, o_s, p32, blk, y_rk, x2, p[i].reshape(t, -1), consts,
                    last=(i == depth - 1))
    return x2.reshape(batch, seq, d)
```

```python
import functools

import jax
import jax.numpy as jnp
from jax import lax
from jax.experimental import pallas as pl
from jax.experimental.pallas import tpu as pltpu

F32 = jnp.float32
BF16 = jnp.bfloat16

HEAD_DIM = 64
LANES = 128
NSA_HEADS = 8
NSA_GROUPS = 2
NSA_HPG = NSA_HEADS // NSA_GROUPS
CMP_BLOCK = 32
CMP_STRIDE = 16
CMP_HIDDEN = 128
CMP_WIDTH_STEP = 128
SEL_BLOCK = 64
SEL_TOPK = 16
N_FORCED = 3
WINDOW = 512
Q_TILE = 128
SEL_Q_TILE = 256
SEL_KEY_TILE = 512
SEL_TAIL_STEP = 256
SG_GROUPS = 8
SG_CHUNK = 128
RWKV_HEADS = 8
RWKV_CHUNK = 64
LORA = 64
NORM_EPS = 1e-6
LN_EPS = 1e-5
GN_EPS = 64e-5
MASK_NEG = -1e30
FORCE_BONUS = 1e4
VMEM_LIMIT = 56 * 1024 * 1024


def _cparams(*sem):
    return pltpu.CompilerParams(dimension_semantics=sem, vmem_limit_bytes=VMEM_LIMIT)


def _dot(a, b, precision=None):
    return jnp.dot(a, b, preferred_element_type=F32, precision=precision)


def _dot_nt(a, b, precision=None):
    return lax.dot_general(a, b, (((1,), (1,)), ((), ())), preferred_element_type=F32,
                           precision=precision)


def _dot_tn(a, b, precision=None):
    return lax.dot_general(a, b, (((0,), (0,)), ((), ())), preferred_element_type=F32,
                           precision=precision)


def _split3(x):
    hi = x.astype(BF16)
    r1 = x - hi.astype(F32)
    mid = r1.astype(BF16)
    lo = (r1 - mid.astype(F32)).astype(BF16)
    return hi, mid, lo


def _dot3_lhs(x, w):
    hi, mid, lo = _split3(x)
    return _dot(hi, w) + _dot(mid, w) + _dot(lo, w)


def _dot3_rhs(w, x):
    hi, mid, lo = _split3(x)
    return _dot(w, hi) + _dot(w, mid) + _dot(w, lo)


def _sigmoid(x):
    return 1.0 / (1.0 + jnp.exp(-x))


def _silu(x):
    return x * _sigmoid(x)


def _rmsnorm(x, g):
    return x * lax.rsqrt(jnp.mean(x * x, axis=-1, keepdims=True) + NORM_EPS) * g


P16_Q = 0
P16_KX = NSA_HEADS * LANES
P16_VX = P16_KX + NSA_GROUPS * 2 * LANES
P16_KW = P16_VX + NSA_GROUPS * LANES
P16_VW = P16_KW + NSA_GROUPS * LANES
P16_N = P16_VW + NSA_GROUPS * LANES


PROJ_CHUNK = 2 * LANES


def _proj_kernel(x_ref, g_ref, w32_ref, wkc_ref, w16_ref, o32_ref, okc_ref, ovc_ref, o16_ref, *,
                 seq, tm):
    h = _rmsnorm(x_ref[...], g_ref[...]).astype(BF16)
    for c0 in range(0, o32_ref.shape[1], PROJ_CHUNK):
        o32_ref[:, c0:c0 + PROJ_CHUNK] = _dot(h, w32_ref[:, c0:c0 + PROJ_CHUNK])
    kcvc = _dot(h, wkc_ref[...])
    okc_ref[...] = kcvc[:, 0:LANES]
    ovc_ref[...] = kcvc[:, LANES:2 * LANES]
    pos = (pl.program_id(0) * tm) % seq + lax.broadcasted_iota(jnp.int32, (tm, LANES), 0)
    lane = lax.broadcasted_iota(jnp.int32, (tm, LANES), 1)
    onehot_blk = jnp.where(pos // SEL_BLOCK == lane, 1.0, 0.0)
    ones_col = jnp.where(lane == HEAD_DIM, 1.0, 0.0)
    zeros = jnp.zeros((tm, LANES), F32)

    def constant(c0):
        if P16_KX <= c0 < P16_VX and ((c0 - P16_KX) // LANES) % 2 == 1:
            return onehot_blk
        if c0 >= P16_VX and not (P16_KW <= c0 < P16_VW):
            return ones_col
        return zeros

    for c0 in range(0, P16_N, PROJ_CHUNK):
        y = _dot(h, w16_ref[:, c0:c0 + PROJ_CHUNK])
        consts = [constant(c0 + i * LANES) for i in range(PROJ_CHUNK // LANES)]
        if any(c is not zeros for c in consts):
            y = y + jnp.concatenate(consts, axis=1)
        o16_ref[:, c0:c0 + PROJ_CHUNK] = y.astype(BF16)


def _proj(x2, g, w32, wkc, w16, seq, tm=512):
    t, d = x2.shape
    n32 = w32.shape[1]
    assert wkc.shape[1] == 2 * LANES
    const = lambda a: pl.BlockSpec(a.shape, lambda i: (0, 0), pipeline_mode=pl.Buffered(1))
    return pl.pallas_call(
        functools.partial(_proj_kernel, seq=seq, tm=tm),
        grid=(t // tm,),
        in_specs=[pl.BlockSpec((tm, d), lambda i: (i, 0)), const(g), const(w32), const(wkc),
                  const(w16)],
        out_specs=[pl.BlockSpec((tm, n32), lambda i: (i, 0)),
                   pl.BlockSpec((tm, LANES), lambda i: (i, 0)),
                   pl.BlockSpec((tm, LANES), lambda i: (i, 0)),
                   pl.BlockSpec((tm, P16_N), lambda i: (i, 0))],
        out_shape=[jax.ShapeDtypeStruct((t, n32), F32), jax.ShapeDtypeStruct((t, LANES), F32),
                   jax.ShapeDtypeStruct((t, LANES), F32), jax.ShapeDtypeStruct((t, P16_N), BF16)],
        compiler_params=_cparams("parallel"),
        name="proj",
    )(x2, g, w32, wkc, w16)


def _compress_kernel(xk_ref, xv_ref, pe_ref, wa_ref, wb_ref, w2_ref, k_ref, v_ref):
    n_grp = xk_ref.shape[1] // CMP_STRIDE
    x = jnp.concatenate([ref[0, pl.ds(l, n_grp, stride=CMP_STRIDE), :]
                         for l in range(CMP_STRIDE) for ref in (xk_ref, xv_ref)], axis=1)
    a = _dot((x + pe_ref[0:1, :]).astype(BF16), wa_ref[...])
    b = _dot((x + pe_ref[1:2, :]).astype(BF16), wb_ref[...])
    hid = _silu(a + pltpu.roll(b, n_grp - 1, 0)).astype(BF16)
    row = lax.broadcasted_iota(jnp.int32, (n_grp, LANES), 0)
    lane = lax.broadcasted_iota(jnp.int32, (n_grp, LANES), 1)
    live = row < n_grp - 1
    for kv, out_ref in enumerate((k_ref, v_ref)):
        for g in range(NSA_GROUPS):
            c0 = (kv * NSA_GROUPS + g) * CMP_HIDDEN
            y = _dot(hid[:, c0:c0 + CMP_HIDDEN], w2_ref[kv])
            if kv == 1:
                y = y + jnp.where(lane == HEAD_DIM, 1.0, 0.0)
            out_ref[0, :, g * LANES:(g + 1) * LANES] = jnp.where(live, y, 0.0).astype(BF16)


def _compress(xk, xv, pe2, wa, wb, w2p):
    b, seq, width = xk.shape
    n_grp = seq // CMP_STRIDE
    out = jax.ShapeDtypeStruct((b, n_grp, NSA_GROUPS * LANES), BF16)
    return pl.pallas_call(
        _compress_kernel,
        grid=(b,),
        in_specs=[pl.BlockSpec((1, seq, width), lambda i: (i, 0, 0)),
                  pl.BlockSpec((1, seq, width), lambda i: (i, 0, 0)),
                  pl.BlockSpec(pe2.shape, lambda i: (0, 0)),
                  pl.BlockSpec(wa.shape, lambda i: (0, 0)),
                  pl.BlockSpec(wb.shape, lambda i: (0, 0)),
                  pl.BlockSpec(w2p.shape, lambda i: (0, 0, 0))],
        out_specs=[pl.BlockSpec((1, n_grp, NSA_GROUPS * LANES), lambda i: (i, 0, 0))] * 2,
        out_shape=[out, out],
        compiler_params=_cparams("parallel"),
        name="nsa_compress",
    )(xk, xv, pe2, wa, wb, w2p)


def _stack_heads(q_ref, g):
    parts = [q_ref[:, (g * NSA_HPG + h) * LANES:(g * NSA_HPG + h + 1) * LANES]
             for h in range(NSA_HPG)]
    return jnp.concatenate(parts, axis=0) * jnp.asarray(HEAD_DIM ** -0.5, BF16)


def _store_heads(o_ref, gate_ref, branch, g, o, accumulate=False):
    qt = o_ref.shape[0]
    lane = lax.broadcasted_iota(jnp.int32, (qt, LANES), 1)
    gate = _sigmoid(gate_ref[...])

    def scaled(h):
        col = (g * NSA_HPG + h) * 3 + branch
        return jnp.broadcast_to(gate[:, col:col + 1], (qt, LANES)) * o[h * qt:(h + 1) * qt]

    for j in range(NSA_HPG // 2):
        tile = jnp.where(lane < HEAD_DIM, scaled(2 * j), pltpu.roll(scaled(2 * j + 1), HEAD_DIM, 1))
        c0 = (g * NSA_HPG // 2 + j) * LANES
        o_ref[:, c0:c0 + LANES] = o_ref[:, c0:c0 + LANES] + tile if accumulate else tile


def _cmp_select_kernel(q_ref, gate_ref, kc_ref, vc_ref, cov_ref, kw_ref, vw_ref, o_ref, bias_ref,
                       imp_ref, *, top_k):
    n_cmp = kc_ref.shape[1]
    n_slc = cov_ref.shape[0]
    q0 = pl.program_id(1) * Q_TILE
    rows = NSA_HPG * Q_TILE
    sel_shape = (n_slc, NSA_GROUPS * Q_TILE)
    j_idx = lax.broadcasted_iota(jnp.int32, sel_shape, 0)
    t_lane = q0 + lax.broadcasted_iota(jnp.int32, sel_shape, 1) % Q_TILE
    t_blk = t_lane // SEL_BLOCK
    forced = (j_idx == 0) | (j_idx == t_blk) | (j_idx == t_blk - 1)
    causal_blk = j_idx <= t_blk

    def attend(width):
        t_row = q0 + lax.broadcasted_iota(jnp.int32, (Q_TILE, width), 0)
        cmp_end = lax.broadcasted_iota(jnp.int32, (Q_TILE, width), 1) * CMP_STRIDE + CMP_BLOCK - 1
        mask = jnp.tile(jnp.where(cmp_end <= t_row, 0.0, MASK_NEG), (NSA_HPG, 1))
        any_valid = jnp.tile(jnp.where(t_row[:, 0:1] >= CMP_BLOCK - 1, 1.0, 0.0), (NSA_HPG, 1))
        s_all = [_dot_nt(_stack_heads(q_ref, g), kc_ref[0, 0:width, g * LANES:(g + 1) * LANES])
                 for g in range(NSA_GROUPS)]
        for g in range(NSA_GROUPS):
            s = s_all[g] + mask
            p = jnp.exp(s - jnp.max(s, axis=-1, keepdims=True))
            l = jnp.sum(p, axis=-1, keepdims=True)
            p = p * (any_valid / l)
            o = _dot(p.astype(BF16), vc_ref[0, 0:width, g * LANES:(g + 1) * LANES])
            _store_heads(o_ref, gate_ref, 0, g, o, accumulate=True)
            p_sum = p[0:Q_TILE]
            for h in range(1, NSA_HPG):
                p_sum = p_sum + p[h * Q_TILE:(h + 1) * Q_TILE]
            hi, mid, lo = _split3(p_sum)
            cov = cov_ref[:, 0:width]
            imp_ref[:, g * Q_TILE:(g + 1) * Q_TILE] = (_dot_nt(cov, hi) + _dot_nt(cov, mid)
                                                       + _dot_nt(cov, lo))

    _window_attention(q_ref, gate_ref, kw_ref, vw_ref, o_ref, q0)
    for var in range(n_cmp // CMP_WIDTH_STEP):
        @pl.when(q0 // (CMP_WIDTH_STEP * CMP_STRIDE) == var)
        def _():
            attend((var + 1) * CMP_WIDTH_STEP)

    score = jnp.where(forced, -jnp.inf, jnp.where(causal_blk, imp_ref[...], -FORCE_BONUS))

    j_grp = j_idx[:, 0:Q_TILE]

    def pick(_, carry):
        out = []
        for sc in carry:
            best = jnp.max(sc, axis=0, keepdims=True)
            first = jnp.min(jnp.where(sc == best, j_grp, n_slc), axis=0, keepdims=True)
            out.append(jnp.where(j_grp == first, -jnp.inf, sc))
        return tuple(out)

    init = tuple(score[:, g * Q_TILE:(g + 1) * Q_TILE] for g in range(NSA_GROUPS))
    picked = lax.fori_loop(0, top_k - N_FORCED, pick, init)
    for g in range(NSA_GROUPS):
        chosen = (picked[g] == -jnp.inf) & causal_blk[:, g * Q_TILE:(g + 1) * Q_TILE]
        bias_ref[:, g * n_slc:(g + 1) * n_slc] = jnp.where(chosen, 0.0, MASK_NEG).T.astype(BF16)


def _cmp_select(p16, p32, gate_blk, kc, vc, cov_t, batch, seq):
    n_cmp = kc.shape[1]
    n_slc = cov_t.shape[0]
    nq = seq // Q_TILE
    w = NSA_GROUPS * LANES
    return pl.pallas_call(
        functools.partial(_cmp_select_kernel, top_k=min(SEL_TOPK, seq // SEL_BLOCK)),
        grid=(batch, nq),
        in_specs=[pl.BlockSpec((Q_TILE, NSA_HEADS * LANES), lambda b, i: (b * nq + i, 0)),
                  pl.BlockSpec((Q_TILE, LANES), lambda b, i: (b * nq + i, gate_blk)),
                  pl.BlockSpec((1, n_cmp, w), lambda b, i: (b, 0, 0)),
                  pl.BlockSpec((1, n_cmp, w), lambda b, i: (b, 0, 0)),
                  pl.BlockSpec(cov_t.shape, lambda b, i: (0, 0)),
                  pl.BlockSpec((seq, w), lambda b, i: (b, P16_KW // w)),
                  pl.BlockSpec((seq, w), lambda b, i: (b, P16_VW // w))],
        out_specs=[pl.BlockSpec((Q_TILE, NSA_HEADS * HEAD_DIM), lambda b, i: (b * nq + i, 0)),
                   pl.BlockSpec((Q_TILE, NSA_GROUPS * n_slc), lambda b, i: (b * nq + i, 0))],
        out_shape=[jax.ShapeDtypeStruct((batch * seq, NSA_HEADS * HEAD_DIM), F32),
                   jax.ShapeDtypeStruct((batch * seq, NSA_GROUPS * n_slc), BF16)],
        scratch_shapes=[pltpu.VMEM((n_slc, NSA_GROUPS * Q_TILE), F32)],
        compiler_params=_cparams("parallel", "arbitrary"),
        name="nsa_cmp_select",
    )(p16, p32, kc, vc, cov_t, p16, p16)


def _sel_attn_kernel(q_ref, gate_ref, bias_ref, kx_ref, vx_ref, o_ref, m_scr, acc_scr, *, tk):
    n_slc = bias_ref.shape[1] // NSA_GROUPS
    qt = q_ref.shape[0]
    q0 = pl.program_id(1) * qt
    rows = NSA_HPG * qt
    wide = 2 * tk
    n_wide = q0 // wide
    qx = []
    for g in range(NSA_GROUPS):
        bias = bias_ref[:, g * n_slc:(g + 1) * n_slc]
        qx.append(jnp.concatenate([_stack_heads(q_ref, g),
                                   jnp.concatenate([bias] * NSA_HPG, axis=0)], axis=1))

    def load():
        return tuple((m_scr[g], acc_scr[g]) for g in range(NSA_GROUPS))

    def store(state):
        for g in range(NSA_GROUPS):
            m_scr[g], acc_scr[g] = state[g]

    def scores(k0, width):
        return [_dot_nt(qx[g], kx_ref[pl.ds(k0, width), g * 2 * LANES:(g + 1) * 2 * LANES])
                for g in range(NSA_GROUPS)]

    def update(k0, width, diagonal, s_all, state):
        probs, m_all = [], []
        for g in range(NSA_GROUPS):
            s = s_all[g]
            if diagonal:
                t_row = q0 + lax.broadcasted_iota(jnp.int32, (qt, width), 0)
                key = k0 + lax.broadcasted_iota(jnp.int32, (qt, width), 1)
                s = s + jnp.tile(jnp.where(key <= t_row, 0.0, MASK_NEG), (NSA_HPG, 1))
            m_new = jnp.maximum(state[g][0], jnp.max(s, axis=-1, keepdims=True))
            probs.append(jnp.exp(s - jnp.tile(m_new, (1, width // LANES))).astype(BF16))
            m_all.append(m_new)
        out = []
        for g in range(NSA_GROUPS):
            m_old, acc = state[g]
            pv = _dot(probs[g], vx_ref[pl.ds(k0, width), g * LANES:(g + 1) * LANES])
            out.append((m_all[g], jnp.exp(m_old - m_all[g]) * acc + pv))
        return tuple(out)

    def step(k0, width, diagonal, state):
        return update(k0, width, diagonal, scores(k0, width), state)

    store(tuple((jnp.full((rows, LANES), MASK_NEG, F32), jnp.zeros((rows, LANES), F32))
                for _ in range(NSA_GROUPS)))

    def pair(j, _):
        k0 = pl.multiple_of(j * 2 * wide, 2 * wide)
        s_a, s_b = scores(k0, wide), scores(k0 + wide, wide)
        store(update(k0 + wide, wide, False, s_b, update(k0, wide, False, s_a, load())))
        return 0

    lax.fori_loop(0, n_wide // 2, pair, 0)

    @pl.when(n_wide % 2 == 1)
    def _():
        store(step(pl.multiple_of((n_wide - 1) * wide, wide), wide, False, load()))

    tail0 = pl.multiple_of(n_wide * wide, wide)
    tail_steps = (q0 + qt - tail0 + SEL_TAIL_STEP - 1) // SEL_TAIL_STEP
    for var in range(1, wide // SEL_TAIL_STEP + 1):
        @pl.when(tail_steps == var)
        def _():
            store(step(tail0, var * SEL_TAIL_STEP, True, load()))

    for g in range(NSA_GROUPS):
        acc = acc_scr[g]
        _store_heads(o_ref, gate_ref, 1, g, acc / acc[:, HEAD_DIM:HEAD_DIM + 1])


def _sel_attn(p16, p32, gate_blk, bias, batch, seq, qt=SEL_Q_TILE):
    nq = seq // qt
    tk = min(SEL_KEY_TILE, seq)
    kx_w = NSA_GROUPS * 2 * LANES
    vx_w = NSA_GROUPS * LANES
    return pl.pallas_call(
        functools.partial(_sel_attn_kernel, tk=tk),
        grid=(batch, nq),
        in_specs=[pl.BlockSpec((qt, NSA_HEADS * LANES), lambda b, i: (b * nq + i, 0)),
                  pl.BlockSpec((qt, LANES), lambda b, i: (b * nq + i, gate_blk)),
                  pl.BlockSpec((qt, bias.shape[1]), lambda b, i: (b * nq + i, 0)),
                  pl.BlockSpec((seq, kx_w), lambda b, i: (b, P16_KX // kx_w)),
                  pl.BlockSpec((seq, vx_w), lambda b, i: (b, P16_VX // vx_w))],
        out_specs=pl.BlockSpec((qt, NSA_HEADS * HEAD_DIM), lambda b, i: (b * nq + i, 0)),
        out_shape=jax.ShapeDtypeStruct((batch * seq, NSA_HEADS * HEAD_DIM), F32),
        scratch_shapes=[pltpu.VMEM((NSA_GROUPS, NSA_HPG * qt, LANES), F32),
                        pltpu.VMEM((NSA_GROUPS, NSA_HPG * qt, LANES), F32)],
        compiler_params=_cparams("parallel", "arbitrary"),
        name="nsa_sel_attn",
    )(p16, p32, bias, p16, p16)


def _window_attention(q_ref, gate_ref, kw_ref, vw_ref, o_ref, q0):
    seq = kw_ref.shape[0]
    span = min(WINDOW + Q_TILE, seq)
    start = pl.multiple_of(jnp.clip(q0 + Q_TILE - span, 0, seq - span), Q_TILE)
    t_row = q0 + lax.broadcasted_iota(jnp.int32, (Q_TILE, span), 0)
    pos = start + lax.broadcasted_iota(jnp.int32, (Q_TILE, span), 1)
    mask = jnp.tile(jnp.where((pos <= t_row) & (pos > t_row - WINDOW), 0.0, MASK_NEG), (NSA_HPG, 1))
    s_all = [_dot_nt(_stack_heads(q_ref, g), kw_ref[pl.ds(start, span), g * LANES:(g + 1) * LANES])
             for g in range(NSA_GROUPS)]
    p_all = []
    for g in range(NSA_GROUPS):
        s = s_all[g] + mask
        p_all.append(jnp.exp(s - jnp.max(s, axis=-1, keepdims=True)).astype(BF16))
    for g in range(NSA_GROUPS):
        acc = _dot(p_all[g], vw_ref[pl.ds(start, span), g * LANES:(g + 1) * LANES])
        _store_heads(o_ref, gate_ref, 2, g, acc / acc[:, HEAD_DIM:HEAD_DIM + 1])


def _spatial_gating_rows(u_ref, v_ref, g_ref, b_ref, w_ref, bexp_ref):
    c = SG_CHUNK
    tri = (lax.broadcasted_iota(jnp.int32, (c, c), 1) <= lax.broadcasted_iota(jnp.int32, (c, c), 0))
    lane = lax.broadcasted_iota(jnp.int32, (c, LANES), 1)
    wm = [jnp.where(tri, w_ref[g], 0.0).astype(BF16) for g in range(SG_GROUPS)]
    out = []
    for ci in range(u_ref.shape[0] // c):
        v = v_ref[ci * c:(ci + 1) * c, :]
        mu = jnp.mean(v, axis=-1, keepdims=True)
        d = v - mu
        var = jnp.mean(d * d, axis=-1, keepdims=True)
        vn = (d * lax.rsqrt(var + LN_EPS) * g_ref[...] + b_ref[...]).astype(BF16)
        mixed = []
        for pair in range(SG_GROUPS // 2):
            vp = vn[:, pair * LANES:(pair + 1) * LANES]
            mixed.append(jnp.where(lane < HEAD_DIM, _dot(wm[2 * pair], vp), _dot(wm[2 * pair + 1], vp)))
        out.append(u_ref[ci * c:(ci + 1) * c, :] * (jnp.concatenate(mixed, axis=1) + bexp_ref[...]))
    return jnp.concatenate(out, axis=0)


def _rwkv_kernel(r_ref, k_ref, v_ref, lo_ref, mu_r, mu_k, mu_v, mu_lo, w0_ref, w2_ref, a0_ref,
                 a2_ref, kk_ref, ka_ref, rk_ref, lng_ref, lnb_ref, seg_ref, o_ref,
                 state, prev_r, prev_k, prev_v, prev_lo, *, nch):
    c = RWKV_CHUNK
    rows = nch * c
    width = r_ref.shape[1]
    n_pairs = width // LANES

    @pl.when(pl.program_id(1) == 0)
    def _():
        state[...] = jnp.zeros_like(state)
        for ref in (prev_r, prev_k, prev_v, prev_lo):
            ref[...] = jnp.zeros_like(ref)

    def token_shift(x_ref, prev_ref, mu_ref):
        x = x_ref[...]
        first = lax.broadcasted_iota(jnp.int32, x.shape, 0) == 0
        prev = jnp.where(first, prev_ref[0:1, :], pltpu.roll(x, 1, 0))
        prev_ref[0:1, :] = x[rows - 1:rows, :]
        return x + (prev - x) * mu_ref[...]

    r = token_shift(r_ref, prev_r, mu_r)
    k = token_shift(k_ref, prev_k, mu_k)
    v = token_shift(v_ref, prev_v, mu_v)
    lo = token_shift(lo_ref, prev_lo, mu_lo)

    lane_lo = lax.broadcasted_iota(jnp.int32, lo.shape, 1)
    w_in = jnp.where(lane_lo < LORA, jnp.tanh(lo), 0.0).astype(BF16)
    a_in = jnp.where(lane_lo < LORA, 0.0, lo).astype(BF16)
    z = -(w0_ref[...] + _dot(w_in, w2_ref[...]))
    w = -(jnp.maximum(z, 0.0) + jnp.log(1.0 + jnp.exp(-jnp.abs(z)))) - 0.5
    a = _sigmoid(a0_ref[...] + _dot(a_in, a2_ref[...]))

    seg = seg_ref[...]
    kkf = k * kk_ref[...]
    kk = kkf / jnp.maximum(jnp.sqrt(_dot3_lhs(kkf * kkf, seg)), 1e-12)
    k = k * (1.0 + (a - 1.0) * ka_ref[...])
    logd = -jnp.exp(w)
    t_row = lax.broadcasted_iota(jnp.int32, (rows, rows), 0)
    t_col = lax.broadcasted_iota(jnp.int32, (rows, rows), 1)
    cum = (t_col <= t_row) & (t_col // c == t_row // c)
    cl = _dot3_rhs(jnp.where(cum, 1.0, 0.0).astype(BF16), logd)
    cl_end = cl.reshape(nch, c, width)[:, c - 1:c, :]
    e_neg = jnp.exp(-cl)
    e_end = jnp.exp(jnp.broadcast_to(cl_end, (nch, c, width)).reshape(rows, width) - cl)
    b = kk * a
    a_t = -kk * jnp.exp(cl - logd)
    b_t = b * e_neg
    k_t = k * e_neg
    r_t = r * jnp.exp(cl)
    b_e = b * e_end
    k_e = k * e_end
    p_end = jnp.exp(cl_end)

    lane = lax.broadcasted_iota(jnp.int32, (c, LANES), 1)
    row2 = lax.broadcasted_iota(jnp.int32, (2 * c, 2 * c), 0)
    col2 = lax.broadcasted_iota(jnp.int32, (2 * c, 2 * c), 1)
    same_head = (row2 // c) == (col2 // c)
    strict = same_head & (col2 < row2)
    incl = same_head & (col2 <= row2)
    eye = jnp.where(row2 == col2, 1.0, 0.0)
    zero16 = jnp.zeros((c, LANES), BF16)

    def stack(x, ci, pair):
        xp = x[ci * c:(ci + 1) * c, pair * LANES:(pair + 1) * LANES].astype(BF16)
        return jnp.concatenate([jnp.where(lane < HEAD_DIM, xp, zero16),
                                jnp.where(lane < HEAD_DIM, zero16, xp)], axis=0)

    def off(m):
        return (((row2 % (2 * m)) >= m) & ((col2 // (2 * m)) == (row2 // (2 * m)))
                & ((col2 % (2 * m)) < m))

    combos = [(ci, pair) for ci in range(nch) for pair in range(n_pairs)]
    a_st = {q: stack(a_t, *q) for q in combos}
    r_st = {q: stack(r_t, *q) for q in combos}
    v_st = {q: stack(v, *q) for q in combos}
    g = {q: _dot_nt(jnp.concatenate([a_st[q], r_st[q]], axis=0),
                    jnp.concatenate([stack(b_t, *q), stack(k_t, *q)], axis=0)) for q in combos}
    l_ab = {q: jnp.where(strict, g[q][0:2 * c, 0:2 * c], 0.0) for q in combos}
    inv = {q: eye + jnp.where(off(1), l_ab[q], 0.0) for q in combos}
    m = 2
    while m < c:
        inv16 = {q: inv[q].astype(BF16) for q in combos}
        x = {q: _dot(jnp.where(off(m), l_ab[q], 0.0).astype(BF16), inv16[q]) for q in combos}
        inv = {q: inv[q] + _dot(inv16[q], x[q].astype(BF16)) for q in combos}
        m *= 2
    lakv = {q: _dot(jnp.where(strict, g[q][0:2 * c, 2 * c:4 * c], 0.0).astype(BF16), v_st[q])
            for q in combos}
    wu = {q: _dot(inv[q].astype(BF16), jnp.concatenate([a_st[q], lakv[q].astype(BF16)], axis=1))
          for q in combos}
    m_rbk = {q: jnp.concatenate([jnp.where(incl, g[q][2 * c:4 * c, 0:2 * c], 0.0),
                                 jnp.where(incl, g[q][2 * c:4 * c, 2 * c:4 * c], 0.0)],
                                axis=1).astype(BF16) for q in combos}
    bke_st = {q: jnp.concatenate([stack(b_e, *q), stack(k_e, *q)], axis=0) for q in combos}

    st = [state[pair] for pair in range(n_pairs)]
    ys = [[None] * n_pairs for _ in range(nch)]
    for ci in range(nch):
        wr = {pair: _dot(jnp.concatenate([wu[ci, pair][:, 0:LANES].astype(BF16), r_st[ci, pair]],
                                         axis=0), st[pair].astype(BF16)) for pair in range(n_pairs)}
        for pair in range(n_pairs):
            q = (ci, pair)
            u_st = wr[pair][0:2 * c] + wu[q][:, LANES:2 * LANES]
            uv_st = jnp.concatenate([u_st.astype(BF16), v_st[q]], axis=0)
            y_st = wr[pair][2 * c:4 * c] + _dot(m_rbk[q], uv_st)
            decay_col = jnp.broadcast_to(p_end[ci, :, pair * LANES:(pair + 1) * LANES],
                                         (LANES, LANES)).T
            st[pair] = st[pair] * decay_col + _dot_tn(bke_st[q], uv_st)
            ys[ci][pair] = y_st[0:c] + y_st[c:2 * c]
    for pair in range(n_pairs):
        state[pair] = st[pair]
    y = jnp.concatenate([jnp.concatenate(yr, axis=1) for yr in ys], axis=0)

    seg_mean = (seg * (1.0 / HEAD_DIM)).astype(BF16)
    mean = _dot3_lhs(y, seg_mean)
    d = y - mean
    var = _dot3_lhs(d * d, seg_mean)
    yn = d * lax.rsqrt(var + GN_EPS) * lng_ref[...] + lnb_ref[...]
    bonus = _dot3_lhs(r * k * rk_ref[...], seg) * v
    o_ref[...] = yn + bonus


def _rwkv(p32, params, batch, seq, r_blk, k_blk, v_blk, lo_blk, nch=4):
    width = params["w0"].shape[1]
    c = nch * RWKV_CHUNK
    nc = seq // c
    full = lambda a: pl.BlockSpec(a.shape, lambda b, i: (0,) * a.ndim)
    names = ("mu_r", "mu_k", "mu_v", "mu_lo", "w0", "w2", "a0", "a2", "kk", "ka", "rk",
             "lng", "lnb", "seg")
    return pl.pallas_call(
        functools.partial(_rwkv_kernel, nch=nch),
        grid=(batch, nc),
        in_specs=[pl.BlockSpec((c, width), lambda b, i: (b * nc + i, r_blk)),
                  pl.BlockSpec((c, width), lambda b, i: (b * nc + i, k_blk)),
                  pl.BlockSpec((c, width), lambda b, i: (b * nc + i, v_blk)),
                  pl.BlockSpec((c, LANES), lambda b, i: (b * nc + i, lo_blk))]
                 + [full(params[n]) for n in names],
        out_specs=pl.BlockSpec((c, width), lambda b, i: (b * nc + i, 0)),
        out_shape=jax.ShapeDtypeStruct((batch * seq, width), F32),
        scratch_shapes=[pltpu.VMEM((width // LANES, LANES, LANES), F32),
                        pltpu.VMEM((8, width), F32), pltpu.VMEM((8, width), F32),
                        pltpu.VMEM((8, width), F32), pltpu.VMEM((8, LANES), F32)],
        compiler_params=_cparams("parallel", "arbitrary"),
        name="rwkv7",
    )(p32, p32, p32, p32, *[params[n] for n in names])


def _merge_kernel(ocw_ref, os_ref, su_ref, sv_ref, rk_ref, x_ref, p_ref, g_ref, wl_ref,
                  sgg_ref, sgb_ref, sgw_ref, bexp_ref,
                  wb0_ref, wb1_ref, wb2_ref, wo_ref, pleg_ref, wpg_ref, wpp_ref, fin_ref,
                  o_ref, *, last):
    d = x_ref.shape[1]
    half = rk_ref.shape[1]
    nzw = os_ref.shape[1]
    h = _rmsnorm(x_ref[...], g_ref[...]).astype(BF16)
    late = lambda c0, n: _dot(h, wl_ref[:, c0:c0 + n])
    y_nsa = ocw_ref[...] + os_ref[...]
    z0 = _dot((y_nsa * _silu(late(3 * d, nzw))).astype(BF16), wb0_ref[...])
    merged = _sigmoid(late(0, d)) * z0
    y_sg = _spatial_gating_rows(su_ref, sv_ref, sgg_ref, sgb_ref, sgw_ref, bexp_ref)
    z1 = _dot((y_sg * _silu(late(3 * d + nzw, half))).astype(BF16), wb1_ref[...])
    merged = merged + _sigmoid(late(d, d)) * z1
    z2 = _dot((rk_ref[...] * _silu(late(3 * d + nzw + half, half))).astype(BF16), wb2_ref[...])
    merged = merged + _sigmoid(late(2 * d, d)) * z2
    x1 = x_ref[...] + _dot(merged.astype(BF16), wo_ref[...])
    hp = _rmsnorm(x1, pleg_ref[...]).astype(BF16)
    x2 = x1 + _sigmoid(_dot(hp, wpg_ref[...])) * _dot(p_ref[...].astype(BF16), wpp_ref[...])
    o_ref[...] = _rmsnorm(x2, fin_ref[...]) if last else x2


def _merge(o_cw, o_s, p32, blocks, y_rk, x2, p2, consts, last, tm=256):
    t, d = x2.shape
    half = y_rk.shape[1]
    tok = lambda w, j=0: pl.BlockSpec((tm, w), lambda i, j=j: (i, j))
    full = lambda a: pl.BlockSpec(a.shape, lambda i: (0,) * a.ndim, pipeline_mode=pl.Buffered(1))
    names = ("g", "wl", "sgg", "sgb", "sgw", "bexp", "wb0", "wb1", "wb2", "wo", "pleg", "wpg",
             "wpp", "fin")
    return pl.pallas_call(
        functools.partial(_merge_kernel, last=last),
        grid=(t // tm,),
        in_specs=[tok(o_cw.shape[1]), tok(o_s.shape[1]), tok(half, blocks["su"]),
                  tok(half, blocks["sv"]), tok(half), tok(d), tok(p2.shape[1])]
                 + [full(consts[n]) for n in names],
        out_specs=tok(d),
        out_shape=jax.ShapeDtypeStruct((t, d), F32),
        compiler_params=_cparams("parallel"),
        name="merge_out_ple",
    )(o_cw, o_s, p32, p32, y_rk, x2, p2, *[consts[n] for n in names])


def _pad_cols(w, n):
    return jnp.pad(w, ((0, 0), (0, n - w.shape[1])))


def _layer_layout(d_model):
    half = d_model // 2
    kvw = NSA_GROUPS * HEAD_DIM
    sizes = dict(nq=half, nkv=6 * kvw, ngate=3 * NSA_HEADS, nz=half, su=half, sv=half, sz=half,
                 rs=3 * half + 2 * LORA, rz=half, mg=3 * d_model)
    off, out = 0, {}
    for name, n in sizes.items():
        out[name] = (off, n)
        off += n
    return out


def _prep_proj_weights(w_in, d_model):
    lay = _layer_layout(d_model)
    half = d_model // 2
    kvw = NSA_GROUPS * HEAD_DIM
    col = lambda name, a=0, n=None: w_in[:, lay[name][0] + a: lay[name][0] + a + (n or lay[name][1] - a)]
    heads = lambda w: jnp.concatenate(
        [_pad_cols(w[:, h * HEAD_DIM:(h + 1) * HEAD_DIM], LANES) for h in range(w.shape[1] // HEAD_DIM)], axis=1)
    w32 = jnp.concatenate([
        col("su"), col("sv"), col("rs", 0, half), col("rs", half, half), col("rs", 2 * half, half),
        col("rs", 3 * half, 2 * LORA), _pad_cols(col("ngate"), LANES)], axis=1)
    w_kcvc = col("nkv", 0, 2 * kvw)
    blocks, o = {}, 0
    for name in ("su", "sv", "rr", "rk", "rv"):
        blocks[name] = o // half
        o += half
    blocks["lora"] = o // LANES
    o += LANES
    blocks["gate"] = o // LANES
    w_late = jnp.concatenate([col("mg"), col("nz"), col("sz"), col("rz")], axis=1)
    zero = jnp.zeros((w_in.shape[0], LANES), w_in.dtype)
    grp = lambda w, g: _pad_cols(w[:, g * HEAD_DIM:(g + 1) * HEAD_DIM], LANES)
    ks, vs = col("nkv", 2 * kvw, kvw), col("nkv", 3 * kvw, kvw)
    kw, vw = col("nkv", 4 * kvw, kvw), col("nkv", 5 * kvw, kvw)
    w16 = jnp.concatenate(
        [heads(col("nq"))]
        + [x for g in range(NSA_GROUPS) for x in (grp(ks, g), zero)]
        + [grp(vs, g) for g in range(NSA_GROUPS)]
        + [grp(kw, g) for g in range(NSA_GROUPS)]
        + [grp(vw, g) for g in range(NSA_GROUPS)], axis=1)
    return w32.astype(BF16), w_kcvc.astype(BF16), w16.astype(BF16), w_late.astype(BF16), blocks


def _prep_compress_weights(cmp_w1, cmp_w2, cmp_pe):
    half_blk = CMP_BLOCK // 2
    w1r = cmp_w1.reshape(2, 2, half_blk, HEAD_DIM, CMP_HIDDEN)
    eye = jnp.eye(NSA_GROUPS, dtype=cmp_w1.dtype)
    eye_kv = jnp.eye(2, dtype=cmp_w1.dtype)
    big = jnp.einsum('kaldh,kK,gG->alkgdKGh', w1r, eye_kv, eye)
    big = big.reshape(2, half_blk * 2 * NSA_GROUPS * HEAD_DIM, 2 * NSA_GROUPS * CMP_HIDDEN)
    pe = cmp_pe.reshape(2, 2, half_blk, HEAD_DIM)
    pe2 = jnp.broadcast_to(pe.transpose(1, 2, 0, 3)[:, :, :, None, :],
                           (2, half_blk, 2, NSA_GROUPS, HEAD_DIM)).reshape(2, -1)
    w2p = jnp.pad(cmp_w2, ((0, 0), (0, 0), (0, LANES - HEAD_DIM)))
    return pe2, big[0].astype(BF16), big[1].astype(BF16), w2p.astype(BF16)


def _cover_t(seq):
    n_grp = seq // CMP_STRIDE
    n_slc = seq // SEL_BLOCK
    cmp_start = jnp.arange(n_grp) * CMP_STRIDE
    slc_start = jnp.arange(n_slc) * SEL_BLOCK
    cover = ((cmp_start[None, :] <= slc_start[:, None] + SEL_BLOCK - 1)
             & (cmp_start[None, :] + CMP_BLOCK - 1 >= slc_start[:, None]))
    return jnp.pad(cover, ((0, LANES - n_slc), (0, 0))).astype(BF16)


def kernel(x, p, norm_g, w_in, cmp_w1, cmp_w2, cmp_pe, sg_ln_g, sg_ln_b, sg_w, sg_b, rk_mu, rk_w0,
           rk_w2, rk_a0, rk_a2, rk_kk, rk_ka, rk_rk, rk_lnx_g, rk_lnx_b, w_branch, w_o, ple_norm_g,
           w_ple_gate, w_ple_proj, final_norm_g):
    batch, seq, d = x.shape
    depth = w_in.shape[0]
    half = d // 2
    t = batch * seq
    assert seq % (2 * SEL_KEY_TILE) == 0 and (seq // CMP_STRIDE) % CMP_WIDTH_STEP == 0
    assert seq >= WINDOW + Q_TILE and seq // SEL_BLOCK <= LANES
    x2 = x.reshape(t, d)
    cov_t = _cover_t(seq)
    seg = (jnp.arange(half)[:, None] // HEAD_DIM == jnp.arange(half)[None, :] // HEAD_DIM).astype(BF16)
    row = lambda v: v.reshape(1, -1)
    for i in range(depth):
        w32, w_kcvc, w16, w_late, blk = _prep_proj_weights(w_in[i], d)
        g = row(norm_g[i])
        p32, kc_raw, vc_raw, p16 = _proj(x2, g, w32, w_kcvc, w16, seq)

        pe2, wa, wb, w2p = _prep_compress_weights(cmp_w1[i], cmp_w2[i], cmp_pe[i])
        kc, vc = _compress(kc_raw.reshape(batch, seq, -1), vc_raw.reshape(batch, seq, -1),
                           pe2, wa, wb, w2p)
        o_cw, bias = _cmp_select(p16, p32, blk["gate"], kc, vc, cov_t, batch, seq)
        o_s = _sel_attn(p16, p32, blk["gate"], bias, batch, seq)

        mu = rk_mu[i]
        rk_params = dict(
            mu_r=row(mu[0:half]), mu_k=row(mu[half:2 * half]), mu_v=row(mu[2 * half:3 * half]),
            mu_lo=row(mu[3 * half:]), w0=row(rk_w0[i]),
            w2=jnp.pad(rk_w2[i], ((0, LANES - LORA), (0, 0))).astype(BF16), a0=row(rk_a0[i]),
            a2=jnp.pad(rk_a2[i], ((LANES - LORA, 0), (0, 0))).astype(BF16),
            kk=row(rk_kk[i]), ka=row(rk_ka[i]), rk=row(rk_rk[i]), lng=row(rk_lnx_g[i]),
            lnb=row(rk_lnx_b[i]), seg=seg)
        y_rk = _rwkv(p32, rk_params, batch, seq, blk["rr"], blk["rk"], blk["rv"], blk["lora"])

        consts = dict(g=g, wl=w_late, sgg=row(sg_ln_g[i]), sgb=row(sg_ln_b[i]), sgw=sg_w[i],
                      bexp=jnp.repeat(sg_b[i].T, half // SG_GROUPS, axis=1),
                      wb0=w_branch[i, 0].astype(BF16), wb1=w_branch[i, 1].astype(BF16),
                      wb2=w_branch[i, 2].astype(BF16), wo=w_o[i].astype(BF16),
                      pleg=row(ple_norm_g[i]), wpg=w_ple_gate[i].astype(BF16),
                      wpp=w_ple_proj[i].astype(BF16), fin=row(final_norm_g))
        x2 = _merge(o_cw, o_s, p32, blk, y_rk, x2, p[i].reshape(t, -1), consts,
                    last=(i == depth - 1))
    return x2.reshape(batch, seq, d)
```

```python
import functools

import jax
import jax.numpy as jnp
from jax import lax
from jax.experimental import pallas as pl
from jax.experimental.pallas import tpu as pltpu

F32 = jnp.float32
BF16 = jnp.bfloat16

HEAD_DIM = 64
LANES = 128
NSA_HEADS = 8
NSA_GROUPS = 2
NSA_HPG = NSA_HEADS // NSA_GROUPS
CMP_BLOCK = 32
CMP_STRIDE = 16
CMP_HIDDEN = 128
CMP_WIDTH_STEP = 128
SEL_BLOCK = 64
SEL_TOPK = 16
N_FORCED = 3
WINDOW = 512
Q_TILE = 128
SEL_Q_TILE = 256
SEL_KEY_TILE = 512
SEL_TAIL_STEP = 256
SG_GROUPS = 8
SG_CHUNK = 128
RWKV_HEADS = 8
RWKV_CHUNK = 64
LORA = 64
NORM_EPS = 1e-6
LN_EPS = 1e-5
GN_EPS = 64e-5
MASK_NEG = -1e30
FORCE_BONUS = 1e4
VMEM_LIMIT = 56 * 1024 * 1024


def _cparams(*sem):
    return pltpu.CompilerParams(dimension_semantics=sem, vmem_limit_bytes=VMEM_LIMIT)


def _dot(a, b, precision=None):
    return jnp.dot(a, b, preferred_element_type=F32, precision=precision)


def _dot_nt(a, b, precision=None):
    return lax.dot_general(a, b, (((1,), (1,)), ((), ())), preferred_element_type=F32,
                           precision=precision)


def _dot_tn(a, b, precision=None):
    return lax.dot_general(a, b, (((0,), (0,)), ((), ())), preferred_element_type=F32,
                           precision=precision)


def _split3(x):
    hi = x.astype(BF16)
    r1 = x - hi.astype(F32)
    mid = r1.astype(BF16)
    lo = (r1 - mid.astype(F32)).astype(BF16)
    return hi, mid, lo


def _dot3_lhs(x, w):
    hi, mid, lo = _split3(x)
    return _dot(hi, w) + _dot(mid, w) + _dot(lo, w)


def _dot3_rhs(w, x):
    hi, mid, lo = _split3(x)
    return _dot(w, hi) + _dot(w, mid) + _dot(w, lo)


def _sigmoid(x):
    return 1.0 / (1.0 + jnp.exp(-x))


def _silu(x):
    return x * _sigmoid(x)


def _rmsnorm(x, g):
    return x * lax.rsqrt(jnp.mean(x * x, axis=-1, keepdims=True) + NORM_EPS) * g


P16_Q = 0
P16_KX = NSA_HEADS * LANES
P16_VX = P16_KX + NSA_GROUPS * 2 * LANES
P16_KW = P16_VX + NSA_GROUPS * LANES
P16_VW = P16_KW + NSA_GROUPS * LANES
P16_N = P16_VW + NSA_GROUPS * LANES


PROJ_CHUNK = 2 * LANES


def _proj_kernel(x_ref, g_ref, w32_ref, wkc_ref, w16_ref, o32_ref, okc_ref, ovc_ref, o16_ref, *,
                 seq, tm):
    h = _rmsnorm(x_ref[...], g_ref[...]).astype(BF16)
    for c0 in range(0, o32_ref.shape[1], PROJ_CHUNK):
        o32_ref[:, c0:c0 + PROJ_CHUNK] = _dot(h, w32_ref[:, c0:c0 + PROJ_CHUNK])
    kcvc = _dot(h, wkc_ref[...])
    okc_ref[...] = kcvc[:, 0:LANES]
    ovc_ref[...] = kcvc[:, LANES:2 * LANES]
    pos = (pl.program_id(0) * tm) % seq + lax.broadcasted_iota(jnp.int32, (tm, LANES), 0)
    lane = lax.broadcasted_iota(jnp.int32, (tm, LANES), 1)
    onehot_blk = jnp.where(pos // SEL_BLOCK == lane, 1.0, 0.0)
    ones_col = jnp.where(lane == HEAD_DIM, 1.0, 0.0)
    zeros = jnp.zeros((tm, LANES), F32)

    def constant(c0):
        if P16_KX <= c0 < P16_VX and ((c0 - P16_KX) // LANES) % 2 == 1:
            return onehot_blk
        if c0 >= P16_VX and not (P16_KW <= c0 < P16_VW):
            return ones_col
        return zeros

    for c0 in range(0, P16_N, PROJ_CHUNK):
        y = _dot(h, w16_ref[:, c0:c0 + PROJ_CHUNK])
        consts = [constant(c0 + i * LANES) for i in range(PROJ_CHUNK // LANES)]
        if any(c is not zeros for c in consts):
            y = y + jnp.concatenate(consts, axis=1)
        o16_ref[:, c0:c0 + PROJ_CHUNK] = y.astype(BF16)


def _proj(x2, g, w32, wkc, w16, seq, tm=512):
    t, d = x2.shape
    n32 = w32.shape[1]
    assert wkc.shape[1] == 2 * LANES
    const = lambda a: pl.BlockSpec(a.shape, lambda i: (0, 0), pipeline_mode=pl.Buffered(1))
    return pl.pallas_call(
        functools.partial(_proj_kernel, seq=seq, tm=tm),
        grid=(t // tm,),
        in_specs=[pl.BlockSpec((tm, d), lambda i: (i, 0)), const(g), const(w32), const(wkc),
                  const(w16)],
        out_specs=[pl.BlockSpec((tm, n32), lambda i: (i, 0)),
                   pl.BlockSpec((tm, LANES), lambda i: (i, 0)),
                   pl.BlockSpec((tm, LANES), lambda i: (i, 0)),
                   pl.BlockSpec((tm, P16_N), lambda i: (i, 0))],
        out_shape=[jax.ShapeDtypeStruct((t, n32), F32), jax.ShapeDtypeStruct((t, LANES), F32),
                   jax.ShapeDtypeStruct((t, LANES), F32), jax.ShapeDtypeStruct((t, P16_N), BF16)],
        compiler_params=_cparams("parallel"),
        name="proj",
    )(x2, g, w32, wkc, w16)


def _compress_kernel(xk_ref, xv_ref, pe_ref, wa_ref, wb_ref, w2_ref, k_ref, v_ref):
    n_grp = xk_ref.shape[1] // CMP_STRIDE
    x = jnp.concatenate([ref[0, pl.ds(l, n_grp, stride=CMP_STRIDE), :]
                         for l in range(CMP_STRIDE) for ref in (xk_ref, xv_ref)], axis=1)
    a = _dot((x + pe_ref[0:1, :]).astype(BF16), wa_ref[...])
    b = _dot((x + pe_ref[1:2, :]).astype(BF16), wb_ref[...])
    hid = _silu(a + pltpu.roll(b, n_grp - 1, 0)).astype(BF16)
    row = lax.broadcasted_iota(jnp.int32, (n_grp, LANES), 0)
    lane = lax.broadcasted_iota(jnp.int32, (n_grp, LANES), 1)
    live = row < n_grp - 1
    for kv, out_ref in enumerate((k_ref, v_ref)):
        for g in range(NSA_GROUPS):
            c0 = (kv * NSA_GROUPS + g) * CMP_HIDDEN
            y = _dot(hid[:, c0:c0 + CMP_HIDDEN], w2_ref[kv])
            if kv == 1:
                y = y + jnp.where(lane == HEAD_DIM, 1.0, 0.0)
            out_ref[0, :, g * LANES:(g + 1) * LANES] = jnp.where(live, y, 0.0).astype(BF16)


def _compress(xk, xv, pe2, wa, wb, w2p):
    b, seq, width = xk.shape
    n_grp = seq // CMP_STRIDE
    out = jax.ShapeDtypeStruct((b, n_grp, NSA_GROUPS * LANES), BF16)
    return pl.pallas_call(
        _compress_kernel,
        grid=(b,),
        in_specs=[pl.BlockSpec((1, seq, width), lambda i: (i, 0, 0)),
                  pl.BlockSpec((1, seq, width), lambda i: (i, 0, 0)),
                  pl.BlockSpec(pe2.shape, lambda i: (0, 0)),
                  pl.BlockSpec(wa.shape, lambda i: (0, 0)),
                  pl.BlockSpec(wb.shape, lambda i: (0, 0)),
                  pl.BlockSpec(w2p.shape, lambda i: (0, 0, 0))],
        out_specs=[pl.BlockSpec((1, n_grp, NSA_GROUPS * LANES), lambda i: (i, 0, 0))] * 2,
        out_shape=[out, out],
        compiler_params=_cparams("parallel"),
        name="nsa_compress",
    )(xk, xv, pe2, wa, wb, w2p)


def _stack_heads(q_ref, g):
    parts = [q_ref[:, (g * NSA_HPG + h) * LANES:(g * NSA_HPG + h + 1) * LANES]
             for h in range(NSA_HPG)]
    return jnp.concatenate(parts, axis=0) * jnp.asarray(HEAD_DIM ** -0.5, BF16)


def _store_heads(o_ref, gate_ref, branch, g, o, accumulate=False):
    qt = o_ref.shape[0]
    lane = lax.broadcasted_iota(jnp.int32, (qt, LANES), 1)
    gate = _sigmoid(gate_ref[...])

    def scaled(h):
        col = (g * NSA_HPG + h) * 3 + branch
        return jnp.broadcast_to(gate[:, col:col + 1], (qt, LANES)) * o[h * qt:(h + 1) * qt]

    for j in range(NSA_HPG // 2):
        tile = jnp.where(lane < HEAD_DIM, scaled(2 * j), pltpu.roll(scaled(2 * j + 1), HEAD_DIM, 1))
        c0 = (g * NSA_HPG // 2 + j) * LANES
        o_ref[:, c0:c0 + LANES] = o_ref[:, c0:c0 + LANES] + tile if accumulate else tile


def _cmp_select_kernel(q_ref, gate_ref, kc_ref, vc_ref, cov_ref, kw_ref, vw_ref, o_ref, bias_ref,
                       imp_ref, *, top_k):
    n_cmp = kc_ref.shape[1]
    n_slc = cov_ref.shape[0]
    q0 = pl.program_id(1) * Q_TILE
    rows = NSA_HPG * Q_TILE
    sel_shape = (n_slc, NSA_GROUPS * Q_TILE)
    j_idx = lax.broadcasted_iota(jnp.int32, sel_shape, 0)
    t_lane = q0 + lax.broadcasted_iota(jnp.int32, sel_shape, 1) % Q_TILE
    t_blk = t_lane // SEL_BLOCK
    forced = (j_idx == 0) | (j_idx == t_blk) | (j_idx == t_blk - 1)
    causal_blk = j_idx <= t_blk

    def attend(width):
        t_row = q0 + lax.broadcasted_iota(jnp.int32, (Q_TILE, width), 0)
        cmp_end = lax.broadcasted_iota(jnp.int32, (Q_TILE, width), 1) * CMP_STRIDE + CMP_BLOCK - 1
        mask = jnp.tile(jnp.where(cmp_end <= t_row, 0.0, MASK_NEG), (NSA_HPG, 1))
        any_valid = jnp.tile(jnp.where(t_row[:, 0:1] >= CMP_BLOCK - 1, 1.0, 0.0), (NSA_HPG, 1))
        s_all = [_dot_nt(_stack_heads(q_ref, g), kc_ref[0, 0:width, g * LANES:(g + 1) * LANES])
                 for g in range(NSA_GROUPS)]
        for g in range(NSA_GROUPS):
            s = s_all[g] + mask
            p = jnp.exp(s - jnp.max(s, axis=-1, keepdims=True))
            l = jnp.sum(p, axis=-1, keepdims=True)
            p = p * (any_valid / l)
            o = _dot(p.astype(BF16), vc_ref[0, 0:width, g * LANES:(g + 1) * LANES])
            _store_heads(o_ref, gate_ref, 0, g, o, accumulate=True)
            p_sum = p[0:Q_TILE]
            for h in range(1, NSA_HPG):
                p_sum = p_sum + p[h * Q_TILE:(h + 1) * Q_TILE]
            hi, mid, lo = _split3(p_sum)
            cov = cov_ref[:, 0:width]
            imp_ref[:, g * Q_TILE:(g + 1) * Q_TILE] = (_dot_nt(cov, hi) + _dot_nt(cov, mid)
                                                       + _dot_nt(cov, lo))

    _window_attention(q_ref, gate_ref, kw_ref, vw_ref, o_ref, q0)
    for var in range(n_cmp // CMP_WIDTH_STEP):
        @pl.when(q0 // (CMP_WIDTH_STEP * CMP_STRIDE) == var)
        def _():
            attend((var + 1) * CMP_WIDTH_STEP)

    score = jnp.where(forced, -jnp.inf, jnp.where(causal_blk, imp_ref[...], -FORCE_BONUS))

    j_grp = j_idx[:, 0:Q_TILE]

    def pick(_, carry):
        out = []
        for sc in carry:
            best = jnp.max(sc, axis=0, keepdims=True)
            first = jnp.min(jnp.where(sc == best, j_grp, n_slc), axis=0, keepdims=True)
            out.append(jnp.where(j_grp == first, -jnp.inf, sc))
        return tuple(out)

    init = tuple(score[:, g * Q_TILE:(g + 1) * Q_TILE] for g in range(NSA_GROUPS))
    picked = lax.fori_loop(0, top_k - N_FORCED, pick, init)
    for g in range(NSA_GROUPS):
        chosen = (picked[g] == -jnp.inf) & causal_blk[:, g * Q_TILE:(g + 1) * Q_TILE]
        bias_ref[:, g * n_slc:(g + 1) * n_slc] = jnp.where(chosen, 0.0, MASK_NEG).T.astype(BF16)


def _cmp_select(p16, p32, gate_blk, kc, vc, cov_t, batch, seq):
    n_cmp = kc.shape[1]
    n_slc = cov_t.shape[0]
    nq = seq // Q_TILE
    w = NSA_GROUPS * LANES
    return pl.pallas_call(
        functools.partial(_cmp_select_kernel, top_k=min(SEL_TOPK, seq // SEL_BLOCK)),
        grid=(batch, nq),
        in_specs=[pl.BlockSpec((Q_TILE, NSA_HEADS * LANES), lambda b, i: (b * nq + i, 0)),
                  pl.BlockSpec((Q_TILE, LANES), lambda b, i: (b * nq + i, gate_blk)),
                  pl.BlockSpec((1, n_cmp, w), lambda b, i: (b, 0, 0)),
                  pl.BlockSpec((1, n_cmp, w), lambda b, i: (b, 0, 0)),
                  pl.BlockSpec(cov_t.shape, lambda b, i: (0, 0)),
                  pl.BlockSpec((seq, w), lambda b, i: (b, P16_KW // w)),
                  pl.BlockSpec((seq, w), lambda b, i: (b, P16_VW // w))],
        out_specs=[pl.BlockSpec((Q_TILE, NSA_HEADS * HEAD_DIM), lambda b, i: (b * nq + i, 0)),
                   pl.BlockSpec((Q_TILE, NSA_GROUPS * n_slc), lambda b, i: (b * nq + i, 0))],
        out_shape=[jax.ShapeDtypeStruct((batch * seq, NSA_HEADS * HEAD_DIM), F32),
                   jax.ShapeDtypeStruct((batch * seq, NSA_GROUPS * n_slc), BF16)],
        scratch_shapes=[pltpu.VMEM((n_slc, NSA_GROUPS * Q_TILE), F32)],
        compiler_params=_cparams("parallel", "arbitrary"),
        name="nsa_cmp_select",
    )(p16, p32, kc, vc, cov_t, p16, p16)


def _sel_attn_kernel(q_ref, gate_ref, bias_ref, kx_ref, vx_ref, o_ref, m_scr, acc_scr, *, tk):
    n_slc = bias_ref.shape[1] // NSA_GROUPS
    qt = q_ref.shape[0]
    q0 = pl.program_id(1) * qt
    rows = NSA_HPG * qt
    wide = 2 * tk
    n_wide = q0 // wide
    qx = []
    for g in range(NSA_GROUPS):
        bias = bias_ref[:, g * n_slc:(g + 1) * n_slc]
        qx.append(jnp.concatenate([_stack_heads(q_ref, g),
                                   jnp.concatenate([bias] * NSA_HPG, axis=0)], axis=1))

    def load():
        return tuple((m_scr[g], acc_scr[g]) for g in range(NSA_GROUPS))

    def store(state):
        for g in range(NSA_GROUPS):
            m_scr[g], acc_scr[g] = state[g]

    def scores(k0, width):
        return [_dot_nt(qx[g], kx_ref[pl.ds(k0, width), g * 2 * LANES:(g + 1) * 2 * LANES])
                for g in range(NSA_GROUPS)]

    def update(k0, width, diagonal, s_all, state):
        probs, m_all = [], []
        for g in range(NSA_GROUPS):
            s = s_all[g]
            if diagonal:
                t_row = q0 + lax.broadcasted_iota(jnp.int32, (qt, width), 0)
                key = k0 + lax.broadcasted_iota(jnp.int32, (qt, width), 1)
                s = s + jnp.tile(jnp.where(key <= t_row, 0.0, MASK_NEG), (NSA_HPG, 1))
            m_new = jnp.maximum(state[g][0], jnp.max(s, axis=-1, keepdims=True))
            probs.append(jnp.exp(s - jnp.tile(m_new, (1, width // LANES))).astype(BF16))
            m_all.append(m_new)
        out = []
        for g in range(NSA_GROUPS):
            m_old, acc = state[g]
            pv = _dot(probs[g], vx_ref[pl.ds(k0, width), g * LANES:(g + 1) * LANES])
            out.append((m_all[g], jnp.exp(m_old - m_all[g]) * acc + pv))
        return tuple(out)

    def step(k0, width, diagonal, state):
        return update(k0, width, diagonal, scores(k0, width), state)

    store(tuple((jnp.full((rows, LANES), MASK_NEG, F32), jnp.zeros((rows, LANES), F32))
                for _ in range(NSA_GROUPS)))

    def pair(j, _):
        k0 = pl.multiple_of(j * 2 * wide, 2 * wide)
        s_a, s_b = scores(k0, wide), scores(k0 + wide, wide)
        store(update(k0 + wide, wide, False, s_b, update(k0, wide, False, s_a, load())))
        return 0

    lax.fori_loop(0, n_wide // 2, pair, 0)

    @pl.when(n_wide % 2 == 1)
    def _():
        store(step(pl.multiple_of((n_wide - 1) * wide, wide), wide, False, load()))

    tail0 = pl.multiple_of(n_wide * wide, wide)
    tail_steps = (q0 + qt - tail0 + SEL_TAIL_STEP - 1) // SEL_TAIL_STEP
    for var in range(1, wide // SEL_TAIL_STEP + 1):
        @pl.when(tail_steps == var)
        def _():
            store(step(tail0, var * SEL_TAIL_STEP, True, load()))

    for g in range(NSA_GROUPS):
        acc = acc_scr[g]
        _store_heads(o_ref, gate_ref, 1, g, acc / acc[:, HEAD_DIM:HEAD_DIM + 1])


def _sel_attn(p16, p32, gate_blk, bias, batch, seq, qt=SEL_Q_TILE):
    nq = seq // qt
    tk = min(SEL_KEY_TILE, seq)
    kx_w = NSA_GROUPS * 2 * LANES
    vx_w = NSA_GROUPS * LANES
    return pl.pallas_call(
        functools.partial(_sel_attn_kernel, tk=tk),
        grid=(batch, nq),
        in_specs=[pl.BlockSpec((qt, NSA_HEADS * LANES), lambda b, i: (b * nq + i, 0)),
                  pl.BlockSpec((qt, LANES), lambda b, i: (b * nq + i, gate_blk)),
                  pl.BlockSpec((qt, bias.shape[1]), lambda b, i: (b * nq + i, 0)),
                  pl.BlockSpec((seq, kx_w), lambda b, i: (b, P16_KX // kx_w)),
                  pl.BlockSpec((seq, vx_w), lambda b, i: (b, P16_VX // vx_w))],
        out_specs=pl.BlockSpec((qt, NSA_HEADS * HEAD_DIM), lambda b, i: (b * nq + i, 0)),
        out_shape=jax.ShapeDtypeStruct((batch * seq, NSA_HEADS * HEAD_DIM), F32),
        scratch_shapes=[pltpu.VMEM((NSA_GROUPS, NSA_HPG * qt, LANES), F32),
                        pltpu.VMEM((NSA_GROUPS, NSA_HPG * qt, LANES), F32)],
        compiler_params=_cparams("parallel", "arbitrary"),
        name="nsa_sel_attn",
    )(p16, p32, bias, p16, p16)


def _window_attention(q_ref, gate_ref, kw_ref, vw_ref, o_ref, q0):
    seq = kw_ref.shape[0]
    span = min(WINDOW + Q_TILE, seq)
    start = pl.multiple_of(jnp.clip(q0 + Q_TILE - span, 0, seq - span), Q_TILE)
    t_row = q0 + lax.broadcasted_iota(jnp.int32, (Q_TILE, span), 0)
    pos = start + lax.broadcasted_iota(jnp.int32, (Q_TILE, span), 1)
    mask = jnp.tile(jnp.where((pos <= t_row) & (pos > t_row - WINDOW), 0.0, MASK_NEG), (NSA_HPG, 1))
    s_all = [_dot_nt(_stack_heads(q_ref, g), kw_ref[pl.ds(start, span), g * LANES:(g + 1) * LANES])
             for g in range(NSA_GROUPS)]
    p_all = []
    for g in range(NSA_GROUPS):
        s = s_all[g] + mask
        p_all.append(jnp.exp(s - jnp.max(s, axis=-1, keepdims=True)).astype(BF16))
    for g in range(NSA_GROUPS):
        acc = _dot(p_all[g], vw_ref[pl.ds(start, span), g * LANES:(g + 1) * LANES])
        _store_heads(o_ref, gate_ref, 2, g, acc / acc[:, HEAD_DIM:HEAD_DIM + 1])


def _spatial_gating_rows(u_ref, v_ref, g_ref, b_ref, w_ref, bexp_ref):
    c = SG_CHUNK
    tri = (lax.broadcasted_iota(jnp.int32, (c, c), 1) <= lax.broadcasted_iota(jnp.int32, (c, c), 0))
    lane = lax.broadcasted_iota(jnp.int32, (c, LANES), 1)
    wm = [jnp.where(tri, w_ref[g], 0.0).astype(BF16) for g in range(SG_GROUPS)]
    out = []
    for ci in range(u_ref.shape[0] // c):
        v = v_ref[ci * c:(ci + 1) * c, :]
        mu = jnp.mean(v, axis=-1, keepdims=True)
        d = v - mu
        var = jnp.mean(d * d, axis=-1, keepdims=True)
        vn = (d * lax.rsqrt(var + LN_EPS) * g_ref[...] + b_ref[...]).astype(BF16)
        mixed = []
        for pair in range(SG_GROUPS // 2):
            vp = vn[:, pair * LANES:(pair + 1) * LANES]
            mixed.append(jnp.where(lane < HEAD_DIM, _dot(wm[2 * pair], vp), _dot(wm[2 * pair + 1], vp)))
        out.append(u_ref[ci * c:(ci + 1) * c, :] * (jnp.concatenate(mixed, axis=1) + bexp_ref[...]))
    return jnp.concatenate(out, axis=0)


def _rwkv_kernel(r_ref, k_ref, v_ref, lo_ref, mu_r, mu_k, mu_v, mu_lo, w0_ref, w2_ref, a0_ref,
                 a2_ref, kk_ref, ka_ref, rk_ref, lng_ref, lnb_ref, seg_ref, o_ref,
                 state, prev_r, prev_k, prev_v, prev_lo, *, nch):
    c = RWKV_CHUNK
    rows = nch * c
    width = r_ref.shape[1]
    n_pairs = width // LANES

    @pl.when(pl.program_id(1) == 0)
    def _():
        state[...] = jnp.zeros_like(state)
        for ref in (prev_r, prev_k, prev_v, prev_lo):
            ref[...] = jnp.zeros_like(ref)

    def token_shift(x_ref, prev_ref, mu_ref):
        x = x_ref[...]
        first = lax.broadcasted_iota(jnp.int32, x.shape, 0) == 0
        prev = jnp.where(first, prev_ref[0:1, :], pltpu.roll(x, 1, 0))
        prev_ref[0:1, :] = x[rows - 1:rows, :]
        return x + (prev - x) * mu_ref[...]

    r = token_shift(r_ref, prev_r, mu_r)
    k = token_shift(k_ref, prev_k, mu_k)
    v = token_shift(v_ref, prev_v, mu_v)
    lo = token_shift(lo_ref, prev_lo, mu_lo)

    lane_lo = lax.broadcasted_iota(jnp.int32, lo.shape, 1)
    w_in = jnp.where(lane_lo < LORA, jnp.tanh(lo), 0.0).astype(BF16)
    a_in = jnp.where(lane_lo < LORA, 0.0, lo).astype(BF16)
    z = -(w0_ref[...] + _dot(w_in, w2_ref[...]))
    w = -(jnp.maximum(z, 0.0) + jnp.log(1.0 + jnp.exp(-jnp.abs(z)))) - 0.5
    a = _sigmoid(a0_ref[...] + _dot(a_in, a2_ref[...]))

    seg = seg_ref[...]
    kkf = k * kk_ref[...]
    kk = kkf / jnp.maximum(jnp.sqrt(_dot3_lhs(kkf * kkf, seg)), 1e-12)
    k = k * (1.0 + (a - 1.0) * ka_ref[...])
    logd = -jnp.exp(w)
    t_row = lax.broadcasted_iota(jnp.int32, (rows, rows), 0)
    t_col = lax.broadcasted_iota(jnp.int32, (rows, rows), 1)
    cum = (t_col <= t_row) & (t_col // c == t_row // c)
    cl = _dot3_rhs(jnp.where(cum, 1.0, 0.0).astype(BF16), logd)
    cl_end = cl.reshape(nch, c, width)[:, c - 1:c, :]
    e_neg = jnp.exp(-cl)
    e_end = jnp.exp(jnp.broadcast_to(cl_end, (nch, c, width)).reshape(rows, width) - cl)
    b = kk * a
    a_t = -kk * jnp.exp(cl - logd)
    b_t = b * e_neg
    k_t = k * e_neg
    r_t = r * jnp.exp(cl)
    b_e = b * e_end
    k_e = k * e_end
    p_end = jnp.exp(cl_end)

    lane = lax.broadcasted_iota(jnp.int32, (c, LANES), 1)
    row2 = lax.broadcasted_iota(jnp.int32, (2 * c, 2 * c), 0)
    col2 = lax.broadcasted_iota(jnp.int32, (2 * c, 2 * c), 1)
    same_head = (row2 // c) == (col2 // c)
    strict = same_head & (col2 < row2)
    incl = same_head & (col2 <= row2)
    eye = jnp.where(row2 == col2, 1.0, 0.0)
    zero16 = jnp.zeros((c, LANES), BF16)

    def stack(x, ci, pair):
        xp = x[ci * c:(ci + 1) * c, pair * LANES:(pair + 1) * LANES].astype(BF16)
        return jnp.concatenate([jnp.where(lane < HEAD_DIM, xp, zero16),
                                jnp.where(lane < HEAD_DIM, zero16, xp)], axis=0)

    def off(m):
        return (((row2 % (2 * m)) >= m) & ((col2 // (2 * m)) == (row2 // (2 * m)))
                & ((col2 % (2 * m)) < m))

    combos = [(ci, pair) for ci in range(nch) for pair in range(n_pairs)]
    a_st = {q: stack(a_t, *q) for q in combos}
    r_st = {q: stack(r_t, *q) for q in combos}
    v_st = {q: stack(v, *q) for q in combos}
    g = {q: _dot_nt(jnp.concatenate([a_st[q], r_st[q]], axis=0),
                    jnp.concatenate([stack(b_t, *q), stack(k_t, *q)], axis=0)) for q in combos}
    l_ab = {q: jnp.where(strict, g[q][0:2 * c, 0:2 * c], 0.0) for q in combos}
    inv = {q: eye + jnp.where(off(1), l_ab[q], 0.0) for q in combos}
    m = 2
    while m < c:
        inv16 = {q: inv[q].astype(BF16) for q in combos}
        x = {q: _dot(jnp.where(off(m), l_ab[q], 0.0).astype(BF16), inv16[q]) for q in combos}
        inv = {q: inv[q] + _dot(inv16[q], x[q].astype(BF16)) for q in combos}
        m *= 2
    lakv = {q: _dot(jnp.where(strict, g[q][0:2 * c, 2 * c:4 * c], 0.0).astype(BF16), v_st[q])
            for q in combos}
    wu = {q: _dot(inv[q].astype(BF16), jnp.concatenate([a_st[q], lakv[q].astype(BF16)], axis=1))
          for q in combos}
    m_rbk = {q: jnp.concatenate([jnp.where(incl, g[q][2 * c:4 * c, 0:2 * c], 0.0),
                                 jnp.where(incl, g[q][2 * c:4 * c, 2 * c:4 * c], 0.0)],
                                axis=1).astype(BF16) for q in combos}
    bke_st = {q: jnp.concatenate([stack(b_e, *q), stack(k_e, *q)], axis=0) for q in combos}

    st = [state[pair] for pair in range(n_pairs)]
    ys = [[None] * n_pairs for _ in range(nch)]
    for ci in range(nch):
        wr = {pair: _dot(jnp.concatenate([wu[ci, pair][:, 0:LANES].astype(BF16), r_st[ci, pair]],
                                         axis=0), st[pair].astype(BF16)) for pair in range(n_pairs)}
        for pair in range(n_pairs):
            q = (ci, pair)
            u_st = wr[pair][0:2 * c] + wu[q][:, LANES:2 * LANES]
            uv_st = jnp.concatenate([u_st.astype(BF16), v_st[q]], axis=0)
            y_st = wr[pair][2 * c:4 * c] + _dot(m_rbk[q], uv_st)
            decay_col = jnp.broadcast_to(p_end[ci, :, pair * LANES:(pair + 1) * LANES],
                                         (LANES, LANES)).T
            st[pair] = st[pair] * decay_col + _dot_tn(bke_st[q], uv_st)
            ys[ci][pair] = y_st[0:c] + y_st[c:2 * c]
    for pair in range(n_pairs):
        state[pair] = st[pair]
    y = jnp.concatenate([jnp.concatenate(yr, axis=1) for yr in ys], axis=0)

    seg_mean = (seg * (1.0 / HEAD_DIM)).astype(BF16)
    mean = _dot3_lhs(y, seg_mean)
    d = y - mean
    var = _dot3_lhs(d * d, seg_mean)
    yn = d * lax.rsqrt(var + GN_EPS) * lng_ref[...] + lnb_ref[...]
    bonus = _dot3_lhs(r * k * rk_ref[...], seg) * v
    o_ref[...] = yn + bonus


def _rwkv(p32, params, batch, seq, r_blk, k_blk, v_blk, lo_blk, nch=4):
    width = params["w0"].shape[1]
    c = nch * RWKV_CHUNK
    nc = seq // c
    full = lambda a: pl.BlockSpec(a.shape, lambda b, i: (0,) * a.ndim)
    names = ("mu_r", "mu_k", "mu_v", "mu_lo", "w0", "w2", "a0", "a2", "kk", "ka", "rk",
             "lng", "lnb", "seg")
    return pl.pallas_call(
        functools.partial(_rwkv_kernel, nch=nch),
        grid=(batch, nc),
        in_specs=[pl.BlockSpec((c, width), lambda b, i: (b * nc + i, r_blk)),
                  pl.BlockSpec((c, width), lambda b, i: (b * nc + i, k_blk)),
                  pl.BlockSpec((c, width), lambda b, i: (b * nc + i, v_blk)),
                  pl.BlockSpec((c, LANES), lambda b, i: (b * nc + i, lo_blk))]
                 + [full(params[n]) for n in names],
        out_specs=pl.BlockSpec((c, width), lambda b, i: (b * nc + i, 0)),
        out_shape=jax.ShapeDtypeStruct((batch * seq, width), F32),
        scratch_shapes=[pltpu.VMEM((width // LANES, LANES, LANES), F32),
                        pltpu.VMEM((8, width), F32), pltpu.VMEM((8, width), F32),
                        pltpu.VMEM((8, width), F32), pltpu.VMEM((8, LANES), F32)],
        compiler_params=_cparams("parallel", "arbitrary"),
        name="rwkv7",
    )(p32, p32, p32, p32, *[params[n] for n in names])


def _merge_kernel(ocw_ref, os_ref, su_ref, sv_ref, rk_ref, x_ref, p_ref, g_ref, wl_ref,
                  sgg_ref, sgb_ref, sgw_ref, bexp_ref,
                  wb0_ref, wb1_ref, wb2_ref, wo_ref, pleg_ref, wpg_ref, wpp_ref, fin_ref,
                  o_ref, *, last):
    d = x_ref.shape[1]
    half = rk_ref.shape[1]
    nzw = os_ref.shape[1]
    h = _rmsnorm(x_ref[...], g_ref[...]).astype(BF16)
    late = lambda c0, n: _dot(h, wl_ref[:, c0:c0 + n])
    y_nsa = ocw_ref[...] + os_ref[...]
    z0 = _dot((y_nsa * _silu(late(3 * d, nzw))).astype(BF16), wb0_ref[...])
    merged = _sigmoid(late(0, d)) * z0
    y_sg = _spatial_gating_rows(su_ref, sv_ref, sgg_ref, sgb_ref, sgw_ref, bexp_ref)
    z1 = _dot((y_sg * _silu(late(3 * d + nzw, half))).astype(BF16), wb1_ref[...])
    merged = merged + _sigmoid(late(d, d)) * z1
    z2 = _dot((rk_ref[...] * _silu(late(3 * d + nzw + half, half))).astype(BF16), wb2_ref[...])
    merged = merged + _sigmoid(late(2 * d, d)) * z2
    x1 = x_ref[...] + _dot(merged.astype(BF16), wo_ref[...])
    hp = _rmsnorm(x1, pleg_ref[...]).astype(BF16)
    x2 = x1 + _sigmoid(_dot(hp, wpg_ref[...])) * _dot(p_ref[...].astype(BF16), wpp_ref[...])
    o_ref[...] = _rmsnorm(x2, fin_ref[...]) if last else x2


def _merge(o_cw, o_s, p32, blocks, y_rk, x2, p2, consts, last, tm=512):
    t, d = x2.shape
    half = y_rk.shape[1]
    tok = lambda w, j=0: pl.BlockSpec((tm, w), lambda i, j=j: (i, j))
    full = lambda a: pl.BlockSpec(a.shape, lambda i: (0,) * a.ndim, pipeline_mode=pl.Buffered(1))
    names = ("g", "wl", "sgg", "sgb", "sgw", "bexp", "wb0", "wb1", "wb2", "wo", "pleg", "wpg",
             "wpp", "fin")
    return pl.pallas_call(
        functools.partial(_merge_kernel, last=last),
        grid=(t // tm,),
        in_specs=[tok(o_cw.shape[1]), tok(o_s.shape[1]), tok(half, blocks["su"]),
                  tok(half, blocks["sv"]), tok(half), tok(d), tok(p2.shape[1])]
                 + [full(consts[n]) for n in names],
        out_specs=tok(d),
        out_shape=jax.ShapeDtypeStruct((t, d), F32),
        compiler_params=_cparams("parallel"),
        name="merge_out_ple",
    )(o_cw, o_s, p32, p32, y_rk, x2, p2, *[consts[n] for n in names])


def _pad_cols(w, n):
    return jnp.pad(w, ((0, 0), (0, n - w.shape[1])))


def _layer_layout(d_model):
    half = d_model // 2
    kvw = NSA_GROUPS * HEAD_DIM
    sizes = dict(nq=half, nkv=6 * kvw, ngate=3 * NSA_HEADS, nz=half, su=half, sv=half, sz=half,
                 rs=3 * half + 2 * LORA, rz=half, mg=3 * d_model)
    off, out = 0, {}
    for name, n in sizes.items():
        out[name] = (off, n)
        off += n
    return out


def _prep_proj_weights(w_in, d_model):
    lay = _layer_layout(d_model)
    half = d_model // 2
    kvw = NSA_GROUPS * HEAD_DIM
    col = lambda name, a=0, n=None: w_in[:, lay[name][0] + a: lay[name][0] + a + (n or lay[name][1] - a)]
    heads = lambda w: jnp.concatenate(
        [_pad_cols(w[:, h * HEAD_DIM:(h + 1) * HEAD_DIM], LANES) for h in range(w.shape[1] // HEAD_DIM)], axis=1)
    w32 = jnp.concatenate([
        col("su"), col("sv"), col("rs", 0, half), col("rs", half, half), col("rs", 2 * half, half),
        col("rs", 3 * half, 2 * LORA), _pad_cols(col("ngate"), LANES)], axis=1)
    w_kcvc = col("nkv", 0, 2 * kvw)
    blocks, o = {}, 0
    for name in ("su", "sv", "rr", "rk", "rv"):
        blocks[name] = o // half
        o += half
    blocks["lora"] = o // LANES
    o += LANES
    blocks["gate"] = o // LANES
    w_late = jnp.concatenate([col("mg"), col("nz"), col("sz"), col("rz")], axis=1)
    zero = jnp.zeros((w_in.shape[0], LANES), w_in.dtype)
    grp = lambda w, g: _pad_cols(w[:, g * HEAD_DIM:(g + 1) * HEAD_DIM], LANES)
    ks, vs = col("nkv", 2 * kvw, kvw), col("nkv", 3 * kvw, kvw)
    kw, vw = col("nkv", 4 * kvw, kvw), col("nkv", 5 * kvw, kvw)
    w16 = jnp.concatenate(
        [heads(col("nq"))]
        + [x for g in range(NSA_GROUPS) for x in (grp(ks, g), zero)]
        + [grp(vs, g) for g in range(NSA_GROUPS)]
        + [grp(kw, g) for g in range(NSA_GROUPS)]
        + [grp(vw, g) for g in range(NSA_GROUPS)], axis=1)
    return w32.astype(BF16), w_kcvc.astype(BF16), w16.astype(BF16), w_late.astype(BF16), blocks


def _prep_compress_weights(cmp_w1, cmp_w2, cmp_pe):
    half_blk = CMP_BLOCK // 2
    w1r = cmp_w1.reshape(2, 2, half_blk, HEAD_DIM, CMP_HIDDEN)
    eye = jnp.eye(NSA_GROUPS, dtype=cmp_w1.dtype)
    eye_kv = jnp.eye(2, dtype=cmp_w1.dtype)
    big = jnp.einsum('kaldh,kK,gG->alkgdKGh', w1r, eye_kv, eye)
    big = big.reshape(2, half_blk * 2 * NSA_GROUPS * HEAD_DIM, 2 * NSA_GROUPS * CMP_HIDDEN)
    pe = cmp_pe.reshape(2, 2, half_blk, HEAD_DIM)
    pe2 = jnp.broadcast_to(pe.transpose(1, 2, 0, 3)[:, :, :, None, :],
                           (2, half_blk, 2, NSA_GROUPS, HEAD_DIM)).reshape(2, -1)
    w2p = jnp.pad(cmp_w2, ((0, 0), (0, 0), (0, LANES - HEAD_DIM)))
    return pe2, big[0].astype(BF16), big[1].astype(BF16), w2p.astype(BF16)


def _cover_t(seq):
    n_grp = seq // CMP_STRIDE
    n_slc = seq // SEL_BLOCK
    cmp_start = jnp.arange(n_grp) * CMP_STRIDE
    slc_start = jnp.arange(n_slc) * SEL_BLOCK
    cover = ((cmp_start[None, :] <= slc_start[:, None] + SEL_BLOCK - 1)
             & (cmp_start[None, :] + CMP_BLOCK - 1 >= slc_start[:, None]))
    return jnp.pad(cover, ((0, LANES - n_slc), (0, 0))).astype(BF16)


def kernel(x, p, norm_g, w_in, cmp_w1, cmp_w2, cmp_pe, sg_ln_g, sg_ln_b, sg_w, sg_b, rk_mu, rk_w0,
           rk_w2, rk_a0, rk_a2, rk_kk, rk_ka, rk_rk, rk_lnx_g, rk_lnx_b, w_branch, w_o, ple_norm_g,
           w_ple_gate, w_ple_proj, final_norm_g):
    batch, seq, d = x.shape
    depth = w_in.shape[0]
    half = d // 2
    t = batch * seq
    assert seq % (2 * SEL_KEY_TILE) == 0 and (seq // CMP_STRIDE) % CMP_WIDTH_STEP == 0
    assert seq >= WINDOW + Q_TILE and seq // SEL_BLOCK <= LANES
    x2 = x.reshape(t, d)
    cov_t = _cover_t(seq)
    seg = (jnp.arange(half)[:, None] // HEAD_DIM == jnp.arange(half)[None, :] // HEAD_DIM).astype(BF16)
    row = lambda v: v.reshape(1, -1)
    for i in range(depth):
        w32, w_kcvc, w16, w_late, blk = _prep_proj_weights(w_in[i], d)
        g = row(norm_g[i])
        p32, kc_raw, vc_raw, p16 = _proj(x2, g, w32, w_kcvc, w16, seq)

        pe2, wa, wb, w2p = _prep_compress_weights(cmp_w1[i], cmp_w2[i], cmp_pe[i])
        kc, vc = _compress(kc_raw.reshape(batch, seq, -1), vc_raw.reshape(batch, seq, -1),
                           pe2, wa, wb, w2p)
        o_cw, bias = _cmp_select(p16, p32, blk["gate"], kc, vc, cov_t, batch, seq)
        o_s = _sel_attn(p16, p32, blk["gate"], bias, batch, seq)

        mu = rk_mu[i]
        rk_params = dict(
            mu_r=row(mu[0:half]), mu_k=row(mu[half:2 * half]), mu_v=row(mu[2 * half:3 * half]),
            mu_lo=row(mu[3 * half:]), w0=row(rk_w0[i]),
            w2=jnp.pad(rk_w2[i], ((0, LANES - LORA), (0, 0))).astype(BF16), a0=row(rk_a0[i]),
            a2=jnp.pad(rk_a2[i], ((LANES - LORA, 0), (0, 0))).astype(BF16),
            kk=row(rk_kk[i]), ka=row(rk_ka[i]), rk=row(rk_rk[i]), lng=row(rk_lnx_g[i]),
            lnb=row(rk_lnx_b[i]), seg=seg)
        y_rk = _rwkv(p32, rk_params, batch, seq, blk["rr"], blk["rk"], blk["rv"], blk["lora"])

        consts = dict(g=g, wl=w_late, sgg=row(sg_ln_g[i]), sgb=row(sg_ln_b[i]), sgw=sg_w[i],
                      bexp=jnp.repeat(sg_b[i].T, half // SG_GROUPS, axis=1),
                      wb0=w_branch[i, 0].astype(BF16), wb1=w_branch[i, 1].astype(BF16),
                      wb2=w_branch[i, 2].astype(BF16), wo=w_o[i].astype(BF16),
                      pleg=row(ple_norm_g[i]), wpg=w_ple_gate[i].astype(BF16),
                      wpp=w_ple_proj[i].astype(BF16), fin=row(final_norm_g))
        x2 = _merge(o_cw, o_s, p32, blk, y_rk, x2, p[i].reshape(t, -1), consts,
                    last=(i == depth - 1))
    return x2.reshape(batch, seq, d)
```
